```python
import math
import jax, jax.numpy as jnp
from jax import lax
import numpy as np

D_MODEL = 1024
BATCH = 4
SEQ = 4096
DEPTH = 2
DEC_BATCH = 32
DEC_SEQ = 32
PAST_LEN = 4096

CHUNK = 64
PLE_DIM = 256
EPS = 1e-6

GLA_H = 4
GLA_DK = 32
GLA_DV = 64
GLA_RANK = 16
GLA_TAU = 16.0
RET_H = 4
RET_DK = 32
RET_DV = 64
ROPE_BASE = 10000.0
GDN_H = 4
GDN_DK = 128
GDN_DV = 128
GDN_CONV_W = 4

GLA_KW = GLA_H * GLA_DK
GLA_VW = GLA_H * GLA_DV
RET_KW = RET_H * RET_DK
RET_VW = RET_H * RET_DV
GDN_KW = GDN_H * GDN_DK
GDN_VW = GDN_H * GDN_DV
GDN_CONV_CH = 2 * GDN_KW + GDN_VW
MIX_WIDTH = GLA_VW + RET_VW + GDN_VW

IN_SIZES = (GLA_KW, GLA_KW, GLA_VW, GLA_RANK, GLA_VW,
            RET_KW, RET_KW, RET_VW, RET_VW,
            GDN_CONV_CH, GDN_H, GDN_H, GDN_VW)
IN_WIDTH = sum(IN_SIZES)

D_FF = 3584
N_EXPERTS = 8
TOP_K = 2
N_DENSE_LAYERS = (DEPTH + 1) // 2
N_MOE_LAYERS = DEPTH // 2

kernel_name = "hybrid_gla_retnet_gdn_stream_step"


def rmsnorm(x, g):
    xf = x.astype(jnp.float32)
    y = xf * lax.rsqrt(jnp.mean(xf * xf, axis=-1, keepdims=True) + EPS)
    return (y * g.astype(jnp.float32)).astype(x.dtype)


def l2norm(x):
    return x * lax.rsqrt(jnp.sum(x * x, axis=-1, keepdims=True) + EPS)


def rotary(x, pos):
    half = x.shape[-1] // 2
    inv = ROPE_BASE ** (-jnp.arange(half, dtype=jnp.float32) / half)
    ang = pos.astype(jnp.float32)[:, None] * inv[None, :]
    cos = jnp.cos(ang)[None, :, None, :]
    sin = jnp.sin(ang)[None, :, None, :]
    x1, x2 = x[..., :half], x[..., half:]
    return jnp.concatenate([x1 * cos - x2 * sin, x1 * sin + x2 * cos], axis=-1)


def block_len(T):
    return T if T <= CHUNK else CHUNK


def to_chunks(a, L):
    B, T, H, d = a.shape
    return a.reshape(B, T // L, L, H, d).transpose(1, 0, 3, 2, 4)


def from_chunks(a):
    nc, B, H, L, d = a.shape
    return a.transpose(1, 0, 3, 2, 4).reshape(B, nc * L, H, d)


def decay_linear_attention(q, k, v, logdecay, s0):
    L = block_len(q.shape[1])
    mask = jnp.tril(jnp.ones((L, L), dtype=bool))

    def step(S, inp):
        qc, kc, vc, gc = inp
        b = jnp.cumsum(gc, axis=2)
        diff = b[:, :, :, None, :] - b[:, :, None, :, :]
        dec = jnp.exp(jnp.where(mask[:, :, None], diff, -jnp.inf))
        att = jnp.einsum('bhtd,bhsd,bhtsd->bhts', qc, kc, dec)
        o = (jnp.einsum('bhtd,bhde->bhte', qc * jnp.exp(b), S)
             + jnp.einsum('bhts,bhse->bhte', att, vc))
        bl = b[:, :, -1:, :]
        S_new = (jnp.exp(bl[:, :, 0, :])[..., None] * S
                 + jnp.einsum('bhsd,bhse->bhde', kc * jnp.exp(bl - b), vc))
        return S_new, o

    S, o = lax.scan(step, s0, (to_chunks(q, L), to_chunks(k, L), to_chunks(v, L), to_chunks(logdecay, L)))
    return from_chunks(o), S


def gated_delta_rule(q, k, v, g, beta, s0):
    L = block_len(q.shape[1])
    tril = jnp.tril(jnp.ones((L, L), dtype=bool))
    strict = jnp.tril(jnp.ones((L, L), dtype=bool), -1)
    eye = jnp.eye(L, dtype=jnp.float32)

    def step(S, inp):
        qc, kc, vc, gc, bc = inp
        b = jnp.cumsum(gc, axis=-1)
        diff = b[..., :, None] - b[..., None, :]
        dec_incl = jnp.exp(jnp.where(tril, diff, -jnp.inf))
        dec_strict = jnp.where(strict, dec_incl, 0.0)
        m = jnp.einsum('bhtd,bhsd->bhts', kc, kc) * dec_strict * bc[..., None, :]
        rhs = vc - jnp.exp(b)[..., None] * jnp.einsum('bhtd,bhde->bhte', kc, S)
        u = lax.linalg.triangular_solve(eye + m, rhs, left_side=True, lower=True, unit_diagonal=True)
        qk = jnp.einsum('bhtd,bhsd->bhts', qc, kc) * dec_incl * bc[..., None, :]
        o = (jnp.exp(b)[..., None] * jnp.einsum('bhtd,bhde->bhte', qc, S)
             + jnp.einsum('bhts,bhse->bhte', qk, u))
        bl = b[..., -1:]
        S_new = (jnp.exp(bl)[..., None] * S
                 + jnp.einsum('bhsd,bhse->bhde', kc * (jnp.exp(bl - b) * bc)[..., None], u))
        return S_new, o

    g4 = to_chunks(g[..., None], L)[..., 0]
    b4 = to_chunks(beta[..., None], L)[..., 0]
    S, o = lax.scan(step, s0, (to_chunks(q, L), to_chunks(k, L), to_chunks(v, L), g4, b4))
    return from_chunks(o), S


def mixer_block(u, pos, s_gla, s_ret, s_gdn, conv_buf, w_in, gla_w_gate_up, gla_b_gate, gla_norm,
                ret_norm, gdn_conv, gdn_a_log, gdn_dt_bias, gdn_norm, w_out):
    B, T, _ = u.shape
    f32 = jnp.float32
    z = u @ w_in
    points = np.cumsum(np.array(IN_SIZES))[:-1].tolist()
    (gq, gk, gv, glr, gr, rq, rk, rv, rr, dqkv, da, db, dz) = jnp.split(z, points, axis=-1)

    def heads(a, h):
        return a.astype(f32).reshape(B, T, h, -1)

    q = heads(gq, GLA_H) * (GLA_DK ** -0.5)
    k = heads(gk, GLA_H)
    v = heads(gv, GLA_H)
    la = jax.nn.log_sigmoid(glr.astype(f32) @ gla_w_gate_up.astype(f32) + gla_b_gate.astype(f32)) / GLA_TAU
    o_gla, s_gla_new = decay_linear_attention(q, k, v, la.reshape(B, T, GLA_H, GLA_DK), s_gla)
    o_gla = rmsnorm(o_gla, gla_norm).reshape(B, T, GLA_VW) * jax.nn.silu(gr.astype(f32))

    q = rotary(heads(rq, RET_H), pos) * (RET_DK ** -0.5)
    k = rotary(heads(rk, RET_H), pos)
    v = heads(rv, RET_H)
    log_gamma = jnp.log(1.0 - jnp.exp2(-5.0 - jnp.arange(RET_H, dtype=f32)))
    ld = jnp.broadcast_to(log_gamma[None, None, :, None], (B, T, RET_H, RET_DK))
    o_ret, s_ret_new = decay_linear_attention(q, k, v, ld, s_ret)
    o_ret = rmsnorm(o_ret, ret_norm).reshape(B, T, RET_VW) * jax.nn.silu(rr.astype(f32))

    padded = jnp.concatenate([conv_buf.astype(dqkv.dtype), dqkv], axis=1)
    conv_new = padded[:, -(GDN_CONV_W - 1):]
    c = lax.conv_general_dilated(padded.astype(f32), gdn_conv.astype(f32)[:, None, :], (1,), 'VALID',
                                 dimension_numbers=('NWC', 'WIO', 'NWC'),
                                 feature_group_count=GDN_CONV_CH)
    c = jax.nn.silu(c)
    cq, ck, cv = jnp.split(c, [GDN_KW, 2 * GDN_KW], axis=-1)
    q = l2norm(cq.reshape(B, T, GDN_H, GDN_DK)) * (GDN_DK ** -0.5)
    k = l2norm(ck.reshape(B, T, GDN_H, GDN_DK))
    v = cv.reshape(B, T, GDN_H, GDN_DV)
    g = -jnp.exp(gdn_a_log.astype(f32)) * jax.nn.softplus(da.astype(f32) + gdn_dt_bias.astype(f32))
    beta = jax.nn.sigmoid(db.astype(f32))
    o_gdn, s_gdn_new = gated_delta_rule(q, k, v, g, beta, s_gdn)
    o_gdn = rmsnorm(o_gdn, gdn_norm).reshape(B, T, GDN_VW) * jax.nn.silu(dz.astype(f32))

    o = jnp.concatenate([o_gla, o_ret, o_gdn], axis=-1).astype(u.dtype)
    return o @ w_out, s_gla_new, s_ret_new, s_gdn_new, conv_new


def swiglu(x, wg, wu, wd):
    return (jax.nn.silu(x @ wg) * (x @ wu)) @ wd


def moe_swiglu(x, router, wg, wu, wd):
    logits = (x @ router).astype(jnp.float32)
    top_v, top_i = lax.top_k(logits, TOP_K)
    w = jax.nn.softmax(top_v, axis=-1)
    gates = jnp.sum(jax.nn.one_hot(top_i, N_EXPERTS, dtype=jnp.float32) * w[..., None], axis=-2)
    out = jnp.zeros(x.shape, jnp.float32)
    for e in range(N_EXPERTS):
        out = out + gates[..., e:e + 1] * swiglu(x, wg[e], wu[e], wd[e]).astype(jnp.float32)
    return out.astype(x.dtype)


def trunk(x, p, pos, st_gla, st_ret, st_gdn, st_conv, params):
    (norm_mix, w_in, gla_w_gate_up, gla_b_gate, gla_norm, ret_norm, gdn_conv, gdn_a_log, gdn_dt_bias,
     gdn_norm, w_out, norm_ffn, ffn_w_gate, ffn_w_up, ffn_w_down, moe_router, moe_w_gate, moe_w_up,
     moe_w_down, ple_w_up, ple_norm, ple_w_gate, norm_final) = params
    B = x.shape[0]
    h = x
    out_gla, out_ret, out_gdn, out_conv = [], [], [], []
    for i in range(DEPTH):
        if st_gla is None:
            s_gla = jnp.zeros((B, GLA_H, GLA_DK, GLA_DV), jnp.float32)
            s_ret = jnp.zeros((B, RET_H, RET_DK, RET_DV), jnp.float32)
            s_gdn = jnp.zeros((B, GDN_H, GDN_DK, GDN_DV), jnp.float32)
            cbuf = jnp.zeros((B, GDN_CONV_W - 1, GDN_CONV_CH), x.dtype)
        else:
            s_gla = st_gla[i].astype(jnp.float32)
            s_ret = st_ret[i].astype(jnp.float32)
            s_gdn = st_gdn[i].astype(jnp.float32)
            cbuf = st_conv[i]
        u = rmsnorm(h, norm_mix[i])
        mix, s_gla, s_ret, s_gdn, cbuf = mixer_block(
            u, pos, s_gla, s_ret, s_gdn, cbuf, w_in[i], gla_w_gate_up[i], gla_b_gate[i], gla_norm[i],
            ret_norm[i], gdn_conv[i], gdn_a_log[i], gdn_dt_bias[i], gdn_norm[i], w_out[i])
        h = h + mix
        v = rmsnorm(h, norm_ffn[i])
        if i % 2 == 0:
            h = h + swiglu(v, ffn_w_gate[i // 2], ffn_w_up[i // 2], ffn_w_down[i // 2])
        else:
            h = h + moe_swiglu(v, moe_router[i // 2], moe_w_gate[i // 2], moe_w_up[i // 2], moe_w_down[i // 2])
        h = h + (p[i] @ ple_w_up[i]) * jax.nn.sigmoid(rmsnorm(h, ple_norm[i]) @ ple_w_gate[i])
        out_gla.append(s_gla.astype(x.dtype))
        out_ret.append(s_ret.astype(x.dtype))
        out_gdn.append(s_gdn.astype(x.dtype))
        out_conv.append(cbuf.astype(x.dtype))
    y = rmsnorm(h, norm_final)
    return y, jnp.stack(out_gla), jnp.stack(out_ret), jnp.stack(out_gdn), jnp.stack(out_conv)


def setup_inputs(seed: int = 0) -> dict:
    key = jax.random.key(seed)
    ks = jax.random.split(key, 40)
    f32 = jnp.float32

    def nrm(k, shape, scale):
        return jax.random.normal(k, shape, f32) * scale

    def gain(k, shape):
        return 1.0 + 0.02 * jax.random.normal(k, shape, f32)

    dt = jnp.exp(jax.random.uniform(ks[18], (DEPTH, GDN_H), f32, math.log(0.001), math.log(0.1)))
    return {
        "x_prompt": nrm(ks[0], (BATCH, SEQ, D_MODEL), 1.0),
        "x_sample": nrm(ks[1], (DEC_BATCH, DEC_SEQ, D_MODEL), 1.0),
        "state_gla": nrm(ks[2], (DEPTH, DEC_BATCH, GLA_H, GLA_DK, GLA_DV), 1.0),
        "state_ret": nrm(ks[3], (DEPTH, DEC_BATCH, RET_H, RET_DK, RET_DV), 1.0),
        "state_gdn": nrm(ks[4], (DEPTH, DEC_BATCH, GDN_H, GDN_DK, GDN_DV), 1.0),
        "state_gdn_conv": nrm(ks[5], (DEPTH, DEC_BATCH, GDN_CONV_W - 1, GDN_CONV_CH), 1.0),
        "p_prompt": nrm(ks[6], (DEPTH, BATCH, SEQ, PLE_DIM), 1.0),
        "p_sample": nrm(ks[7], (DEPTH, DEC_BATCH, DEC_SEQ, PLE_DIM), 1.0),
        "norm_mix": gain(ks[8], (DEPTH, D_MODEL)),
        "w_in": nrm(ks[9], (DEPTH, D_MODEL, IN_WIDTH), D_MODEL ** -0.5),
        "gla_w_gate_up": nrm(ks[10], (DEPTH, GLA_RANK, GLA_KW), GLA_RANK ** -0.5),
        "gla_b_gate": nrm(ks[11], (DEPTH, GLA_KW), 0.1),
        "gla_norm": gain(ks[12], (DEPTH, GLA_DV)),
        "ret_norm": gain(ks[13], (DEPTH, RET_DV)),
        "gdn_conv": nrm(ks[14], (DEPTH, GDN_CONV_W, GDN_CONV_CH), GDN_CONV_W ** -0.5),
        "gdn_a_log": jnp.log(jax.random.uniform(ks[15], (DEPTH, GDN_H), f32, 1.0, 16.0)),
        "gdn_dt_bias": dt + jnp.log(-jnp.expm1(-dt)),
        "gdn_norm": gain(ks[16], (DEPTH, GDN_DV)),
        "w_out": nrm(ks[17], (DEPTH, MIX_WIDTH, D_MODEL), MIX_WIDTH ** -0.5),
        "norm_ffn": gain(ks[19], (DEPTH, D_MODEL)),
        "ffn_w_gate": nrm(ks[20], (N_DENSE_LAYERS, D_MODEL, D_FF), D_MODEL ** -0.5),
        "ffn_w_up": nrm(ks[21], (N_DENSE_LAYERS, D_MODEL, D_FF), D_MODEL ** -0.5),
        "ffn_w_down": nrm(ks[22], (N_DENSE_LAYERS, D_FF, D_MODEL), D_FF ** -0.5),
        "moe_router": nrm(ks[23], (N_MOE_LAYERS, D_MODEL, N_EXPERTS), D_MODEL ** -0.5),
        "moe_w_gate": nrm(ks[24], (N_MOE_LAYERS, N_EXPERTS, D_MODEL, D_FF), D_MODEL ** -0.5),
        "moe_w_up": nrm(ks[25], (N_MOE_LAYERS, N_EXPERTS, D_MODEL, D_FF), D_MODEL ** -0.5),
        "moe_w_down": nrm(ks[26], (N_MOE_LAYERS, N_EXPERTS, D_FF, D_MODEL), D_FF ** -0.5),
        "ple_w_up": nrm(ks[27], (DEPTH, PLE_DIM, D_MODEL), PLE_DIM ** -0.5),
        "ple_norm": gain(ks[28], (DEPTH, D_MODEL)),
        "ple_w_gate": nrm(ks[29], (DEPTH, D_MODEL, D_MODEL), D_MODEL ** -0.5),
        "norm_final": gain(ks[30], (D_MODEL,)),
    }


def reference(x_prompt, x_sample, state_gla, state_ret, state_gdn, state_gdn_conv, p_prompt, p_sample,
              norm_mix, w_in, gla_w_gate_up, gla_b_gate, gla_norm, ret_norm, gdn_conv, gdn_a_log,
              gdn_dt_bias, gdn_norm, w_out, norm_ffn, ffn_w_gate, ffn_w_up, ffn_w_down, moe_router,
              moe_w_gate, moe_w_up, moe_w_down, ple_w_up, ple_norm, ple_w_gate, norm_final):
    params = (norm_mix, w_in, gla_w_gate_up, gla_b_gate, gla_norm, ret_norm, gdn_conv, gdn_a_log,
              gdn_dt_bias, gdn_norm, w_out, norm_ffn, ffn_w_gate, ffn_w_up, ffn_w_down, moe_router,
              moe_w_gate, moe_w_up, moe_w_down, ple_w_up, ple_norm, ple_w_gate, norm_final)
    pos_prompt = jnp.arange(x_prompt.shape[1], dtype=jnp.int32)
    pos_sample = PAST_LEN + jnp.arange(x_sample.shape[1], dtype=jnp.int32)
    y_prompt, gla_p, ret_p, gdn_p, conv_p = trunk(x_prompt, p_prompt, pos_prompt, None, None, None, None, params)
    y_sample, gla_s, ret_s, gdn_s, conv_s = trunk(x_sample, p_sample, pos_sample, state_gla, state_ret,
                                                  state_gdn, state_gdn_conv, params)
    return (y_prompt, y_sample, gla_p, ret_p, gdn_p, conv_p, gla_s, ret_s, gdn_s, conv_s)
```

```python
import functools
import math

import jax
import jax.numpy as jnp
import numpy as np
from jax import lax
from jax.experimental import pallas as pl
from jax.experimental.pallas import tpu as pltpu

F32 = jnp.float32
BF16 = jnp.bfloat16
HIGHEST = lax.Precision.HIGHEST

D_MODEL = 1024
CHUNK = 64
PLE_DIM = 256
EPS = 1e-6
PAST_LEN = 4096

GLA_H, GLA_DK, GLA_DV, GLA_RANK, GLA_TAU = 4, 32, 64, 16, 16.0
RET_H, RET_DK, RET_DV = 4, 32, 64
ROPE_BASE = 10000.0
GDN_H, GDN_DK, GDN_DV, GDN_CONV_W = 4, 128, 128, 4
GLA_KW, GLA_VW = GLA_H * GLA_DK, GLA_H * GLA_DV
RET_KW, RET_VW = RET_H * RET_DK, RET_H * RET_DV
GDN_KW, GDN_VW = GDN_H * GDN_DK, GDN_H * GDN_DV
GDN_CONV_CH = 2 * GDN_KW + GDN_VW
D_FF = 3584
N_EXPERTS = 8

LANES = 128

Z_GQ, Z_GK, Z_GV, Z_GR = 0, 128, 256, 512
Z_RQ, Z_RK, Z_RV, Z_RR = 768, 896, 1024, 1280
Z_CONV = 1536
Z_DZ = Z_CONV + GDN_CONV_CH
Z_SMALL = Z_DZ + GDN_VW
Z_WIDTH = Z_SMALL + LANES
SM_DA, SM_DB = GLA_RANK, GLA_RANK + GDN_H

VMEM_LIMIT = 56 * 1024 * 1024


def _cparams(sem):
    return pltpu.CompilerParams(dimension_semantics=sem, vmem_limit_bytes=VMEM_LIMIT)


def _rms(x, g):
    return x * lax.rsqrt(jnp.mean(x * x, axis=-1, keepdims=True) + EPS) * g


def _bdot(a, b):
    return jnp.dot(a.astype(BF16), b.astype(BF16), preferred_element_type=F32)


def _bdot_t(a, b):
    return lax.dot_general(a.astype(BF16), b.astype(BF16), (((1,), (1,)), ((), ())),
                           preferred_element_type=F32)


def _hdot(a, b):
    return jnp.dot(a, b, precision=HIGHEST, preferred_element_type=F32)


def _silu(x):
    return x * jax.nn.sigmoid(x)


def _norm_proj_kernel(h_ref, g_ref, w_ref, z_ref):
    u = _rms(h_ref[...], g_ref[...])
    z_ref[...] = jnp.dot(u.astype(BF16), w_ref[...], preferred_element_type=F32)


def _norm_proj(h, g, w, tm):
    n = h.shape[0]
    return pl.pallas_call(
        _norm_proj_kernel,
        out_shape=jax.ShapeDtypeStruct((n, w.shape[1]), F32),
        grid=(n // tm,),
        in_specs=[pl.BlockSpec((tm, D_MODEL), lambda i: (i, 0)),
                  pl.BlockSpec((1, D_MODEL), lambda i: (0, 0)),
                  pl.BlockSpec(w.shape, lambda i: (0, 0))],
        out_specs=pl.BlockSpec((tm, w.shape[1]), lambda i: (i, 0)),
        compiler_params=_cparams(("parallel",)),
        name="norm_proj",
    )(h, g, w)


def _mixer_kernel(L,
                  z_ref, cos_ref, sin_ref, retd_ref, reteb_ref, retebl_ref, retsdec_ref,
                  sgla0_ref, sret0_ref, sgdn0_ref, conv0_ref,
                  wgu_ref, gbias_ref, alog_ref, dtb_ref, glan_ref, retn_ref, gdnn_ref, convw_ref,
                  indv_ref, gn64_ref, blkmask_ref, hmk_ref, hmv_ref,
                  o_ref, sgla_ref, sret_ref, sgdn_ref, convo_ref,
                  qs_ref, bs_ref, p_ref, xpad_ref):
    c = pl.program_id(1)

    @pl.when(c == 0)
    def _():
        sgla_ref[...] = sgla0_ref[...]
        sret_ref[...] = sret0_ref[...]
        sgdn_ref[...] = sgdn0_ref[...]
        convo_ref[...] = conv0_ref[...]

    row = lax.broadcasted_iota(jnp.int32, (L, L), 0)
    col = lax.broadcasted_iota(jnp.int32, (L, L), 1)
    tril = row >= col
    tri_f = tril.astype(F32)
    eye_f = (row == col).astype(F32)
    eye128 = (lax.broadcasted_iota(jnp.int32, (LANES, LANES), 0)
              == lax.broadcasted_iota(jnp.int32, (LANES, LANES), 1)).astype(F32)
    blkmask = blkmask_ref[...]

    small = z_ref[:, Z_SMALL:Z_SMALL + LANES]

    def row_to_col(r):
        return jnp.sum(eye128 * r, axis=1, keepdims=True)

    x = _bdot(small, wgu_ref[...]) + gbias_ref[...]
    la = jax.nn.log_sigmoid(x) * (1.0 / GLA_TAU)
    b = _hdot(tri_f, la)
    q = z_ref[:, Z_GQ:Z_GQ + GLA_KW] * (GLA_DK ** -0.5)
    k = z_ref[:, Z_GK:Z_GK + GLA_KW]
    v = z_ref[:, Z_GV:Z_GV + GLA_VW]
    qs_ref[...] = q
    bs_ref[...] = b
    srow = lax.broadcasted_iota(jnp.int32, (L, LANES), 0)

    def p_body(t, carry):
        bt = bs_ref[pl.ds(t, 1), :]
        qt = qs_ref[pl.ds(t, 1), :]
        e = jnp.exp(jnp.where(srow <= t, bt - b, -jnp.inf))
        p_ref[pl.ds(pl.multiple_of(t * L, L), L), :] = (k * qt * e).astype(BF16)
        return carry

    lax.fori_loop(0, L, p_body, 0)
    a = jnp.dot(p_ref[...], indv_ref[...], preferred_element_type=F32)
    o_intra = jnp.sum(a.reshape(L, L, GLA_VW) * v[None, :, :], axis=1)
    s_gla = sgla_ref[...]
    o_gla = o_intra + _bdot(q * jnp.exp(b), s_gla)
    bl = b[L - 1:L, :]
    kt = k * jnp.exp(bl - b)
    sgla_ref[...] = s_gla * row_to_col(jnp.exp(bl)) + blkmask * _bdot(kt.T, v)
    ms = _hdot(o_gla * o_gla, gn64_ref[...])
    o_gla = o_gla * lax.rsqrt(ms + EPS) * glan_ref[...] * _silu(z_ref[:, Z_GR:Z_GR + GLA_VW])
    o_ref[:, 0:GLA_VW] = o_gla.astype(o_ref.dtype)

    cos = cos_ref[...]
    sin = sin_ref[...]
    lane = lax.broadcasted_iota(jnp.int32, (L, LANES), 1)
    first_half = (lane % RET_DK) < (RET_DK // 2)

    def rot(xx):
        sw = jnp.where(first_half, pltpu.roll(xx, LANES - RET_DK // 2, 1), pltpu.roll(xx, RET_DK // 2, 1))
        return xx * cos + sw * sin

    q = rot(z_ref[:, Z_RQ:Z_RQ + RET_KW]) * (RET_DK ** -0.5)
    k = rot(z_ref[:, Z_RK:Z_RK + RET_KW])
    v = z_ref[:, Z_RV:Z_RV + RET_VW]
    kblk = jnp.concatenate([k] * RET_H, axis=0) * hmk_ref[...]
    att = _bdot_t(q, kblk) * retd_ref[...]
    vblk = jnp.concatenate([v] * RET_H, axis=0) * hmv_ref[...]
    s_ret = sret_ref[...]
    o_ret = _bdot(att, vblk) + _bdot(q * reteb_ref[...], s_ret)
    kt = k * retebl_ref[...]
    sret_ref[...] = s_ret * retsdec_ref[...] + blkmask * _bdot(kt.T, v)
    ms = _hdot(o_ret * o_ret, gn64_ref[...])
    o_ret = o_ret * lax.rsqrt(ms + EPS) * retn_ref[...] * _silu(z_ref[:, Z_RR:Z_RR + RET_VW])
    o_ref[:, GLA_VW:GLA_VW + RET_VW] = o_ret.astype(o_ref.dtype)

    xin = z_ref[:, Z_CONV:Z_CONV + GDN_CONV_CH]
    xpad_ref[8 - (GDN_CONV_W - 1):8, :] = convo_ref[...]
    xpad_ref[8:8 + L, :] = xin
    convo_ref[...] = xin[L - (GDN_CONV_W - 1):L, :]
    cw = convw_ref[...]
    cacc = xin * cw[GDN_CONV_W - 1:GDN_CONV_W, :]
    for j in range(GDN_CONV_W - 1):
        sh = GDN_CONV_W - 1 - j
        cacc = cacc + xpad_ref[8 - sh:8 - sh + L, :] * cw[j:j + 1, :]
    cact = _silu(cacc)

    g_all = -jnp.exp(alog_ref[...]) * jax.nn.softplus(small + dtb_ref[...])
    beta_all = jax.nn.sigmoid(small)
    bg = _hdot(tri_f, g_all)
    bg_t = bg.T
    beta_t = beta_all.T
    strict = row > col
    gdnn = gdnn_ref[...]

    def l2n(xx):
        return xx * lax.rsqrt(jnp.sum(xx * xx, axis=-1, keepdims=True) + EPS)

    for h in range(GDN_H):
        qh = l2n(cact[:, h * GDN_DK:(h + 1) * GDN_DK]) * (GDN_DK ** -0.5)
        kh = l2n(cact[:, GDN_KW + h * GDN_DK:GDN_KW + (h + 1) * GDN_DK])
        vh = cact[:, 2 * GDN_KW + h * GDN_DV:2 * GDN_KW + (h + 1) * GDN_DV]
        bcol = bg[:, SM_DA + h:SM_DA + h + 1]
        brow = bg_t[SM_DA + h:SM_DA + h + 1, :]
        beta_row = beta_t[SM_DB + h:SM_DB + h + 1, :]
        beta_col = beta_all[:, SM_DB + h:SM_DB + h + 1]
        dec = jnp.exp(jnp.where(tril, bcol - brow, -jnp.inf))
        m = _bdot_t(kh, kh) * jnp.where(strict, dec, 0.0) * beta_row
        tinv = eye_f - m
        pw = m
        span = 2
        while span < L:
            pw = _hdot(pw, pw)
            tinv = tinv + _hdot(tinv, pw)
            span *= 2
        s_h = sgdn_ref[h]
        ebc = jnp.exp(bcol)
        rhs = vh - ebc * _bdot(kh, s_h)
        u = _hdot(tinv, rhs)
        qk = _bdot_t(qh, kh) * dec * beta_row
        oh = ebc * _bdot(qh, s_h) + _bdot(qk, u)
        blg = bcol[L - 1:L, :]
        kt = kh * (jnp.exp(blg - bcol) * beta_col)
        sgdn_ref[h] = jnp.exp(blg) * s_h + _bdot(kt.T, u)
        oh = _rms(oh, gdnn) * _silu(z_ref[:, Z_DZ + h * GDN_DV:Z_DZ + (h + 1) * GDN_DV])
        lo = GLA_VW + RET_VW + h * GDN_DV
        o_ref[:, lo:lo + GDN_DV] = oh.astype(o_ref.dtype)


def _head_block_mask(rows_per_head, cols_per_head, heads):
    r = np.arange(rows_per_head * heads)[:, None] // rows_per_head
    c = np.arange(cols_per_head * heads)[None, :] // cols_per_head
    return (r == c).astype(np.float32)


def _mixer(z, pos0, sgla0, sret0, sgdn0, conv0, prm):
    B, T, _ = z.shape
    L = T if T <= CHUNK else CHUNK
    nc = T // L
    half = RET_DK // 2
    inv = ROPE_BASE ** (-jnp.arange(half, dtype=F32) / half)
    ang = (pos0 + jnp.arange(T, dtype=jnp.int32)).astype(F32)[:, None] * inv[None, :]
    cos_h = jnp.concatenate([jnp.cos(ang), jnp.cos(ang)], axis=1)
    sin_h = jnp.concatenate([-jnp.sin(ang), jnp.sin(ang)], axis=1)
    cos_t = jnp.tile(cos_h, (1, RET_H))
    sin_t = jnp.tile(sin_h, (1, RET_H))
    log_gamma = jnp.log(1.0 - jnp.exp2(-5.0 - jnp.arange(RET_H, dtype=F32)))
    tpos = jnp.arange(L, dtype=F32)
    bret = (tpos[:, None] + 1.0) * log_gamma[None, :]
    dmat = tpos[:, None] - tpos[None, :]
    retd = jnp.where(dmat[None] >= 0, jnp.exp(dmat[None] * log_gamma[:, None, None]), 0.0)
    retd = jnp.transpose(retd, (1, 0, 2)).reshape(L, RET_H * L)
    reteb = jnp.repeat(jnp.exp(bret), RET_DK, axis=1)
    retebl = jnp.repeat(jnp.exp(bret[L - 1:L] - bret), RET_DK, axis=1)
    retsdec = jnp.broadcast_to(jnp.repeat(jnp.exp(bret[L - 1]), RET_DK)[:, None], (RET_KW, RET_VW))

    indv = jnp.asarray(_head_block_mask(GLA_DK, GLA_DV, GLA_H), BF16)
    gn64 = jnp.asarray(_head_block_mask(GLA_DV, GLA_DV, GLA_H) / GLA_DV, F32)
    blkmask = jnp.asarray(_head_block_mask(GLA_DK, GLA_DV, GLA_H), F32)
    hmk = jnp.asarray(_head_block_mask(L, RET_DK, RET_H), F32)
    hmv = jnp.asarray(_head_block_mask(L, RET_DV, RET_H), F32)

    def full(a):
        nd = a.ndim
        return pl.BlockSpec(a.shape, lambda b, c: (0,) * nd)

    def per_b(a):
        nd = a.ndim
        return pl.BlockSpec((None,) + a.shape[1:], lambda b, c: (b,) + (0,) * (nd - 1))

    consts = [retd, reteb, retebl, retsdec]
    states = [sgla0, sret0, sgdn0, conv0]
    params = [prm["wgu"], prm["gbias"], prm["alog"], prm["dtb"], prm["glan"], prm["retn"], prm["gdnn"],
              prm["convw"], indv, gn64, blkmask, hmk, hmv]
    in_specs = ([pl.BlockSpec((None, L, Z_WIDTH), lambda b, c: (b, c, 0)),
                 pl.BlockSpec((L, LANES), lambda b, c: (c, 0)),
                 pl.BlockSpec((L, LANES), lambda b, c: (c, 0))]
                + [full(a) for a in consts] + [per_b(a) for a in states] + [full(a) for a in params])
    out_shape = [jax.ShapeDtypeStruct((B, T, D_MODEL), BF16)] + [
        jax.ShapeDtypeStruct(a.shape, F32) for a in states]
    out_specs = [pl.BlockSpec((None, L, D_MODEL), lambda b, c: (b, c, 0))] + [per_b(a) for a in states]
    return pl.pallas_call(
        functools.partial(_mixer_kernel, L),
        out_shape=out_shape,
        grid=(B, nc),
        in_specs=in_specs,
        out_specs=out_specs,
        scratch_shapes=[pltpu.VMEM((L, LANES), F32), pltpu.VMEM((L, LANES), F32),
                        pltpu.VMEM((L * L, LANES), BF16), pltpu.VMEM((L + 8, GDN_CONV_CH), F32)],
        compiler_params=_cparams(("parallel", "arbitrary")),
        name="mixer",
    )(z, cos_t, sin_t, *consts, *states, *params)


def _out_proj_kernel(h_ref, o_ref, w_ref, out_ref):
    out_ref[...] = h_ref[...] + jnp.dot(o_ref[...], w_ref[...], preferred_element_type=F32)


def _out_proj(h, o, w, tm):
    n = h.shape[0]
    return pl.pallas_call(
        _out_proj_kernel,
        out_shape=jax.ShapeDtypeStruct((n, D_MODEL), F32),
        grid=(n // tm,),
        in_specs=[pl.BlockSpec((tm, D_MODEL), lambda i: (i, 0)),
                  pl.BlockSpec((tm, D_MODEL), lambda i: (i, 0)),
                  pl.BlockSpec((D_MODEL, D_MODEL), lambda i: (0, 0))],
        out_specs=pl.BlockSpec((tm, D_MODEL), lambda i: (i, 0)),
        compiler_params=_cparams(("parallel",)),
        name="out_proj",
    )(h, o, w)


def _ffn_kernel(h_ref, g_ref, wg_ref, wu_ref, wd_ref, out_ref, v_ref, acc_ref):
    f = pl.program_id(1)

    @pl.when(f == 0)
    def _():
        v_ref[...] = _rms(h_ref[...], g_ref[...]).astype(BF16)
        acc_ref[...] = h_ref[...]

    v = v_ref[...]
    a = jnp.dot(v, wg_ref[...], preferred_element_type=F32)
    u = jnp.dot(v, wu_ref[...], preferred_element_type=F32)
    acc_ref[...] += jnp.dot((_silu(a) * u).astype(BF16), wd_ref[...], preferred_element_type=F32)

    @pl.when(f == pl.num_programs(1) - 1)
    def _():
        out_ref[...] = acc_ref[...]


def _ffn(h, g, wg, wu, wd, tm, tf):
    n = h.shape[0]
    return pl.pallas_call(
        _ffn_kernel,
        out_shape=jax.ShapeDtypeStruct((n, D_MODEL), F32),
        grid=(n // tm, D_FF // tf),
        in_specs=[pl.BlockSpec((tm, D_MODEL), lambda i, f: (i, 0)),
                  pl.BlockSpec((1, D_MODEL), lambda i, f: (0, 0)),
                  pl.BlockSpec((D_MODEL, tf), lambda i, f: (0, f)),
                  pl.BlockSpec((D_MODEL, tf), lambda i, f: (0, f)),
                  pl.BlockSpec((tf, D_MODEL), lambda i, f: (f, 0))],
        out_specs=pl.BlockSpec((tm, D_MODEL), lambda i, f: (i, 0)),
        scratch_shapes=[pltpu.VMEM((tm, D_MODEL), BF16), pltpu.VMEM((tm, D_MODEL), F32)],
        compiler_params=_cparams(("parallel", "arbitrary")),
        name="ffn",
    )(h, g, wg, wu, wd)


def _moe_kernel(h_ref, g_ref, r_ref, wg_ref, wu_ref, wd_ref, out_ref, v_ref, gate_ref, acc_ref):
    e = pl.program_id(1)
    f = pl.program_id(2)

    @pl.when((e == 0) & (f == 0))
    def _():
        vf = _rms(h_ref[...], g_ref[...])
        v_ref[...] = vf.astype(BF16)
        acc_ref[...] = jnp.zeros_like(acc_ref)
        logits = _hdot(vf, r_ref[...])
        lane = lax.broadcasted_iota(jnp.int32, logits.shape, 1)
        neg = jnp.float32(-jnp.inf)
        lg = jnp.where(lane < N_EXPERTS, logits, neg)
        m1 = jnp.max(lg, axis=1, keepdims=True)
        i1 = jnp.min(jnp.where(lg == m1, lane, LANES), axis=1, keepdims=True)
        lg2 = jnp.where(lane == i1, neg, lg)
        m2 = jnp.max(lg2, axis=1, keepdims=True)
        i2 = jnp.min(jnp.where(lg2 == m2, lane, LANES), axis=1, keepdims=True)
        e2 = jnp.exp(m2 - m1)
        den = 1.0 + e2
        gate_ref[...] = jnp.where(lane == i1, 1.0 / den, 0.0) + jnp.where(lane == i2, e2 / den, 0.0)

    v = v_ref[...]
    a = jnp.dot(v, wg_ref[...], preferred_element_type=F32)
    u = jnp.dot(v, wu_ref[...], preferred_element_type=F32)
    y = jnp.dot((_silu(a) * u).astype(BF16), wd_ref[...], preferred_element_type=F32)
    lane = lax.broadcasted_iota(jnp.int32, gate_ref.shape, 1)
    ge = jnp.sum(jnp.where(lane == e, gate_ref[...], 0.0), axis=1, keepdims=True)
    acc_ref[...] += ge * y

    @pl.when((e == pl.num_programs(1) - 1) & (f == pl.num_programs(2) - 1))
    def _():
        out_ref[...] = h_ref[...] + acc_ref[...]


def _moe(h, g, router, wg, wu, wd, tm, tf):
    n = h.shape[0]
    return pl.pallas_call(
        _moe_kernel,
        out_shape=jax.ShapeDtypeStruct((n, D_MODEL), F32),
        grid=(n // tm, N_EXPERTS, D_FF // tf),
        in_specs=[pl.BlockSpec((tm, D_MODEL), lambda i, e, f: (i, 0)),
                  pl.BlockSpec((1, D_MODEL), lambda i, e, f: (0, 0)),
                  pl.BlockSpec((D_MODEL, LANES), lambda i, e, f: (0, 0)),
                  pl.BlockSpec((None, D_MODEL, tf), lambda i, e, f: (e, 0, f)),
                  pl.BlockSpec((None, D_MODEL, tf), lambda i, e, f: (e, 0, f)),
                  pl.BlockSpec((None, tf, D_MODEL), lambda i, e, f: (e, f, 0))],
        out_specs=pl.BlockSpec((tm, D_MODEL), lambda i, e, f: (i, 0)),
        scratch_shapes=[pltpu.VMEM((tm, D_MODEL), BF16), pltpu.VMEM((tm, LANES), F32),
                        pltpu.VMEM((tm, D_MODEL), F32)],
        compiler_params=_cparams(("parallel", "arbitrary", "arbitrary")),
        name="moe",
    )(h, g, router, wg, wu, wd)


def _ple_kernel(final, h_ref, p_ref, g_ref, wup_ref, wgate_ref, gf_ref, out_ref):
    h = h_ref[...]
    up = jnp.dot(p_ref[...].astype(BF16), wup_ref[...], preferred_element_type=F32)
    gt = jnp.dot(_rms(h, g_ref[...]).astype(BF16), wgate_ref[...], preferred_element_type=F32)
    hn = h + up * jax.nn.sigmoid(gt)
    if final:
        hn = _rms(hn, gf_ref[...])
    out_ref[...] = hn


def _ple(h, p, g, wup, wgate, gf, final, tm):
    n = h.shape[0]
    return pl.pallas_call(
        functools.partial(_ple_kernel, final),
        out_shape=jax.ShapeDtypeStruct((n, D_MODEL), F32),
        grid=(n // tm,),
        in_specs=[pl.BlockSpec((tm, D_MODEL), lambda i: (i, 0)),
                  pl.BlockSpec((tm, PLE_DIM), lambda i: (i, 0)),
                  pl.BlockSpec((1, D_MODEL), lambda i: (0, 0)),
                  pl.BlockSpec((PLE_DIM, D_MODEL), lambda i: (0, 0)),
                  pl.BlockSpec((D_MODEL, D_MODEL), lambda i: (0, 0)),
                  pl.BlockSpec((1, D_MODEL), lambda i: (0, 0))],
        out_specs=pl.BlockSpec((tm, D_MODEL), lambda i: (i, 0)),
        compiler_params=_cparams(("parallel",)),
        name="ple",
    )(h, p, g, wup, wgate, gf)


def _reorder_w_in(w):
    sizes = (GLA_KW, GLA_KW, GLA_VW, GLA_RANK, GLA_VW, RET_KW, RET_KW, RET_VW, RET_VW,
             GDN_CONV_CH, GDN_H, GDN_H, GDN_VW)
    pts = np.cumsum(np.array(sizes))[:-1].tolist()
    gq, gk, gv, glr, gr, rq, rk, rv, rr, dqkv, da, db, dz = jnp.split(w, pts, axis=1)
    pad = jnp.zeros((w.shape[0], LANES - GLA_RANK - 2 * GDN_H), w.dtype)
    return jnp.concatenate([gq, gk, gv, gr, rq, rk, rv, rr, dqkv, dz, glr, da, db, pad], axis=1)


def _lane_row(vals, offset):
    return jnp.zeros((1, LANES), F32).at[0, offset:offset + vals.shape[0]].set(vals.astype(F32))


def _to_blockdiag(s):
    B, H, dk, dv = s.shape
    eye = jnp.eye(H, dtype=s.dtype)
    return jnp.einsum("bhde,hg->bhdge", s, eye).reshape(B, H * dk, H * dv)


def _from_blockdiag(s, H):
    B, R, C = s.shape
    dk, dv = R // H, C // H
    s5 = s.reshape(B, H, dk, H, dv)
    return jnp.stack([s5[:, h, :, h, :] for h in range(H)], axis=1)


def kernel(x_prompt, x_sample, state_gla, state_ret, state_gdn, state_gdn_conv, p_prompt, p_sample, norm_mix, w_in, gla_w_gate_up, gla_b_gate, gla_norm, ret_norm, gdn_conv, gdn_a_log, gdn_dt_bias, gdn_norm, w_out, norm_ffn, ffn_w_gate, ffn_w_up, ffn_w_down, moe_router, moe_w_gate, moe_w_up, moe_w_down, ple_w_up, ple_norm, ple_w_gate, norm_final):
    depth = w_in.shape[0]
    Bp, Tp, _ = x_prompt.shape
    Bs, Ts, _ = x_sample.shape
    n_p, n_s = Bp * Tp, Bs * Ts
    tm = 512

    h = jnp.concatenate([x_prompt.reshape(n_p, D_MODEL), x_sample.reshape(n_s, D_MODEL)], axis=0)
    p_all = jnp.concatenate([p_prompt.reshape(depth, n_p, PLE_DIM), p_sample.reshape(depth, n_s, PLE_DIM)], axis=1)

    outs_p = [[], [], [], []]
    outs_s = [[], [], [], []]
    for i in range(depth):
        w_in_k = _reorder_w_in(w_in[i]).astype(BF16)
        prm = dict(
            wgu=jnp.zeros((LANES, GLA_KW), F32).at[:GLA_RANK].set(gla_w_gate_up[i]).astype(BF16),
            gbias=gla_b_gate[i].reshape(1, GLA_KW).astype(F32),
            alog=_lane_row(gdn_a_log[i], SM_DA),
            dtb=_lane_row(gdn_dt_bias[i], SM_DA),
            glan=jnp.tile(gla_norm[i].astype(F32), GLA_H).reshape(1, GLA_VW),
            retn=jnp.tile(ret_norm[i].astype(F32), RET_H).reshape(1, RET_VW),
            gdnn=gdn_norm[i].astype(F32).reshape(1, GDN_DV),
            convw=gdn_conv[i].astype(F32),
        )
        z = _norm_proj(h, norm_mix[i].reshape(1, D_MODEL), w_in_k, tm)
        z_p = z[:n_p].reshape(Bp, Tp, Z_WIDTH)
        z_s = z[n_p:].reshape(Bs, Ts, Z_WIDTH)
        o_p, gla_p, ret_p, gdn_p, conv_p = _mixer(
            z_p, 0,
            jnp.zeros((Bp, GLA_KW, GLA_VW), F32), jnp.zeros((Bp, RET_KW, RET_VW), F32),
            jnp.zeros((Bp, GDN_H, GDN_DK, GDN_DV), F32), jnp.zeros((Bp, GDN_CONV_W - 1, GDN_CONV_CH), F32), prm)
        o_s, gla_s, ret_s, gdn_s, conv_s = _mixer(
            z_s, PAST_LEN,
            _to_blockdiag(state_gla[i].astype(F32)), _to_blockdiag(state_ret[i].astype(F32)),
            state_gdn[i].astype(F32), state_gdn_conv[i].astype(F32), prm)
        for lst, val in zip(outs_p, (_from_blockdiag(gla_p, GLA_H), _from_blockdiag(ret_p, RET_H), gdn_p, conv_p)):
            lst.append(val)
        for lst, val in zip(outs_s, (_from_blockdiag(gla_s, GLA_H), _from_blockdiag(ret_s, RET_H), gdn_s, conv_s)):
            lst.append(val)
        o = jnp.concatenate([o_p.reshape(n_p, D_MODEL), o_s.reshape(n_s, D_MODEL)], axis=0)
        h = _out_proj(h, o, w_out[i].astype(BF16), tm)
        if i % 2 == 0:
            j = i // 2
            h = _ffn(h, norm_ffn[i].reshape(1, D_MODEL), ffn_w_gate[j].astype(BF16), ffn_w_up[j].astype(BF16),
                     ffn_w_down[j].astype(BF16), tm, 512)
        else:
            j = i // 2
            router = jnp.zeros((D_MODEL, LANES), F32).at[:, :N_EXPERTS].set(moe_router[j])
            h = _moe(h, norm_ffn[i].reshape(1, D_MODEL), router, moe_w_gate[j].astype(BF16),
                     moe_w_up[j].astype(BF16), moe_w_down[j].astype(BF16), tm, 512)
        h = _ple(h, p_all[i], ple_norm[i].reshape(1, D_MODEL), ple_w_up[i].astype(BF16),
                 ple_w_gate[i].astype(BF16), norm_final.reshape(1, D_MODEL), i == depth - 1, tm)

    y_p = h[:n_p].reshape(Bp, Tp, D_MODEL)
    y_s = h[n_p:].reshape(Bs, Ts, D_MODEL)
    return (y_p, y_s,
            jnp.stack(outs_p[0]), jnp.stack(outs_p[1]), jnp.stack(outs_p[2]), jnp.stack(outs_p[3]),
            jnp.stack(outs_s[0]), jnp.stack(outs_s[1]), jnp.stack(outs_s[2]), jnp.stack(outs_s[3]))
```

```python
import functools
import math

import jax
import jax.numpy as jnp
import numpy as np
from jax import lax
from jax.experimental import pallas as pl
from jax.experimental.pallas import tpu as pltpu

F32 = jnp.float32
BF16 = jnp.bfloat16
HIGHEST = lax.Precision.HIGHEST

D_MODEL = 1024
CHUNK = 64
PLE_DIM = 256
EPS = 1e-6
PAST_LEN = 4096

GLA_H, GLA_DK, GLA_DV, GLA_RANK, GLA_TAU = 4, 32, 64, 16, 16.0
RET_H, RET_DK, RET_DV = 4, 32, 64
ROPE_BASE = 10000.0
GDN_H, GDN_DK, GDN_DV, GDN_CONV_W = 4, 128, 128, 4
GLA_KW, GLA_VW = GLA_H * GLA_DK, GLA_H * GLA_DV
RET_KW, RET_VW = RET_H * RET_DK, RET_H * RET_DV
GDN_KW, GDN_VW = GDN_H * GDN_DK, GDN_H * GDN_DV
GDN_CONV_CH = 2 * GDN_KW + GDN_VW
D_FF = 3584
N_EXPERTS = 8

LANES = 128

Z_GQ, Z_GK, Z_GV, Z_GR = 0, 128, 256, 512
Z_RQ, Z_RK, Z_RV, Z_RR = 768, 896, 1024, 1280
Z_CONV = 1536
Z_DZ = Z_CONV + GDN_CONV_CH
Z_SMALL = Z_DZ + GDN_VW
Z_WIDTH = Z_SMALL + LANES
SM_DA, SM_DB = GLA_RANK, GLA_RANK + GDN_H

MIX_ROWS = 256
VMEM_LIMIT = 56 * 1024 * 1024


def _cparams(sem):
    return pltpu.CompilerParams(dimension_semantics=sem, vmem_limit_bytes=VMEM_LIMIT)


def _rms(x, g):
    return x * lax.rsqrt(jnp.mean(x * x, axis=-1, keepdims=True) + EPS) * g


def _bdot(a, b):
    return jnp.dot(a.astype(BF16), b.astype(BF16), preferred_element_type=F32)


def _bdot_t(a, b):
    return lax.dot_general(a.astype(BF16), b.astype(BF16), (((1,), (1,)), ((), ())),
                           preferred_element_type=F32)


def _bdot_tl(a, b):
    return lax.dot_general(a.astype(BF16), b.astype(BF16), (((0,), (0,)), ((), ())),
                           preferred_element_type=F32)


def _hdot(a, b):
    return jnp.dot(a, b, precision=HIGHEST, preferred_element_type=F32)


def _silu(x):
    return x * jax.nn.sigmoid(x)


def _split3(a):
    hi = a.astype(BF16)
    r1 = a - hi.astype(F32)
    mid = r1.astype(BF16)
    lo = (r1 - mid.astype(F32)).astype(BF16)
    return [hi, mid, lo]


def _sum3(x):
    return x[:, 0:LANES] + x[:, LANES:2 * LANES] + x[:, 2 * LANES:3 * LANES]


def _norm_proj_kernel(h_ref, g_ref, w_ref, z_ref):
    u = _rms(h_ref[...], g_ref[...])
    z_ref[...] = jnp.dot(u.astype(BF16), w_ref[...], preferred_element_type=F32)


def _norm_proj(h, g, w, tm):
    n = h.shape[0]
    return pl.pallas_call(
        _norm_proj_kernel,
        out_shape=jax.ShapeDtypeStruct((n, w.shape[1]), F32),
        grid=(n // tm,),
        in_specs=[pl.BlockSpec((tm, D_MODEL), lambda i: (i, 0)),
                  pl.BlockSpec((1, D_MODEL), lambda i: (0, 0)),
                  pl.BlockSpec(w.shape, lambda i: (0, 0))],
        out_specs=pl.BlockSpec((tm, w.shape[1]), lambda i: (i, 0)),
        compiler_params=_cparams(("parallel",)),
        name="norm_proj",
    )(h, g, w)


def _mixer_kernel(L, n_seg, seq,
                  z_ref, cos_ref, sin_ref, retd_ref, reteb_ref, retebl_ref, retsdec_ref,
                  sgla0_ref, sret0_ref, sgdn0_ref, conv0_ref,
                  wgu_ref, gbias_ref, alog_ref, dtb_ref, glan_ref, retn_ref, gdnn_ref, convw_ref,
                  tri_ref, de_ref, lmask_ref, gn64_ref, blkmask_ref, hmk_ref, hmv_ref,
                  o_ref, sgla_ref, sret_ref, sgdn_ref, convo_ref,
                  xpad_ref):
    R = L * n_seg
    n_lev = int(math.log2(L))
    c = pl.program_id(1)

    @pl.when(c == 0)
    def _():
        sgla_ref[...] = sgla0_ref[...]
        sret_ref[...] = sret0_ref[...]
        sgdn_ref[...] = sgdn0_ref[...]
        convo_ref[...] = conv0_ref[...]

    blkmask = blkmask_ref[...]
    hmk = hmk_ref[...]
    hmv = hmv_ref[...]
    small = z_ref[:, Z_SMALL:Z_SMALL + LANES]

    def slot(j):
        return 0 if seq else j

    def rows(j):
        return slice(j * L, (j + 1) * L)

    la = jax.nn.log_sigmoid(_bdot(small, wgu_ref[...]) + gbias_ref[...]) * (1.0 / GLA_TAU)
    g_all = -jnp.exp(alog_ref[...]) * jax.nn.softplus(small + dtb_ref[...])
    beta_all = jax.nn.sigmoid(small)
    la3 = jnp.concatenate(_split3(la), axis=1)
    g3 = jnp.concatenate(_split3(g_all), axis=1)
    cs = jnp.dot(tri_ref[...], jnp.concatenate([la3, g3], axis=1), preferred_element_type=F32)
    b = _sum3(cs[:, 0:3 * LANES])
    bg = _sum3(cs[:, 3 * LANES:6 * LANES])
    ede = jnp.exp(_sum3(jnp.dot(de_ref[...], la3, preferred_element_type=F32)))

    def group_norm_gate(o, g, gate):
        sq = o * o
        hi = sq.astype(BF16)
        lo = (sq - hi.astype(F32)).astype(BF16)
        gn = gn64_ref[...]
        ms = (jnp.dot(hi, gn, preferred_element_type=F32) + jnp.dot(lo, gn, preferred_element_type=F32))
        return o * lax.rsqrt(ms + EPS) * g * _silu(gate)

    def intra_scores(q_list, kblk_list, masks):
        att = None
        for qm, kb, mk in zip(q_list, kblk_list, masks):
            term = mk * _bdot_t(qm, kb)
            att = term if att is None else att + term
        return att

    q = z_ref[:, Z_GQ:Z_GQ + GLA_KW] * (GLA_DK ** -0.5)
    k = z_ref[:, Z_GK:Z_GK + GLA_KW]
    v = z_ref[:, Z_GV:Z_GV + GLA_VW]
    qe = q * jnp.exp(b)
    b_t = b.T
    s_gla = sgla_ref[0] if seq else None
    o_parts = []
    for j in range(n_seg):
        r = rows(j)
        qj, kj, vj = q[r], k[r], v[r]
        kblk0 = jnp.concatenate([kj] * GLA_H, axis=0) * hmk
        q_list, kb_list, masks = [], [], []
        for lv in range(n_lev):
            eq = ede[lv * R + j * L:lv * R + (j + 1) * L]
            ek = ede[(n_lev + lv) * R + j * L:(n_lev + lv) * R + (j + 1) * L]
            q_list.append(qj * eq)
            kb_list.append(kblk0 * jnp.concatenate([ek] * GLA_H, axis=0))
            masks.append(lmask_ref[lv])
        q_list.append(qj)
        kb_list.append(kblk0)
        masks.append(lmask_ref[n_lev])
        att = intra_scores(q_list, kb_list, masks)
        vblk = jnp.concatenate([vj] * GLA_H, axis=0) * hmv
        last = (j + 1) * L - 1
        bl = b[last:last + 1, :]
        kv = blkmask * _bdot_tl(kj * jnp.exp(bl - b[r]), vj)
        decc = jnp.exp(b_t[:, last:last + 1])
        if not seq:
            s_gla = sgla_ref[j]
        o_parts.append(_bdot(att, vblk) + _bdot(qe[r], s_gla))
        s_gla = s_gla * decc + kv
        if not seq:
            sgla_ref[j] = s_gla
    if seq:
        sgla_ref[0] = s_gla
    o_gla = jnp.concatenate(o_parts, axis=0)
    o_ref[:, 0:GLA_VW] = group_norm_gate(o_gla, glan_ref[...], z_ref[:, Z_GR:Z_GR + GLA_VW]).astype(o_ref.dtype)

    cos = cos_ref[...]
    sin = sin_ref[...]
    lane = lax.broadcasted_iota(jnp.int32, (R, LANES), 1)
    first_half = (lane % RET_DK) < (RET_DK // 2)

    def rot(xx):
        sw = jnp.where(first_half, pltpu.roll(xx, LANES - RET_DK // 2, 1), pltpu.roll(xx, RET_DK // 2, 1))
        return xx * cos + sw * sin

    q = rot(z_ref[:, Z_RQ:Z_RQ + RET_KW]) * (RET_DK ** -0.5)
    k = rot(z_ref[:, Z_RK:Z_RK + RET_KW])
    v = z_ref[:, Z_RV:Z_RV + RET_VW]
    retd = retd_ref[...]
    reteb = reteb_ref[...]
    retebl = retebl_ref[...]
    retsdec = retsdec_ref[...]
    s_ret = sret_ref[0] if seq else None
    o_parts = []
    for j in range(n_seg):
        r = rows(j)
        qj, kj, vj = q[r], k[r], v[r]
        kblk = jnp.concatenate([kj] * RET_H, axis=0) * hmk
        att = _bdot_t(qj, kblk) * retd
        vblk = jnp.concatenate([vj] * RET_H, axis=0) * hmv
        kv = blkmask * _bdot_tl(kj * retebl, vj)
        if not seq:
            s_ret = sret_ref[j]
        o_parts.append(_bdot(att, vblk) + _bdot(qj * reteb, s_ret))
        s_ret = s_ret * retsdec + kv
        if not seq:
            sret_ref[j] = s_ret
    if seq:
        sret_ref[0] = s_ret
    o_ret = jnp.concatenate(o_parts, axis=0)
    o_ref[:, GLA_VW:GLA_VW + RET_VW] = group_norm_gate(
        o_ret, retn_ref[...], z_ref[:, Z_RR:Z_RR + RET_VW]).astype(o_ref.dtype)

    n_cs = 1 if seq else n_seg
    Lc = R // n_cs
    cw = convw_ref[...]
    xin = z_ref[:, Z_CONV:Z_CONV + GDN_CONV_CH]
    c_parts = []
    for s in range(n_cs):
        xs = xin[s * Lc:(s + 1) * Lc]
        xpad_ref[s, 8 - (GDN_CONV_W - 1):8, :] = convo_ref[s]
        xpad_ref[s, 8:8 + Lc, :] = xs
        convo_ref[s] = xs[Lc - (GDN_CONV_W - 1):Lc, :]
        cacc = xs * cw[GDN_CONV_W - 1:GDN_CONV_W, :]
        for jj in range(GDN_CONV_W - 1):
            sh = GDN_CONV_W - 1 - jj
            cacc = cacc + xpad_ref[s, 8 - sh:8 - sh + Lc, :] * cw[jj:jj + 1, :]
        c_parts.append(cacc)
    cact = _silu(c_parts[0] if n_cs == 1 else jnp.concatenate(c_parts, axis=0))

    row = lax.broadcasted_iota(jnp.int32, (R, R), 0)
    col = lax.broadcasted_iota(jnp.int32, (R, R), 1)
    same = lax.shift_right_logical(row, n_lev) == lax.shift_right_logical(col, n_lev)
    tril = same & (row >= col)
    strict = same & (row > col)
    eye_f = (row == col).astype(F32)
    bg_t = bg.T
    beta_t = beta_all.T
    gdnn = gdnn_ref[...]

    def l2n(xx):
        return xx * lax.rsqrt(jnp.sum(xx * xx, axis=-1, keepdims=True) + EPS)

    for h in range(GDN_H):
        qh = l2n(cact[:, h * GDN_DK:(h + 1) * GDN_DK]) * (GDN_DK ** -0.5)
        kh = l2n(cact[:, GDN_KW + h * GDN_DK:GDN_KW + (h + 1) * GDN_DK])
        vh = cact[:, 2 * GDN_KW + h * GDN_DV:2 * GDN_KW + (h + 1) * GDN_DV]
        bcol = bg[:, SM_DA + h:SM_DA + h + 1]
        brow = bg_t[SM_DA + h:SM_DA + h + 1, :]
        beta_row = beta_t[SM_DB + h:SM_DB + h + 1, :]
        beta_col = beta_all[:, SM_DB + h:SM_DB + h + 1]
        dec = jnp.exp(jnp.where(tril, bcol - brow, -jnp.inf))
        khb = kh.astype(BF16)
        m = _bdot_t(khb, khb) * jnp.where(strict, dec, 0.0) * beta_row
        tinv = eye_f - m
        pw = m.astype(BF16)
        span = 2
        while span < L:
            pwf = jnp.dot(pw, pw, preferred_element_type=F32)
            pw = pwf.astype(BF16)
            tinv = tinv + jnp.dot(tinv.astype(BF16), pw, preferred_element_type=F32)
            span *= 2
        ebc = jnp.exp(bcol)
        x1 = _bdot(tinv, jnp.concatenate([vh, ebc * kh], axis=1)).astype(BF16)
        qk = _bdot_t(qh, khb) * dec * beta_row
        x2 = _bdot(qk, x1)
        o0 = x2[:, 0:GDN_DV]
        qeff = ebc * qh - x2[:, GDN_DV:2 * GDN_DV]
        s_h = sgdn_ref[0, h] if seq else None
        o_parts = []
        for j in range(n_seg):
            r = rows(j)
            last = (j + 1) * L - 1
            blg = bcol[last:last + 1, :]
            kt = kh[r] * (jnp.exp(blg - bcol[r]) * beta_col[r])
            x3 = _bdot_tl(kt, x1[r])
            if not seq:
                s_h = sgdn_ref[j, h]
            y = _bdot(jnp.concatenate([qeff[r], x3[:, GDN_DV:2 * GDN_DV]], axis=0), s_h)
            o_parts.append(o0[r] + y[0:L])
            s_h = jnp.exp(blg) * s_h - y[L:L + GDN_DK] + x3[:, 0:GDN_DV]
            if not seq:
                sgdn_ref[j, h] = s_h
        if seq:
            sgdn_ref[0, h] = s_h
        oh = jnp.concatenate(o_parts, axis=0)
        oh = _rms(oh, gdnn) * _silu(z_ref[:, Z_DZ + h * GDN_DV:Z_DZ + (h + 1) * GDN_DV])
        lo = GLA_VW + RET_VW + h * GDN_DV
        o_ref[:, lo:lo + GDN_DV] = oh.astype(o_ref.dtype)


def _head_block_mask(rows_per_head, cols_per_head, heads):
    r = np.arange(rows_per_head * heads)[:, None] // rows_per_head
    c = np.arange(cols_per_head * heads)[None, :] // cols_per_head
    return (r == c).astype(np.float32)


def _pack_consts(L, n_seg):
    R = L * n_seg
    t = np.arange(R)
    tl = t % L
    base = t - tl
    jj = np.arange(R)[None, :]
    tri = ((jj // L) == (t[:, None] // L)) & (jj <= t[:, None])
    tt = np.arange(L)[:, None]
    ss = np.arange(L)[None, :]
    d_list, e_list, masks = [], [], []
    m = L // 2
    while m >= 1:
        second = (tl // m) % 2 == 1
        ref = base + (tl // (2 * m)) * 2 * m + m - 1
        d_list.append(second[:, None] & (jj > ref[:, None]) & (jj <= t[:, None]))
        e_list.append((~second)[:, None] & (jj > t[:, None]) & (jj <= ref[:, None]))
        mk = (tt // (2 * m) == ss // (2 * m)) & ((tt // m) % 2 == 1) & ((ss // m) % 2 == 0)
        masks.append(np.tile(mk, (1, GLA_H)))
        m //= 2
    masks.append(np.tile(np.eye(L, dtype=bool), (1, GLA_H)))
    de = np.concatenate(d_list + e_list, axis=0)
    return (jnp.asarray(tri, BF16), jnp.asarray(de, BF16), jnp.asarray(np.stack(masks), F32))


def _mixer(z, pos0, seq, sgla0, sret0, sgdn0, conv0, prm):
    B, T, _ = z.shape
    if seq:
        L = CHUNK
        n_seg = MIX_ROWS // L
        grid = (B, T // MIX_ROWS)
        n_state = 1
    else:
        L = T
        n_seg = MIX_ROWS // L
        grid = (B // n_seg, 1)
        n_state = n_seg
    R = MIX_ROWS
    zf = z.reshape(B * T, Z_WIDTH)
    steps_per_b = grid[1]
    half = RET_DK // 2
    inv = ROPE_BASE ** (-jnp.arange(half, dtype=F32) / half)
    ang = (pos0 + jnp.arange(T, dtype=jnp.int32)).astype(F32)[:, None] * inv[None, :]
    cos_h = jnp.concatenate([jnp.cos(ang), jnp.cos(ang)], axis=1)
    sin_h = jnp.concatenate([-jnp.sin(ang), jnp.sin(ang)], axis=1)
    cos_t = jnp.tile(cos_h, (1, RET_H))
    sin_t = jnp.tile(sin_h, (1, RET_H))
    if not seq:
        cos_t = jnp.tile(cos_t, (n_seg, 1))
        sin_t = jnp.tile(sin_t, (n_seg, 1))
    log_gamma = jnp.log(1.0 - jnp.exp2(-5.0 - jnp.arange(RET_H, dtype=F32)))
    tpos = jnp.arange(L, dtype=F32)
    bret = (tpos[:, None] + 1.0) * log_gamma[None, :]
    dmat = tpos[:, None] - tpos[None, :]
    retd = jnp.where(dmat[None] >= 0, jnp.exp(dmat[None] * log_gamma[:, None, None]), 0.0)
    retd = jnp.transpose(retd, (1, 0, 2)).reshape(L, RET_H * L)
    reteb = jnp.repeat(jnp.exp(bret), RET_DK, axis=1)
    retebl = jnp.repeat(jnp.exp(bret[L - 1:L] - bret), RET_DK, axis=1)
    retsdec = jnp.broadcast_to(jnp.repeat(jnp.exp(bret[L - 1]), RET_DK)[:, None], (RET_KW, RET_VW))

    tri, de, lmask = _pack_consts(L, n_seg)
    gn64 = jnp.asarray(_head_block_mask(GLA_DV, GLA_DV, GLA_H) / GLA_DV, BF16)
    blkmask = jnp.asarray(_head_block_mask(GLA_DK, GLA_DV, GLA_H), F32)
    hmk = jnp.asarray(_head_block_mask(L, RET_DK, RET_H), F32)
    hmv = jnp.asarray(_head_block_mask(L, RET_DV, RET_H), F32)

    def full(a):
        nd = a.ndim
        return pl.BlockSpec(a.shape, lambda b, c: (0,) * nd)

    def per_b(a):
        nd = a.ndim
        return pl.BlockSpec((n_state,) + a.shape[1:], lambda b, c: (b,) + (0,) * (nd - 1))

    if seq:
        tab_spec = pl.BlockSpec((R, LANES), lambda b, c: (c, 0))
    else:
        tab_spec = pl.BlockSpec((R, LANES), lambda b, c: (0, 0))
    consts = [retd, reteb, retebl, retsdec]
    states = [sgla0, sret0, sgdn0, conv0]
    params = [prm["wgu"], prm["gbias"], prm["alog"], prm["dtb"], prm["glan"], prm["retn"], prm["gdnn"],
              prm["convw"], tri, de, lmask, gn64, blkmask, hmk, hmv]
    in_specs = ([pl.BlockSpec((R, Z_WIDTH), lambda b, c: (b * steps_per_b + c, 0)), tab_spec, tab_spec]
                + [full(a) for a in consts] + [per_b(a) for a in states] + [full(a) for a in params])
    out_shape = [jax.ShapeDtypeStruct((B * T, D_MODEL), BF16)] + [
        jax.ShapeDtypeStruct(a.shape, F32) for a in states]
    out_specs = [pl.BlockSpec((R, D_MODEL), lambda b, c: (b * steps_per_b + c, 0))] + [per_b(a) for a in states]
    n_cs = 1 if seq else n_seg
    return pl.pallas_call(
        functools.partial(_mixer_kernel, L, n_seg, seq),
        out_shape=out_shape,
        grid=grid,
        in_specs=in_specs,
        out_specs=out_specs,
        scratch_shapes=[pltpu.VMEM((n_cs, R // n_cs + 8, GDN_CONV_CH), F32)],
        compiler_params=_cparams(("parallel", "arbitrary")),
        name="mixer",
    )(zf, cos_t, sin_t, *consts, *states, *params)


def _out_proj_kernel(h_ref, o_ref, w_ref, out_ref):
    out_ref[...] = h_ref[...] + jnp.dot(o_ref[...], w_ref[...], preferred_element_type=F32)


def _out_proj(h, o, w, tm):
    n = h.shape[0]
    return pl.pallas_call(
        _out_proj_kernel,
        out_shape=jax.ShapeDtypeStruct((n, D_MODEL), F32),
        grid=(n // tm,),
        in_specs=[pl.BlockSpec((tm, D_MODEL), lambda i: (i, 0)),
                  pl.BlockSpec((tm, D_MODEL), lambda i: (i, 0)),
                  pl.BlockSpec((D_MODEL, D_MODEL), lambda i: (0, 0))],
        out_specs=pl.BlockSpec((tm, D_MODEL), lambda i: (i, 0)),
        compiler_params=_cparams(("parallel",)),
        name="out_proj",
    )(h, o, w)


def _ffn_kernel(h_ref, g_ref, wg_ref, wu_ref, wd_ref, out_ref, v_ref, acc_ref):
    f = pl.program_id(1)

    @pl.when(f == 0)
    def _():
        v_ref[...] = _rms(h_ref[...], g_ref[...]).astype(BF16)
        acc_ref[...] = h_ref[...]

    v = v_ref[...]
    a = jnp.dot(v, wg_ref[...], preferred_element_type=F32)
    u = jnp.dot(v, wu_ref[...], preferred_element_type=F32)
    acc_ref[...] += jnp.dot((_silu(a) * u).astype(BF16), wd_ref[...], preferred_element_type=F32)

    @pl.when(f == pl.num_programs(1) - 1)
    def _():
        out_ref[...] = acc_ref[...]


def _ffn(h, g, wg, wu, wd, tm, tf):
    n = h.shape[0]
    return pl.pallas_call(
        _ffn_kernel,
        out_shape=jax.ShapeDtypeStruct((n, D_MODEL), F32),
        grid=(n // tm, D_FF // tf),
        in_specs=[pl.BlockSpec((tm, D_MODEL), lambda i, f: (i, 0)),
                  pl.BlockSpec((1, D_MODEL), lambda i, f: (0, 0)),
                  pl.BlockSpec((D_MODEL, tf), lambda i, f: (0, f)),
                  pl.BlockSpec((D_MODEL, tf), lambda i, f: (0, f)),
                  pl.BlockSpec((tf, D_MODEL), lambda i, f: (f, 0))],
        out_specs=pl.BlockSpec((tm, D_MODEL), lambda i, f: (i, 0)),
        scratch_shapes=[pltpu.VMEM((tm, D_MODEL), BF16), pltpu.VMEM((tm, D_MODEL), F32)],
        compiler_params=_cparams(("parallel", "arbitrary")),
        name="ffn",
    )(h, g, wg, wu, wd)


def _moe_kernel(h_ref, g_ref, r_ref, wg_ref, wu_ref, wd_ref, out_ref, v_ref, gate_ref, acc_ref):
    e = pl.program_id(1)
    f = pl.program_id(2)

    @pl.when((e == 0) & (f == 0))
    def _():
        vf = _rms(h_ref[...], g_ref[...])
        v_ref[...] = vf.astype(BF16)
        acc_ref[...] = jnp.zeros_like(acc_ref)
        logits = _hdot(vf, r_ref[...])
        lane = lax.broadcasted_iota(jnp.int32, logits.shape, 1)
        neg = jnp.float32(-jnp.inf)
        lg = jnp.where(lane < N_EXPERTS, logits, neg)
        m1 = jnp.max(lg, axis=1, keepdims=True)
        i1 = jnp.min(jnp.where(lg == m1, lane, LANES), axis=1, keepdims=True)
        lg2 = jnp.where(lane == i1, neg, lg)
        m2 = jnp.max(lg2, axis=1, keepdims=True)
        i2 = jnp.min(jnp.where(lg2 == m2, lane, LANES), axis=1, keepdims=True)
        e2 = jnp.exp(m2 - m1)
        den = 1.0 + e2
        gate_ref[...] = jnp.where(lane == i1, 1.0 / den, 0.0) + jnp.where(lane == i2, e2 / den, 0.0)

    v = v_ref[...]
    a = jnp.dot(v, wg_ref[...], preferred_element_type=F32)
    u = jnp.dot(v, wu_ref[...], preferred_element_type=F32)
    y = jnp.dot((_silu(a) * u).astype(BF16), wd_ref[...], preferred_element_type=F32)
    lane = lax.broadcasted_iota(jnp.int32, gate_ref.shape, 1)
    ge = jnp.sum(jnp.where(lane == e, gate_ref[...], 0.0), axis=1, keepdims=True)
    acc_ref[...] += ge * y

    @pl.when((e == pl.num_programs(1) - 1) & (f == pl.num_programs(2) - 1))
    def _():
        out_ref[...] = h_ref[...] + acc_ref[...]


def _moe(h, g, router, wg, wu, wd, tm, tf):
    n = h.shape[0]
    return pl.pallas_call(
        _moe_kernel,
        out_shape=jax.ShapeDtypeStruct((n, D_MODEL), F32),
        grid=(n // tm, N_EXPERTS, D_FF // tf),
        in_specs=[pl.BlockSpec((tm, D_MODEL), lambda i, e, f: (i, 0)),
                  pl.BlockSpec((1, D_MODEL), lambda i, e, f: (0, 0)),
                  pl.BlockSpec((D_MODEL, LANES), lambda i, e, f: (0, 0)),
                  pl.BlockSpec((None, D_MODEL, tf), lambda i, e, f: (e, 0, f)),
                  pl.BlockSpec((None, D_MODEL, tf), lambda i, e, f: (e, 0, f)),
                  pl.BlockSpec((None, tf, D_MODEL), lambda i, e, f: (e, f, 0))],
        out_specs=pl.BlockSpec((tm, D_MODEL), lambda i, e, f: (i, 0)),
        scratch_shapes=[pltpu.VMEM((tm, D_MODEL), BF16), pltpu.VMEM((tm, LANES), F32),
                        pltpu.VMEM((tm, D_MODEL), F32)],
        compiler_params=_cparams(("parallel", "arbitrary", "arbitrary")),
        name="moe",
    )(h, g, router, wg, wu, wd)


def _ple_kernel(final, h_ref, p_ref, g_ref, wup_ref, wgate_ref, gf_ref, out_ref):
    h = h_ref[...]
    up = jnp.dot(p_ref[...].astype(BF16), wup_ref[...], preferred_element_type=F32)
    gt = jnp.dot(_rms(h, g_ref[...]).astype(BF16), wgate_ref[...], preferred_element_type=F32)
    hn = h + up * jax.nn.sigmoid(gt)
    if final:
        hn = _rms(hn, gf_ref[...])
    out_ref[...] = hn


def _ple(h, p, g, wup, wgate, gf, final, tm):
    n = h.shape[0]
    return pl.pallas_call(
        functools.partial(_ple_kernel, final),
        out_shape=jax.ShapeDtypeStruct((n, D_MODEL), F32),
        grid=(n // tm,),
        in_specs=[pl.BlockSpec((tm, D_MODEL), lambda i: (i, 0)),
                  pl.BlockSpec((tm, PLE_DIM), lambda i: (i, 0)),
                  pl.BlockSpec((1, D_MODEL), lambda i: (0, 0)),
                  pl.BlockSpec((PLE_DIM, D_MODEL), lambda i: (0, 0)),
                  pl.BlockSpec((D_MODEL, D_MODEL), lambda i: (0, 0)),
                  pl.BlockSpec((1, D_MODEL), lambda i: (0, 0))],
        out_specs=pl.BlockSpec((tm, D_MODEL), lambda i: (i, 0)),
        compiler_params=_cparams(("parallel",)),
        name="ple",
    )(h, p, g, wup, wgate, gf)


def _reorder_w_in(w):
    sizes = (GLA_KW, GLA_KW, GLA_VW, GLA_RANK, GLA_VW, RET_KW, RET_KW, RET_VW, RET_VW,
             GDN_CONV_CH, GDN_H, GDN_H, GDN_VW)
    pts = np.cumsum(np.array(sizes))[:-1].tolist()
    gq, gk, gv, glr, gr, rq, rk, rv, rr, dqkv, da, db, dz = jnp.split(w, pts, axis=1)
    pad = jnp.zeros((w.shape[0], LANES - GLA_RANK - 2 * GDN_H), w.dtype)
    return jnp.concatenate([gq, gk, gv, gr, rq, rk, rv, rr, dqkv, dz, glr, da, db, pad], axis=1)


def _lane_row(vals, offset):
    return jnp.zeros((1, LANES), F32).at[0, offset:offset + vals.shape[0]].set(vals.astype(F32))


def _to_blockdiag(s):
    B, H, dk, dv = s.shape
    eye = jnp.eye(H, dtype=s.dtype)
    return jnp.einsum("bhde,hg->bhdge", s, eye).reshape(B, H * dk, H * dv)


def _from_blockdiag(s, H):
    B, R, C = s.shape
    dk, dv = R // H, C // H
    s5 = s.reshape(B, H, dk, H, dv)
    return jnp.stack([s5[:, h, :, h, :] for h in range(H)], axis=1)


def kernel(x_prompt, x_sample, state_gla, state_ret, state_gdn, state_gdn_conv, p_prompt, p_sample, norm_mix, w_in, gla_w_gate_up, gla_b_gate, gla_norm, ret_norm, gdn_conv, gdn_a_log, gdn_dt_bias, gdn_norm, w_out, norm_ffn, ffn_w_gate, ffn_w_up, ffn_w_down, moe_router, moe_w_gate, moe_w_up, moe_w_down, ple_w_up, ple_norm, ple_w_gate, norm_final):
    depth = w_in.shape[0]
    Bp, Tp, _ = x_prompt.shape
    Bs, Ts, _ = x_sample.shape
    n_p, n_s = Bp * Tp, Bs * Ts
    tm = 512

    h = jnp.concatenate([x_prompt.reshape(n_p, D_MODEL), x_sample.reshape(n_s, D_MODEL)], axis=0)
    p_all = jnp.concatenate([p_prompt.reshape(depth, n_p, PLE_DIM), p_sample.reshape(depth, n_s, PLE_DIM)], axis=1)

    outs_p = [[], [], [], []]
    outs_s = [[], [], [], []]
    for i in range(depth):
        w_in_k = _reorder_w_in(w_in[i]).astype(BF16)
        prm = dict(
            wgu=jnp.zeros((LANES, GLA_KW), F32).at[:GLA_RANK].set(gla_w_gate_up[i]).astype(BF16),
            gbias=gla_b_gate[i].reshape(1, GLA_KW).astype(F32),
            alog=_lane_row(gdn_a_log[i], SM_DA),
            dtb=_lane_row(gdn_dt_bias[i], SM_DA),
            glan=jnp.tile(gla_norm[i].astype(F32), GLA_H).reshape(1, GLA_VW),
            retn=jnp.tile(ret_norm[i].astype(F32), RET_H).reshape(1, RET_VW),
            gdnn=gdn_norm[i].astype(F32).reshape(1, GDN_DV),
            convw=gdn_conv[i].astype(F32),
        )
        z = _norm_proj(h, norm_mix[i].reshape(1, D_MODEL), w_in_k, tm)
        z_p = z[:n_p].reshape(Bp, Tp, Z_WIDTH)
        z_s = z[n_p:].reshape(Bs, Ts, Z_WIDTH)
        o_p, gla_p, ret_p, gdn_p, conv_p = _mixer(
            z_p, 0, True,
            jnp.zeros((Bp, GLA_KW, GLA_VW), F32), jnp.zeros((Bp, RET_KW, RET_VW), F32),
            jnp.zeros((Bp, GDN_H, GDN_DK, GDN_DV), F32), jnp.zeros((Bp, GDN_CONV_W - 1, GDN_CONV_CH), F32), prm)
        o_s, gla_s, ret_s, gdn_s, conv_s = _mixer(
            z_s, PAST_LEN, False,
            _to_blockdiag(state_gla[i].astype(F32)), _to_blockdiag(state_ret[i].astype(F32)),
            state_gdn[i].astype(F32), state_gdn_conv[i].astype(F32), prm)
        for lst, val in zip(outs_p, (_from_blockdiag(gla_p, GLA_H), _from_blockdiag(ret_p, RET_H), gdn_p, conv_p)):
            lst.append(val)
        for lst, val in zip(outs_s, (_from_blockdiag(gla_s, GLA_H), _from_blockdiag(ret_s, RET_H), gdn_s, conv_s)):
            lst.append(val)
        o = jnp.concatenate([o_p, o_s], axis=0)
        h = _out_proj(h, o, w_out[i].astype(BF16), tm)
        if i % 2 == 0:
            j = i // 2
            h = _ffn(h, norm_ffn[i].reshape(1, D_MODEL), ffn_w_gate[j].astype(BF16), ffn_w_up[j].astype(BF16),
                     ffn_w_down[j].astype(BF16), tm, 512)
        else:
            j = i // 2
            router = jnp.zeros((D_MODEL, LANES), F32).at[:, :N_EXPERTS].set(moe_router[j])
            h = _moe(h, norm_ffn[i].reshape(1, D_MODEL), router, moe_w_gate[j].astype(BF16),
                     moe_w_up[j].astype(BF16), moe_w_down[j].astype(BF16), tm, 512)
        h = _ple(h, p_all[i], ple_norm[i].reshape(1, D_MODEL), ple_w_up[i].astype(BF16),
                 ple_w_gate[i].astype(BF16), norm_final.reshape(1, D_MODEL), i == depth - 1, tm)

    y_p = h[:n_p].reshape(Bp, Tp, D_MODEL)
    y_s = h[n_p:].reshape(Bs, Ts, D_MODEL)
    return (y_p, y_s,
            jnp.stack(outs_p[0]), jnp.stack(outs_p[1]), jnp.stack(outs_p[2]), jnp.stack(outs_p[3]),
            jnp.stack(outs_s[0]), jnp.stack(outs_s[1]), jnp.stack(outs_s[2]), jnp.stack(outs_s[3]))
```

```python
import functools
import math

import jax
import jax.numpy as jnp
import numpy as np
from jax import lax
from jax.experimental import pallas as pl
from jax.experimental.pallas import tpu as pltpu

F32 = jnp.float32
BF16 = jnp.bfloat16
HIGHEST = lax.Precision.HIGHEST

D_MODEL = 1024
CHUNK = 64
PLE_DIM = 256
EPS = 1e-6
PAST_LEN = 4096

GLA_H, GLA_DK, GLA_DV, GLA_RANK, GLA_TAU = 4, 32, 64, 16, 16.0
RET_H, RET_DK, RET_DV = 4, 32, 64
ROPE_BASE = 10000.0
GDN_H, GDN_DK, GDN_DV, GDN_CONV_W = 4, 128, 128, 4
GLA_KW, GLA_VW = GLA_H * GLA_DK, GLA_H * GLA_DV
RET_KW, RET_VW = RET_H * RET_DK, RET_H * RET_DV
GDN_KW, GDN_VW = GDN_H * GDN_DK, GDN_H * GDN_DV
GDN_CONV_CH = 2 * GDN_KW + GDN_VW
D_FF = 3584
N_EXPERTS = 8

LANES = 128

Z_GQ, Z_GK, Z_GV, Z_GR = 0, 128, 256, 512
Z_RQ, Z_RK, Z_RV, Z_RR = 768, 896, 1024, 1280
Z_CONV = 1536
Z_DZ = Z_CONV + GDN_CONV_CH
Z_SMALL = Z_DZ + GDN_VW
Z_WIDTH = Z_SMALL + LANES
SM_DA, SM_DB = GLA_RANK, GLA_RANK + GDN_H

MIX_ROWS = 256
VMEM_LIMIT = 56 * 1024 * 1024


def _cparams(sem):
    return pltpu.CompilerParams(dimension_semantics=sem, vmem_limit_bytes=VMEM_LIMIT)


def _rms(x, g):
    return x * lax.rsqrt(jnp.mean(x * x, axis=-1, keepdims=True) + EPS) * g


def _bdot(a, b):
    return jnp.dot(a.astype(BF16), b.astype(BF16), preferred_element_type=F32)


def _bdot_t(a, b):
    return lax.dot_general(a.astype(BF16), b.astype(BF16), (((1,), (1,)), ((), ())),
                           preferred_element_type=F32)


def _bdot_tl(a, b):
    return lax.dot_general(a.astype(BF16), b.astype(BF16), (((0,), (0,)), ((), ())),
                           preferred_element_type=F32)


def _hdot(a, b):
    return jnp.dot(a, b, precision=HIGHEST, preferred_element_type=F32)


def _silu(x):
    return x * jax.nn.sigmoid(x)


def _split3(a):
    hi = a.astype(BF16)
    r1 = a - hi.astype(F32)
    mid = r1.astype(BF16)
    lo = (r1 - mid.astype(F32)).astype(BF16)
    return [hi, mid, lo]


def _sum3(x):
    return x[:, 0:LANES] + x[:, LANES:2 * LANES] + x[:, 2 * LANES:3 * LANES]


def _norm_proj_kernel(h_ref, g_ref, w_ref, z_ref):
    u = _rms(h_ref[...], g_ref[...])
    z_ref[...] = jnp.dot(u.astype(BF16), w_ref[...], preferred_element_type=F32)


def _norm_proj(h, g, w, tm):
    n = h.shape[0]
    return pl.pallas_call(
        _norm_proj_kernel,
        out_shape=jax.ShapeDtypeStruct((n, w.shape[1]), F32),
        grid=(n // tm,),
        in_specs=[pl.BlockSpec((tm, D_MODEL), lambda i: (i, 0)),
                  pl.BlockSpec((1, D_MODEL), lambda i: (0, 0)),
                  pl.BlockSpec(w.shape, lambda i: (0, 0))],
        out_specs=pl.BlockSpec((tm, w.shape[1]), lambda i: (i, 0)),
        compiler_params=_cparams(("parallel",)),
        name="norm_proj",
    )(h, g, w)


def _mixer_kernel(L, n_seg, seq,
                  z_ref, cos_ref, sin_ref, retd_ref, reteb_ref, retebl_ref, retsdec_ref,
                  sgla0_ref, sret0_ref, sgdn0_ref, conv0_ref,
                  wgu_ref, gbias_ref, alog_ref, dtb_ref, glan_ref, retn_ref, gdnn_ref, convw_ref,
                  tri_ref, de_ref, lmask_ref, gn64_ref, blkmask_ref, hmk_ref, hmv_ref,
                  o_ref, sgla_ref, sret_ref, sgdn_ref, convo_ref,
                  xpad_ref):
    R = L * n_seg
    n_lev = int(math.log2(L))
    c = pl.program_id(1)

    @pl.when(c == 0)
    def _():
        sgla_ref[...] = sgla0_ref[...]
        sret_ref[...] = sret0_ref[...]
        sgdn_ref[...] = sgdn0_ref[...]
        convo_ref[...] = conv0_ref[...]

    blkmask = blkmask_ref[...]
    hmk = hmk_ref[...]
    hmv = hmv_ref[...]
    small = z_ref[:, Z_SMALL:Z_SMALL + LANES]

    def slot(j):
        return 0 if seq else j

    def rows(j):
        return slice(j * L, (j + 1) * L)

    la = jax.nn.log_sigmoid(_bdot(small, wgu_ref[...]) + gbias_ref[...]) * (1.0 / GLA_TAU)
    g_all = -jnp.exp(alog_ref[...]) * jax.nn.softplus(small + dtb_ref[...])
    beta_all = jax.nn.sigmoid(small)
    la3 = jnp.concatenate(_split3(la), axis=1)
    g3 = jnp.concatenate(_split3(g_all), axis=1)
    cs = jnp.dot(tri_ref[...], jnp.concatenate([la3, g3], axis=1), preferred_element_type=F32)
    b = _sum3(cs[:, 0:3 * LANES])
    bg = _sum3(cs[:, 3 * LANES:6 * LANES])
    ede = jnp.exp(_sum3(jnp.dot(de_ref[...], la3, preferred_element_type=F32)))

    def group_norm_gate(o, g, gate):
        sq = o * o
        hi = sq.astype(BF16)
        lo = (sq - hi.astype(F32)).astype(BF16)
        gn = gn64_ref[...]
        ms = (jnp.dot(hi, gn, preferred_element_type=F32) + jnp.dot(lo, gn, preferred_element_type=F32))
        return o * lax.rsqrt(ms + EPS) * g * _silu(gate)

    def intra_scores(q_list, kblk_list, masks):
        att = None
        for qm, kb, mk in zip(q_list, kblk_list, masks):
            term = mk * _bdot_t(qm, kb)
            att = term if att is None else att + term
        return att

    q = z_ref[:, Z_GQ:Z_GQ + GLA_KW] * (GLA_DK ** -0.5)
    k = z_ref[:, Z_GK:Z_GK + GLA_KW]
    v = z_ref[:, Z_GV:Z_GV + GLA_VW]
    qe = q * jnp.exp(b)
    b_t = b.T
    s_gla = sgla_ref[0] if seq else None
    o_parts = []
    for j in range(n_seg):
        r = rows(j)
        qj, kj, vj = q[r], k[r], v[r]
        kblk0 = jnp.concatenate([kj] * GLA_H, axis=0) * hmk
        q_list, kb_list, masks = [], [], []
        for lv in range(n_lev):
            eq = ede[lv * R + j * L:lv * R + (j + 1) * L]
            ek = ede[(n_lev + lv) * R + j * L:(n_lev + lv) * R + (j + 1) * L]
            q_list.append(qj * eq)
            kb_list.append(kblk0 * jnp.concatenate([ek] * GLA_H, axis=0))
            masks.append(lmask_ref[lv])
        q_list.append(qj)
        kb_list.append(kblk0)
        masks.append(lmask_ref[n_lev])
        att = intra_scores(q_list, kb_list, masks)
        vblk = jnp.concatenate([vj] * GLA_H, axis=0) * hmv
        last = (j + 1) * L - 1
        bl = b[last:last + 1, :]
        kv = blkmask * _bdot_tl(kj * jnp.exp(bl - b[r]), vj)
        decc = jnp.exp(b_t[:, last:last + 1])
        if not seq:
            s_gla = sgla_ref[j]
        o_parts.append(_bdot(att, vblk) + _bdot(qe[r], s_gla))
        s_gla = s_gla * decc + kv
        if not seq:
            sgla_ref[j] = s_gla
    if seq:
        sgla_ref[0] = s_gla
    o_gla = jnp.concatenate(o_parts, axis=0)
    o_ref[:, 0:GLA_VW] = group_norm_gate(o_gla, glan_ref[...], z_ref[:, Z_GR:Z_GR + GLA_VW]).astype(o_ref.dtype)

    cos = cos_ref[...]
    sin = sin_ref[...]
    lane = lax.broadcasted_iota(jnp.int32, (R, LANES), 1)
    first_half = (lane % RET_DK) < (RET_DK // 2)

    def rot(xx):
        sw = jnp.where(first_half, pltpu.roll(xx, LANES - RET_DK // 2, 1), pltpu.roll(xx, RET_DK // 2, 1))
        return xx * cos + sw * sin

    q = rot(z_ref[:, Z_RQ:Z_RQ + RET_KW]) * (RET_DK ** -0.5)
    k = rot(z_ref[:, Z_RK:Z_RK + RET_KW])
    v = z_ref[:, Z_RV:Z_RV + RET_VW]
    retd = retd_ref[...]
    reteb = reteb_ref[...]
    retebl = retebl_ref[...]
    retsdec = retsdec_ref[...]
    s_ret = sret_ref[0] if seq else None
    o_parts = []
    for j in range(n_seg):
        r = rows(j)
        qj, kj, vj = q[r], k[r], v[r]
        kblk = jnp.concatenate([kj] * RET_H, axis=0) * hmk
        att = _bdot_t(qj, kblk) * retd
        vblk = jnp.concatenate([vj] * RET_H, axis=0) * hmv
        kv = blkmask * _bdot_tl(kj * retebl, vj)
        if not seq:
            s_ret = sret_ref[j]
        o_parts.append(_bdot(att, vblk) + _bdot(qj * reteb, s_ret))
        s_ret = s_ret * retsdec + kv
        if not seq:
            sret_ref[j] = s_ret
    if seq:
        sret_ref[0] = s_ret
    o_ret = jnp.concatenate(o_parts, axis=0)
    o_ref[:, GLA_VW:GLA_VW + RET_VW] = group_norm_gate(
        o_ret, retn_ref[...], z_ref[:, Z_RR:Z_RR + RET_VW]).astype(o_ref.dtype)

    n_cs = 1 if seq else n_seg
    Lc = R // n_cs
    cw = convw_ref[...]
    xin = z_ref[:, Z_CONV:Z_CONV + GDN_CONV_CH]
    c_parts = []
    for s in range(n_cs):
        xs = xin[s * Lc:(s + 1) * Lc]
        xpad_ref[s, 8 - (GDN_CONV_W - 1):8, :] = convo_ref[s]
        xpad_ref[s, 8:8 + Lc, :] = xs
        convo_ref[s] = xs[Lc - (GDN_CONV_W - 1):Lc, :]
        cacc = xs * cw[GDN_CONV_W - 1:GDN_CONV_W, :]
        for jj in range(GDN_CONV_W - 1):
            sh = GDN_CONV_W - 1 - jj
            cacc = cacc + xpad_ref[s, 8 - sh:8 - sh + Lc, :] * cw[jj:jj + 1, :]
        c_parts.append(cacc)
    cact = _silu(c_parts[0] if n_cs == 1 else jnp.concatenate(c_parts, axis=0))

    row = lax.broadcasted_iota(jnp.int32, (R, R), 0)
    col = lax.broadcasted_iota(jnp.int32, (R, R), 1)
    same = lax.shift_right_logical(row, n_lev) == lax.shift_right_logical(col, n_lev)
    tril = same & (row >= col)
    strict = same & (row > col)
    eye_f = (row == col).astype(F32)
    bg_t = bg.T
    beta_t = beta_all.T
    gdnn = gdnn_ref[...]

    def l2n(xx):
        return xx * lax.rsqrt(jnp.sum(xx * xx, axis=-1, keepdims=True) + EPS)

    for h in range(GDN_H):
        qh = l2n(cact[:, h * GDN_DK:(h + 1) * GDN_DK]) * (GDN_DK ** -0.5)
        kh = l2n(cact[:, GDN_KW + h * GDN_DK:GDN_KW + (h + 1) * GDN_DK])
        vh = cact[:, 2 * GDN_KW + h * GDN_DV:2 * GDN_KW + (h + 1) * GDN_DV]
        bcol = bg[:, SM_DA + h:SM_DA + h + 1]
        brow = bg_t[SM_DA + h:SM_DA + h + 1, :]
        beta_row = beta_t[SM_DB + h:SM_DB + h + 1, :]
        beta_col = beta_all[:, SM_DB + h:SM_DB + h + 1]
        dec = jnp.exp(jnp.where(tril, bcol - brow, -jnp.inf))
        khb = kh.astype(BF16)
        m = _bdot_t(khb, khb) * jnp.where(strict, dec, 0.0) * beta_row
        tinv = eye_f - m
        pw = m.astype(BF16)
        span = 2
        while span < L:
            pwf = jnp.dot(pw, pw, preferred_element_type=F32)
            pw = pwf.astype(BF16)
            tinv = tinv + jnp.dot(tinv.astype(BF16), pw, preferred_element_type=F32)
            span *= 2
        ebc = jnp.exp(bcol)
        x1 = _bdot(tinv, jnp.concatenate([vh, ebc * kh], axis=1)).astype(BF16)
        qk = _bdot_t(qh, khb) * dec * beta_row
        x2 = _bdot(qk, x1)
        o0 = x2[:, 0:GDN_DV]
        qeff = ebc * qh - x2[:, GDN_DV:2 * GDN_DV]
        s_h = sgdn_ref[0, h] if seq else None
        o_parts = []
        for j in range(n_seg):
            r = rows(j)
            last = (j + 1) * L - 1
            blg = bcol[last:last + 1, :]
            kt = kh[r] * (jnp.exp(blg - bcol[r]) * beta_col[r])
            x3 = _bdot_tl(kt, x1[r])
            if not seq:
                s_h = sgdn_ref[j, h]
            y = _bdot(jnp.concatenate([qeff[r], x3[:, GDN_DV:2 * GDN_DV]], axis=0), s_h)
            o_parts.append(o0[r] + y[0:L])
            s_h = jnp.exp(blg) * s_h - y[L:L + GDN_DK] + x3[:, 0:GDN_DV]
            if not seq:
                sgdn_ref[j, h] = s_h
        if seq:
            sgdn_ref[0, h] = s_h
        oh = jnp.concatenate(o_parts, axis=0)
        oh = _rms(oh, gdnn) * _silu(z_ref[:, Z_DZ + h * GDN_DV:Z_DZ + (h + 1) * GDN_DV])
        lo = GLA_VW + RET_VW + h * GDN_DV
        o_ref[:, lo:lo + GDN_DV] = oh.astype(o_ref.dtype)


def _head_block_mask(rows_per_head, cols_per_head, heads):
    r = np.arange(rows_per_head * heads)[:, None] // rows_per_head
    c = np.arange(cols_per_head * heads)[None, :] // cols_per_head
    return (r == c).astype(np.float32)


def _pack_consts(L, n_seg):
    R = L * n_seg
    t = np.arange(R)
    tl = t % L
    base = t - tl
    jj = np.arange(R)[None, :]
    tri = ((jj // L) == (t[:, None] // L)) & (jj <= t[:, None])
    tt = np.arange(L)[:, None]
    ss = np.arange(L)[None, :]
    d_list, e_list, masks = [], [], []
    m = L // 2
    while m >= 1:
        second = (tl // m) % 2 == 1
        ref = base + (tl // (2 * m)) * 2 * m + m - 1
        d_list.append(second[:, None] & (jj > ref[:, None]) & (jj <= t[:, None]))
        e_list.append((~second)[:, None] & (jj > t[:, None]) & (jj <= ref[:, None]))
        mk = (tt // (2 * m) == ss // (2 * m)) & ((tt // m) % 2 == 1) & ((ss // m) % 2 == 0)
        masks.append(np.tile(mk, (1, GLA_H)))
        m //= 2
    masks.append(np.tile(np.eye(L, dtype=bool), (1, GLA_H)))
    de = np.concatenate(d_list + e_list, axis=0)
    return (jnp.asarray(tri, BF16), jnp.asarray(de, BF16), jnp.asarray(np.stack(masks), F32))


def _mixer(z, pos0, seq, sgla0, sret0, sgdn0, conv0, prm):
    B, T, _ = z.shape
    if seq:
        L = CHUNK
        n_seg = MIX_ROWS // L
        grid = (B, T // MIX_ROWS)
        n_state = 1
    else:
        L = T
        n_seg = MIX_ROWS // L
        grid = (B // n_seg, 1)
        n_state = n_seg
    R = MIX_ROWS
    zf = z.reshape(B * T, Z_WIDTH)
    steps_per_b = grid[1]
    half = RET_DK // 2
    inv = ROPE_BASE ** (-jnp.arange(half, dtype=F32) / half)
    ang = (pos0 + jnp.arange(T, dtype=jnp.int32)).astype(F32)[:, None] * inv[None, :]
    cos_h = jnp.concatenate([jnp.cos(ang), jnp.cos(ang)], axis=1)
    sin_h = jnp.concatenate([-jnp.sin(ang), jnp.sin(ang)], axis=1)
    cos_t = jnp.tile(cos_h, (1, RET_H))
    sin_t = jnp.tile(sin_h, (1, RET_H))
    if not seq:
        cos_t = jnp.tile(cos_t, (n_seg, 1))
        sin_t = jnp.tile(sin_t, (n_seg, 1))
    log_gamma = jnp.log(1.0 - jnp.exp2(-5.0 - jnp.arange(RET_H, dtype=F32)))
    tpos = jnp.arange(L, dtype=F32)
    bret = (tpos[:, None] + 1.0) * log_gamma[None, :]
    dmat = tpos[:, None] - tpos[None, :]
    retd = jnp.where(dmat[None] >= 0, jnp.exp(dmat[None] * log_gamma[:, None, None]), 0.0)
    retd = jnp.transpose(retd, (1, 0, 2)).reshape(L, RET_H * L)
    reteb = jnp.repeat(jnp.exp(bret), RET_DK, axis=1)
    retebl = jnp.repeat(jnp.exp(bret[L - 1:L] - bret), RET_DK, axis=1)
    retsdec = jnp.broadcast_to(jnp.repeat(jnp.exp(bret[L - 1]), RET_DK)[:, None], (RET_KW, RET_VW))

    tri, de, lmask = _pack_consts(L, n_seg)
    gn64 = jnp.asarray(_head_block_mask(GLA_DV, GLA_DV, GLA_H) / GLA_DV, BF16)
    blkmask = jnp.asarray(_head_block_mask(GLA_DK, GLA_DV, GLA_H), F32)
    hmk = jnp.asarray(_head_block_mask(L, RET_DK, RET_H), F32)
    hmv = jnp.asarray(_head_block_mask(L, RET_DV, RET_H), F32)

    def full(a):
        nd = a.ndim
        return pl.BlockSpec(a.shape, lambda b, c: (0,) * nd)

    def per_b(a):
        nd = a.ndim
        return pl.BlockSpec((n_state,) + a.shape[1:], lambda b, c: (b,) + (0,) * (nd - 1))

    if seq:
        tab_spec = pl.BlockSpec((R, LANES), lambda b, c: (c, 0))
    else:
        tab_spec = pl.BlockSpec((R, LANES), lambda b, c: (0, 0))
    consts = [retd, reteb, retebl, retsdec]
    states = [sgla0, sret0, sgdn0, conv0]
    params = [prm["wgu"], prm["gbias"], prm["alog"], prm["dtb"], prm["glan"], prm["retn"], prm["gdnn"],
              prm["convw"], tri, de, lmask, gn64, blkmask, hmk, hmv]
    in_specs = ([pl.BlockSpec((R, Z_WIDTH), lambda b, c: (b * steps_per_b + c, 0)), tab_spec, tab_spec]
                + [full(a) for a in consts] + [per_b(a) for a in states] + [full(a) for a in params])
    out_shape = [jax.ShapeDtypeStruct((B * T, D_MODEL), BF16)] + [
        jax.ShapeDtypeStruct(a.shape, F32) for a in states]
    out_specs = [pl.BlockSpec((R, D_MODEL), lambda b, c: (b * steps_per_b + c, 0))] + [per_b(a) for a in states]
    n_cs = 1 if seq else n_seg
    return pl.pallas_call(
        functools.partial(_mixer_kernel, L, n_seg, seq),
        out_shape=out_shape,
        grid=grid,
        in_specs=in_specs,
        out_specs=out_specs,
        scratch_shapes=[pltpu.VMEM((n_cs, R // n_cs + 8, GDN_CONV_CH), F32)],
        compiler_params=_cparams(("parallel", "arbitrary")),
        name="mixer",
    )(zf, cos_t, sin_t, *consts, *states, *params)


def _out_proj_kernel(h_ref, o_ref, w_ref, out_ref):
    out_ref[...] = h_ref[...] + jnp.dot(o_ref[...], w_ref[...], preferred_element_type=F32)


def _out_proj(h, o, w, tm):
    n = h.shape[0]
    return pl.pallas_call(
        _out_proj_kernel,
        out_shape=jax.ShapeDtypeStruct((n, D_MODEL), F32),
        grid=(n // tm,),
        in_specs=[pl.BlockSpec((tm, D_MODEL), lambda i: (i, 0)),
                  pl.BlockSpec((tm, D_MODEL), lambda i: (i, 0)),
                  pl.BlockSpec((D_MODEL, D_MODEL), lambda i: (0, 0))],
        out_specs=pl.BlockSpec((tm, D_MODEL), lambda i: (i, 0)),
        compiler_params=_cparams(("parallel",)),
        name="out_proj",
    )(h, o, w)


def _ffn_kernel(h_ref, g_ref, wg_ref, wu_ref, wd_ref, out_ref, v_ref, acc_ref):
    f = pl.program_id(1)

    @pl.when(f == 0)
    def _():
        v_ref[...] = _rms(h_ref[...], g_ref[...]).astype(BF16)
        acc_ref[...] = h_ref[...]

    v = v_ref[...]
    a = jnp.dot(v, wg_ref[...], preferred_element_type=F32)
    u = jnp.dot(v, wu_ref[...], preferred_element_type=F32)
    acc_ref[...] += jnp.dot((_silu(a) * u).astype(BF16), wd_ref[...], preferred_element_type=F32)

    @pl.when(f == pl.num_programs(1) - 1)
    def _():
        out_ref[...] = acc_ref[...]


def _ffn(h, g, wg, wu, wd, tm, tf):
    n = h.shape[0]
    return pl.pallas_call(
        _ffn_kernel,
        out_shape=jax.ShapeDtypeStruct((n, D_MODEL), F32),
        grid=(n // tm, D_FF // tf),
        in_specs=[pl.BlockSpec((tm, D_MODEL), lambda i, f: (i, 0)),
                  pl.BlockSpec((1, D_MODEL), lambda i, f: (0, 0)),
                  pl.BlockSpec((D_MODEL, tf), lambda i, f: (0, f)),
                  pl.BlockSpec((D_MODEL, tf), lambda i, f: (0, f)),
                  pl.BlockSpec((tf, D_MODEL), lambda i, f: (f, 0))],
        out_specs=pl.BlockSpec((tm, D_MODEL), lambda i, f: (i, 0)),
        scratch_shapes=[pltpu.VMEM((tm, D_MODEL), BF16), pltpu.VMEM((tm, D_MODEL), F32)],
        compiler_params=_cparams(("parallel", "arbitrary")),
        name="ffn",
    )(h, g, wg, wu, wd)


MOE_TM = 512
MOE_BLK = 512
MOE_TF = 512
SEG_ALIGN = 16
SEL_ROWS = 128


def _moe_rows(n):
    n_tiles = n // MOE_TM
    bound = 2 * n + n_tiles * N_EXPERTS * (SEG_ALIGN - 1) + N_EXPERTS * (MOE_BLK - 1)
    return -(-bound // MOE_BLK) * MOE_BLK


def _route_kernel(h_ref, g_ref, r_ref, v_ref, gate_ref, memb_ref, cnt_ref):
    vf = _rms(h_ref[...], g_ref[...])
    v_ref[...] = vf.astype(BF16)
    logits = _hdot(vf, r_ref[...])
    lane = lax.broadcasted_iota(jnp.int32, logits.shape, 1)
    neg = jnp.float32(-jnp.inf)
    lg = jnp.where(lane < N_EXPERTS, logits, neg)
    m1 = jnp.max(lg, axis=1, keepdims=True)
    i1 = jnp.min(jnp.where(lg == m1, lane, LANES), axis=1, keepdims=True)
    lg2 = jnp.where(lane == i1, neg, lg)
    m2 = jnp.max(lg2, axis=1, keepdims=True)
    i2 = jnp.min(jnp.where(lg2 == m2, lane, LANES), axis=1, keepdims=True)
    e2 = jnp.exp(m2 - m1)
    den = 1.0 + e2
    gate_ref[...] = jnp.where(lane == i1, 1.0 / den, 0.0) + jnp.where(lane == i2, e2 / den, 0.0)
    memb = jnp.where((lane == i1) | (lane == i2), 1.0, 0.0)
    memb_ref[...] = memb
    cnt_ref[...] = jnp.broadcast_to(jnp.sum(memb, axis=0, keepdims=True), cnt_ref.shape)


def _moe_route(h, g, router):
    n = h.shape[0]
    tm = MOE_TM
    return pl.pallas_call(
        _route_kernel,
        out_shape=[jax.ShapeDtypeStruct((n, D_MODEL), BF16), jax.ShapeDtypeStruct((n, LANES), F32),
                   jax.ShapeDtypeStruct((n, LANES), F32), jax.ShapeDtypeStruct((n // tm, 8, LANES), F32)],
        grid=(n // tm,),
        in_specs=[pl.BlockSpec((tm, D_MODEL), lambda i: (i, 0)),
                  pl.BlockSpec((1, D_MODEL), lambda i: (0, 0)),
                  pl.BlockSpec((D_MODEL, LANES), lambda i: (0, 0))],
        out_specs=[pl.BlockSpec((tm, D_MODEL), lambda i: (i, 0)),
                   pl.BlockSpec((tm, LANES), lambda i: (i, 0)),
                   pl.BlockSpec((tm, LANES), lambda i: (i, 0)),
                   pl.BlockSpec((None, 8, LANES), lambda i: (i, 0, 0))],
        compiler_params=_cparams(("parallel",)),
        name="moe_route",
    )(h, g, router)


def _seg_pad(c):
    return (c + (SEG_ALIGN - 1)) // SEG_ALIGN * SEG_ALIGN


def _gather_kernel(base_ref, cnt_ref, v_ref, memb_ref, tri_ref, xs_in_ref, xs_ref, stg_ref, sem, nout_ref):
    del xs_in_ref
    i = pl.program_id(0)
    n_steps = pl.num_programs(0)
    par = i % 2
    unit = SEG_ALIGN

    def unit_copy(p, src_row, dst_row):
        return pltpu.make_async_copy(stg_ref.at[p, pl.ds(src_row, unit)], xs_ref.at[pl.ds(dst_row, unit)], sem.at[p])

    def wait_units(p, count):
        def body(u, carry):
            unit_copy(p, 0, 0).wait()
            return carry
        lax.fori_loop(0, count, body, 0)

    @pl.when(i >= 2)
    def _():
        wait_units(par, nout_ref[par])

    tm = v_ref.shape[0]
    memb = memb_ref[...]
    rank = jnp.dot(tri_ref[...], memb.astype(BF16), preferred_element_type=F32)
    rank_t = rank.T
    memb_t = memb.T
    v = v_ref[...]
    ridx = lax.broadcasted_iota(jnp.int32, (SEL_ROWS, tm), 0).astype(F32)

    so = jnp.int32(0)
    for e in range(N_EXPERTS):
        c = cnt_ref[i, e]
        rrow = rank_t[e:e + 1, :]
        mrow = memb_t[e:e + 1, :]

        def chunk(qq, carry, rrow=rrow, mrow=mrow, so=so):
            sel = jnp.where((ridx + (qq * SEL_ROWS).astype(F32) == rrow) & (mrow > 0.0), 1.0, 0.0).astype(BF16)
            xc = jnp.dot(sel, v, preferred_element_type=F32).astype(BF16)
            stg_ref[par, pl.ds(pl.multiple_of(so + qq * SEL_ROWS, unit), SEL_ROWS), :] = xc
            return carry

        lax.fori_loop(0, (c + (SEL_ROWS - 1)) // SEL_ROWS, chunk, 0)
        so = so + _seg_pad(c)

    so = jnp.int32(0)
    for e in range(N_EXPERTS):
        cp = _seg_pad(cnt_ref[i, e])
        dst = base_ref[i, e]

        def send(u, carry, so=so, dst=dst):
            unit_copy(par, pl.multiple_of(so + u * unit, unit), pl.multiple_of(dst + u * unit, unit)).start()
            return carry

        lax.fori_loop(0, cp // unit, send, 0)
        so = so + cp
    nout_ref[par] = so // unit

    @pl.when(i == n_steps - 1)
    def _():
        wait_units(par, nout_ref[par])

        @pl.when(i >= 1)
        def _():
            wait_units(1 - par, nout_ref[1 - par])


def _moe_gather(base, cnt, v, memb, tri, xs_rows):
    n = v.shape[0]
    tm = MOE_TM
    stg_rows = 2 * tm + N_EXPERTS * SEG_ALIGN + SEL_ROWS
    xs0 = jnp.zeros((xs_rows, D_MODEL), BF16)
    grid_spec = pltpu.PrefetchScalarGridSpec(
        num_scalar_prefetch=2,
        grid=(n // tm,),
        in_specs=[pl.BlockSpec((tm, D_MODEL), lambda i, b, c: (i, 0)),
                  pl.BlockSpec((tm, LANES), lambda i, b, c: (i, 0)),
                  pl.BlockSpec((tm, tm), lambda i, b, c: (0, 0)),
                  pl.BlockSpec(memory_space=pl.ANY)],
        out_specs=pl.BlockSpec(memory_space=pl.ANY),
        scratch_shapes=[pltpu.VMEM((2, stg_rows, D_MODEL), BF16), pltpu.SemaphoreType.DMA((2,)),
                        pltpu.SMEM((2,), jnp.int32)],
    )
    return pl.pallas_call(
        _gather_kernel,
        out_shape=jax.ShapeDtypeStruct((xs_rows, D_MODEL), BF16),
        grid_spec=grid_spec,
        input_output_aliases={5: 0},
        compiler_params=_cparams(("arbitrary",)),
        name="moe_gather",
    )(base, cnt, v, memb, tri, xs0)


def _expert_kernel(bexp_ref, bval_ref, xs_ref, wg_ref, wu_ref, wd_ref, ys_ref, acc_ref):
    del bexp_ref
    k = pl.program_id(0)
    f = pl.program_id(1)

    @pl.when(f == 0)
    def _():
        acc_ref[...] = jnp.zeros_like(acc_ref)

    @pl.when(bval_ref[k] > 0)
    def _():
        x = xs_ref[...]
        a = jnp.dot(x, wg_ref[...], preferred_element_type=F32)
        u = jnp.dot(x, wu_ref[...], preferred_element_type=F32)
        acc_ref[...] += jnp.dot((_silu(a) * u).astype(BF16), wd_ref[...], preferred_element_type=F32)

    @pl.when(f == pl.num_programs(1) - 1)
    def _():
        ys_ref[...] = acc_ref[...]


def _moe_experts(bexp, bval, xs, wg, wu, wd):
    n_blk = xs.shape[0] // MOE_BLK
    n_f = D_FF // MOE_TF

    def f_idx(k, f, bval):
        return jnp.where(bval[k] > 0, f, n_f - 1)

    grid_spec = pltpu.PrefetchScalarGridSpec(
        num_scalar_prefetch=2,
        grid=(n_blk + 1, n_f),
        in_specs=[pl.BlockSpec((MOE_BLK, D_MODEL), lambda k, f, be, bv: (jnp.minimum(k, n_blk - 1), 0)),
                  pl.BlockSpec((None, D_MODEL, MOE_TF), lambda k, f, be, bv: (be[k], 0, f_idx(k, f, bv))),
                  pl.BlockSpec((None, D_MODEL, MOE_TF), lambda k, f, be, bv: (be[k], 0, f_idx(k, f, bv))),
                  pl.BlockSpec((None, MOE_TF, D_MODEL), lambda k, f, be, bv: (be[k], f_idx(k, f, bv), 0))],
        out_specs=pl.BlockSpec((MOE_BLK, D_MODEL), lambda k, f, be, bv: (k, 0)),
        scratch_shapes=[pltpu.VMEM((MOE_BLK, D_MODEL), F32)],
    )
    return pl.pallas_call(
        _expert_kernel,
        out_shape=jax.ShapeDtypeStruct(((n_blk + 1) * MOE_BLK, D_MODEL), F32),
        grid_spec=grid_spec,
        compiler_params=_cparams(("arbitrary", "arbitrary")),
        name="moe_experts",
    )(bexp, bval, xs, wg, wu, wd)


def _combine_kernel(base_ref, cnt_ref, h_ref, gate_ref, memb_ref, tri_ref, ys_ref, out_ref, ybuf_ref, acc_ref, sem):
    i = pl.program_id(0)
    n_steps = pl.num_programs(0)
    par = i % 2
    tm = h_ref.shape[0]

    def chunk_copy(p, src_row, slot_row):
        return pltpu.make_async_copy(ys_ref.at[pl.ds(src_row, SEL_ROWS)], ybuf_ref.at[p, pl.ds(slot_row, SEL_ROWS)],
                                     sem.at[p])

    def n_chunks(c):
        return (c + (SEL_ROWS - 1)) // SEL_ROWS

    def fetch(step, p):
        slot = jnp.int32(0)
        for e in range(N_EXPERTS):
            nq = n_chunks(cnt_ref[step, e])
            src = base_ref[step, e]

            def body(qq, carry, slot=slot, src=src):
                chunk_copy(p, pl.multiple_of(src + qq * SEL_ROWS, SEG_ALIGN),
                           pl.multiple_of((slot + qq) * SEL_ROWS, SEL_ROWS)).start()
                return carry

            lax.fori_loop(0, nq, body, 0)
            slot = slot + nq
        return slot

    @pl.when(i == 0)
    def _():
        fetch(0, 0)

    @pl.when(i + 1 < n_steps)
    def _():
        fetch(i + 1, 1 - par)

    total = jnp.int32(0)
    for e in range(N_EXPERTS):
        total = total + n_chunks(cnt_ref[i, e])

    def wait_body(u, carry):
        chunk_copy(par, 0, 0).wait()
        return carry

    lax.fori_loop(0, total, wait_body, 0)

    memb = memb_ref[...]
    gates = gate_ref[...]
    rank = jnp.dot(tri_ref[...], memb.astype(BF16), preferred_element_type=F32)
    cidx = lax.broadcasted_iota(jnp.int32, (tm, SEL_ROWS), 1).astype(F32)
    acc_ref[...] = h_ref[...]

    slot = jnp.int32(0)
    for e in range(N_EXPERTS):
        nq = n_chunks(cnt_ref[i, e])
        rcol = rank[:, e:e + 1]
        mcol = memb[:, e:e + 1]
        gcol = gates[:, e:e + 1]

        def chunk(qq, carry, slot=slot, rcol=rcol, mcol=mcol, gcol=gcol):
            y = ybuf_ref[par, pl.ds(pl.multiple_of((slot + qq) * SEL_ROWS, SEL_ROWS), SEL_ROWS), :]
            sel = jnp.where((cidx + (qq * SEL_ROWS).astype(F32) == rcol) & (mcol > 0.0), 1.0, 0.0).astype(BF16)
            yh = y.astype(BF16)
            yl = (y - yh.astype(F32)).astype(BF16)
            got = jnp.dot(sel, yh, preferred_element_type=F32) + jnp.dot(sel, yl, preferred_element_type=F32)
            acc_ref[...] += gcol * got
            return carry

        lax.fori_loop(0, nq, chunk, 0)
        slot = slot + nq
    out_ref[...] = acc_ref[...]


def _moe_combine(base, cnt, h, gates, memb, tri, ys):
    n = h.shape[0]
    tm = MOE_TM
    max_chunks = 2 * tm // SEL_ROWS + N_EXPERTS
    grid_spec = pltpu.PrefetchScalarGridSpec(
        num_scalar_prefetch=2,
        grid=(n // tm,),
        in_specs=[pl.BlockSpec((tm, D_MODEL), lambda i, b, c: (i, 0)),
                  pl.BlockSpec((tm, LANES), lambda i, b, c: (i, 0)),
                  pl.BlockSpec((tm, LANES), lambda i, b, c: (i, 0)),
                  pl.BlockSpec((tm, tm), lambda i, b, c: (0, 0)),
                  pl.BlockSpec(memory_space=pl.ANY)],
        out_specs=pl.BlockSpec((tm, D_MODEL), lambda i, b, c: (i, 0)),
        scratch_shapes=[pltpu.VMEM((2, max_chunks * SEL_ROWS, D_MODEL), F32), pltpu.VMEM((tm, D_MODEL), F32),
                        pltpu.SemaphoreType.DMA((2,))],
    )
    return pl.pallas_call(
        _combine_kernel,
        out_shape=jax.ShapeDtypeStruct((n, D_MODEL), F32),
        grid_spec=grid_spec,
        compiler_params=_cparams(("arbitrary",)),
        name="moe_combine",
    )(base, cnt, h, gates, memb, tri, ys)


def _moe(h, g, router, wg, wu, wd):
    n = h.shape[0]
    n_tiles = n // MOE_TM
    xs_rows = _moe_rows(n)
    n_blk = xs_rows // MOE_BLK
    v, gates, memb, cnt_f = _moe_route(h, g, router)
    cnt = cnt_f[:, 0, :N_EXPERTS].astype(jnp.int32)
    cp = _seg_pad(cnt)
    exp_rows = -(-jnp.sum(cp, axis=0) // MOE_BLK) * MOE_BLK
    exp_end = jnp.cumsum(exp_rows)
    exp_start = exp_end - exp_rows
    base = (exp_start[None, :] + jnp.cumsum(cp, axis=0) - cp).astype(jnp.int32)
    blk_row = jnp.arange(n_blk + 1, dtype=jnp.int32) * MOE_BLK
    bval = (blk_row < exp_end[-1]).astype(jnp.int32)
    bexp = jnp.minimum(jnp.sum((blk_row[:, None] >= exp_end[None, :]).astype(jnp.int32), axis=1), N_EXPERTS - 1)
    last_valid = jnp.maximum(jnp.sum(bval) - 1, 0)
    bexp = jnp.where(bval > 0, bexp, bexp[last_valid]).astype(jnp.int32)
    tri = jnp.asarray(np.tril(np.ones((MOE_TM, MOE_TM), np.float32), -1), BF16)
    xs = _moe_gather(base, cnt, v, memb, tri, xs_rows)
    ys = _moe_experts(bexp, bval, xs, wg, wu, wd)
    return _moe_combine(base, cnt, h, gates, memb, tri, ys)


def _ple_kernel(final, h_ref, p_ref, g_ref, wup_ref, wgate_ref, gf_ref, out_ref):
    h = h_ref[...]
    up = jnp.dot(p_ref[...].astype(BF16), wup_ref[...], preferred_element_type=F32)
    gt = jnp.dot(_rms(h, g_ref[...]).astype(BF16), wgate_ref[...], preferred_element_type=F32)
    hn = h + up * jax.nn.sigmoid(gt)
    if final:
        hn = _rms(hn, gf_ref[...])
    out_ref[...] = hn


def _ple(h, p, g, wup, wgate, gf, final, tm):
    n = h.shape[0]
    return pl.pallas_call(
        functools.partial(_ple_kernel, final),
        out_shape=jax.ShapeDtypeStruct((n, D_MODEL), F32),
        grid=(n // tm,),
        in_specs=[pl.BlockSpec((tm, D_MODEL), lambda i: (i, 0)),
                  pl.BlockSpec((tm, PLE_DIM), lambda i: (i, 0)),
                  pl.BlockSpec((1, D_MODEL), lambda i: (0, 0)),
                  pl.BlockSpec((PLE_DIM, D_MODEL), lambda i: (0, 0)),
                  pl.BlockSpec((D_MODEL, D_MODEL), lambda i: (0, 0)),
                  pl.BlockSpec((1, D_MODEL), lambda i: (0, 0))],
        out_specs=pl.BlockSpec((tm, D_MODEL), lambda i: (i, 0)),
        compiler_params=_cparams(("parallel",)),
        name="ple",
    )(h, p, g, wup, wgate, gf)


def _reorder_w_in(w):
    sizes = (GLA_KW, GLA_KW, GLA_VW, GLA_RANK, GLA_VW, RET_KW, RET_KW, RET_VW, RET_VW,
             GDN_CONV_CH, GDN_H, GDN_H, GDN_VW)
    pts = np.cumsum(np.array(sizes))[:-1].tolist()
    gq, gk, gv, glr, gr, rq, rk, rv, rr, dqkv, da, db, dz = jnp.split(w, pts, axis=1)
    pad = jnp.zeros((w.shape[0], LANES - GLA_RANK - 2 * GDN_H), w.dtype)
    return jnp.concatenate([gq, gk, gv, gr, rq, rk, rv, rr, dqkv, dz, glr, da, db, pad], axis=1)


def _lane_row(vals, offset):
    return jnp.zeros((1, LANES), F32).at[0, offset:offset + vals.shape[0]].set(vals.astype(F32))


def _to_blockdiag(s):
    B, H, dk, dv = s.shape
    eye = jnp.eye(H, dtype=s.dtype)
    return jnp.einsum("bhde,hg->bhdge", s, eye).reshape(B, H * dk, H * dv)


def _from_blockdiag(s, H):
    B, R, C = s.shape
    dk, dv = R // H, C // H
    s5 = s.reshape(B, H, dk, H, dv)
    return jnp.stack([s5[:, h, :, h, :] for h in range(H)], axis=1)


def kernel(x_prompt, x_sample, state_gla, state_ret, state_gdn, state_gdn_conv, p_prompt, p_sample, norm_mix, w_in, gla_w_gate_up, gla_b_gate, gla_norm, ret_norm, gdn_conv, gdn_a_log, gdn_dt_bias, gdn_norm, w_out, norm_ffn, ffn_w_gate, ffn_w_up, ffn_w_down, moe_router, moe_w_gate, moe_w_up, moe_w_down, ple_w_up, ple_norm, ple_w_gate, norm_final):
    depth = w_in.shape[0]
    Bp, Tp, _ = x_prompt.shape
    Bs, Ts, _ = x_sample.shape
    n_p, n_s = Bp * Tp, Bs * Ts
    tm = 512

    h = jnp.concatenate([x_prompt.reshape(n_p, D_MODEL), x_sample.reshape(n_s, D_MODEL)], axis=0)
    p_all = jnp.concatenate([p_prompt.reshape(depth, n_p, PLE_DIM), p_sample.reshape(depth, n_s, PLE_DIM)], axis=1)

    outs_p = [[], [], [], []]
    outs_s = [[], [], [], []]
    for i in range(depth):
        w_in_k = _reorder_w_in(w_in[i]).astype(BF16)
        prm = dict(
            wgu=jnp.zeros((LANES, GLA_KW), F32).at[:GLA_RANK].set(gla_w_gate_up[i]).astype(BF16),
            gbias=gla_b_gate[i].reshape(1, GLA_KW).astype(F32),
            alog=_lane_row(gdn_a_log[i], SM_DA),
            dtb=_lane_row(gdn_dt_bias[i], SM_DA),
            glan=jnp.tile(gla_norm[i].astype(F32), GLA_H).reshape(1, GLA_VW),
            retn=jnp.tile(ret_norm[i].astype(F32), RET_H).reshape(1, RET_VW),
            gdnn=gdn_norm[i].astype(F32).reshape(1, GDN_DV),
            convw=gdn_conv[i].astype(F32),
        )
        z = _norm_proj(h, norm_mix[i].reshape(1, D_MODEL), w_in_k, tm)
        z_p = z[:n_p].reshape(Bp, Tp, Z_WIDTH)
        z_s = z[n_p:].reshape(Bs, Ts, Z_WIDTH)
        o_p, gla_p, ret_p, gdn_p, conv_p = _mixer(
            z_p, 0, True,
            jnp.zeros((Bp, GLA_KW, GLA_VW), F32), jnp.zeros((Bp, RET_KW, RET_VW), F32),
            jnp.zeros((Bp, GDN_H, GDN_DK, GDN_DV), F32), jnp.zeros((Bp, GDN_CONV_W - 1, GDN_CONV_CH), F32), prm)
        o_s, gla_s, ret_s, gdn_s, conv_s = _mixer(
            z_s, PAST_LEN, False,
            _to_blockdiag(state_gla[i].astype(F32)), _to_blockdiag(state_ret[i].astype(F32)),
            state_gdn[i].astype(F32), state_gdn_conv[i].astype(F32), prm)
        for lst, val in zip(outs_p, (_from_blockdiag(gla_p, GLA_H), _from_blockdiag(ret_p, RET_H), gdn_p, conv_p)):
            lst.append(val)
        for lst, val in zip(outs_s, (_from_blockdiag(gla_s, GLA_H), _from_blockdiag(ret_s, RET_H), gdn_s, conv_s)):
            lst.append(val)
        o = jnp.concatenate([o_p, o_s], axis=0)
        h = _out_proj(h, o, w_out[i].astype(BF16), tm)
        if i % 2 == 0:
            j = i // 2
            h = _ffn(h, norm_ffn[i].reshape(1, D_MODEL), ffn_w_gate[j].astype(BF16), ffn_w_up[j].astype(BF16),
                     ffn_w_down[j].astype(BF16), tm, 512)
        else:
            j = i // 2
            router = jnp.zeros((D_MODEL, LANES), F32).at[:, :N_EXPERTS].set(moe_router[j])
            h = _moe(h, norm_ffn[i].reshape(1, D_MODEL), router, moe_w_gate[j].astype(BF16),
                     moe_w_up[j].astype(BF16), moe_w_down[j].astype(BF16))
        h = _ple(h, p_all[i], ple_norm[i].reshape(1, D_MODEL), ple_w_up[i].astype(BF16),
                 ple_w_gate[i].astype(BF16), norm_final.reshape(1, D_MODEL), i == depth - 1, tm)

    y_p = h[:n_p].reshape(Bp, Tp, D_MODEL)
    y_s = h[n_p:].reshape(Bs, Ts, D_MODEL)
    return (y_p, y_s,
            jnp.stack(outs_p[0]), jnp.stack(outs_p[1]), jnp.stack(outs_p[2]), jnp.stack(outs_p[3]),
            jnp.stack(outs_s[0]), jnp.stack(outs_s[1]), jnp.stack(outs_s[2]), jnp.stack(outs_s[3]))
```

```python
import functools
import math

import jax
import jax.numpy as jnp
import numpy as np
from jax import lax
from jax.experimental import pallas as pl
from jax.experimental.pallas import tpu as pltpu

F32 = jnp.float32
BF16 = jnp.bfloat16
HIGHEST = lax.Precision.HIGHEST

D_MODEL = 1024
CHUNK = 64
PLE_DIM = 256
EPS = 1e-6
PAST_LEN = 4096

GLA_H, GLA_DK, GLA_DV, GLA_RANK, GLA_TAU = 4, 32, 64, 16, 16.0
RET_H, RET_DK, RET_DV = 4, 32, 64
ROPE_BASE = 10000.0
GDN_H, GDN_DK, GDN_DV, GDN_CONV_W = 4, 128, 128, 4
GLA_KW, GLA_VW = GLA_H * GLA_DK, GLA_H * GLA_DV
RET_KW, RET_VW = RET_H * RET_DK, RET_H * RET_DV
GDN_KW, GDN_VW = GDN_H * GDN_DK, GDN_H * GDN_DV
GDN_CONV_CH = 2 * GDN_KW + GDN_VW
D_FF = 3584
N_EXPERTS = 8

LANES = 128

Z_GQ, Z_GK, Z_GV, Z_GR = 0, 128, 256, 512
Z_RQ, Z_RK, Z_RV, Z_RR = 768, 896, 1024, 1280
Z_CONV = 1536
Z_DZ = Z_CONV + GDN_CONV_CH
Z_SMALL = Z_DZ + GDN_VW
Z_WIDTH = Z_SMALL + LANES
SM_DA, SM_DB = GLA_RANK, GLA_RANK + GDN_H

MIX_ROWS = 256
TOKEN_TILE = 512
VMEM_LIMIT = 56 * 1024 * 1024


def _cparams(sem):
    return pltpu.CompilerParams(dimension_semantics=sem, vmem_limit_bytes=VMEM_LIMIT)


def _rms(x, g):
    return x * lax.rsqrt(jnp.mean(x * x, axis=-1, keepdims=True) + EPS) * g


def _bdot(a, b):
    return jnp.dot(a.astype(BF16), b.astype(BF16), preferred_element_type=F32)


def _bdot_t(a, b):
    return lax.dot_general(a.astype(BF16), b.astype(BF16), (((1,), (1,)), ((), ())),
                           preferred_element_type=F32)


def _bdot_tl(a, b):
    return lax.dot_general(a.astype(BF16), b.astype(BF16), (((0,), (0,)), ((), ())),
                           preferred_element_type=F32)


def _hdot(a, b):
    return jnp.dot(a, b, precision=HIGHEST, preferred_element_type=F32)


def _silu(x):
    return x * jax.nn.sigmoid(x)


def _split3(a):
    hi = a.astype(BF16)
    r1 = a - hi.astype(F32)
    mid = r1.astype(BF16)
    lo = (r1 - mid.astype(F32)).astype(BF16)
    return [hi, mid, lo]


def _sum3(x):
    return x[:, 0:LANES] + x[:, LANES:2 * LANES] + x[:, 2 * LANES:3 * LANES]


def _norm_proj_kernel(h_ref, g_ref, w_ref, z_ref):
    u = _rms(h_ref[...], g_ref[...])
    z_ref[...] = jnp.dot(u.astype(BF16), w_ref[...], preferred_element_type=F32)


def _norm_proj(h, g, w, tm):
    n = h.shape[0]
    return pl.pallas_call(
        _norm_proj_kernel,
        out_shape=jax.ShapeDtypeStruct((n, w.shape[1]), F32),
        grid=(n // tm,),
        in_specs=[pl.BlockSpec((tm, D_MODEL), lambda i: (i, 0)),
                  pl.BlockSpec((1, D_MODEL), lambda i: (0, 0)),
                  pl.BlockSpec(w.shape, lambda i: (0, 0))],
        out_specs=pl.BlockSpec((tm, w.shape[1]), lambda i: (i, 0)),
        compiler_params=_cparams(("parallel",)),
        name="norm_proj",
    )(h, g, w)


def _two_src_specs(tm, width, split):
    return [pl.BlockSpec((tm, width), lambda i: (jnp.minimum(i, split - 1), 0)),
            pl.BlockSpec((tm, width), lambda i: (jnp.maximum(i - split, 0), 0))]


def _two_src_load(a_ref, b_ref, split):
    return jnp.where(pl.program_id(0) < split, a_ref[...], b_ref[...])


def _norm_proj2_kernel(split, xa_ref, xb_ref, g_ref, w_ref, z_ref, h_ref):
    x = _two_src_load(xa_ref, xb_ref, split)
    h_ref[...] = x
    z_ref[...] = jnp.dot(_rms(x, g_ref[...]).astype(BF16), w_ref[...], preferred_element_type=F32)


def _norm_proj2(xa, xb, g, w, tm):
    n = xa.shape[0] + xb.shape[0]
    split = xa.shape[0] // tm
    return pl.pallas_call(
        functools.partial(_norm_proj2_kernel, split),
        out_shape=[jax.ShapeDtypeStruct((n, w.shape[1]), F32), jax.ShapeDtypeStruct((n, D_MODEL), F32)],
        grid=(n // tm,),
        in_specs=_two_src_specs(tm, D_MODEL, split) + [pl.BlockSpec((1, D_MODEL), lambda i: (0, 0)),
                                                      pl.BlockSpec(w.shape, lambda i: (0, 0))],
        out_specs=[pl.BlockSpec((tm, w.shape[1]), lambda i: (i, 0)),
                   pl.BlockSpec((tm, D_MODEL), lambda i: (i, 0))],
        compiler_params=_cparams(("parallel",)),
        name="norm_proj_first",
    )(xa, xb, g, w)


def _mixer_kernel(L, n_seg, seq,
                  z_ref, cos_ref, sin_ref, retd_ref, reteb_ref, retebl_ref, retsdec_ref,
                  sgla0_ref, sret0_ref, sgdn0_ref, conv0_ref,
                  wgu_ref, gbias_ref, alog_ref, dtb_ref, glan_ref, retn_ref, gdnn_ref, convw_ref,
                  tri_ref, de_ref, lmask_ref, gn64_ref, blkmask_ref, hmk_ref, hmv_ref,
                  o_ref, sgla_ref, sret_ref, sgdn_ref, convo_ref,
                  xpad_ref):
    R = L * n_seg
    n_lev = int(math.log2(L))
    c = pl.program_id(1)

    @pl.when(c == 0)
    def _():
        sgla_ref[...] = sgla0_ref[...]
        sret_ref[...] = sret0_ref[...]
        sgdn_ref[...] = sgdn0_ref[...]
        convo_ref[...] = conv0_ref[...]

    blkmask = blkmask_ref[...]
    hmk = hmk_ref[...]
    hmv = hmv_ref[...]
    small = z_ref[:, Z_SMALL:Z_SMALL + LANES]

    def slot(j):
        return 0 if seq else j

    def rows(j):
        return slice(j * L, (j + 1) * L)

    la = jax.nn.log_sigmoid(_bdot(small, wgu_ref[...]) + gbias_ref[...]) * (1.0 / GLA_TAU)
    g_all = -jnp.exp(alog_ref[...]) * jax.nn.softplus(small + dtb_ref[...])
    beta_all = jax.nn.sigmoid(small)
    la_parts = _split3(la)
    la3 = jnp.concatenate(la_parts, axis=1)
    g3 = jnp.concatenate(_split3(g_all), axis=1)
    cs = jnp.dot(tri_ref[...], jnp.concatenate([la3, g3], axis=1), preferred_element_type=F32)
    b = _sum3(cs[:, 0:3 * LANES])
    bg = _sum3(cs[:, 3 * LANES:6 * LANES])
    de = jnp.dot(de_ref[...], jnp.concatenate(la_parts[0:2], axis=1), preferred_element_type=F32)
    ede = jnp.exp(de[:, 0:LANES] + de[:, LANES:2 * LANES])

    def group_norm_gate(o, g, gate):
        sq = o * o
        hi = sq.astype(BF16)
        lo = (sq - hi.astype(F32)).astype(BF16)
        gn = gn64_ref[...]
        ms = (jnp.dot(hi, gn, preferred_element_type=F32) + jnp.dot(lo, gn, preferred_element_type=F32))
        return o * lax.rsqrt(ms + EPS) * g * _silu(gate)

    def intra_scores(q_list, kblk_list, masks):
        att = None
        for qm, kb, mk in zip(q_list, kblk_list, masks):
            term = mk * _bdot_t(qm, kb)
            att = term if att is None else att + term
        return att

    q = z_ref[:, Z_GQ:Z_GQ + GLA_KW] * (GLA_DK ** -0.5)
    k = z_ref[:, Z_GK:Z_GK + GLA_KW]
    v = z_ref[:, Z_GV:Z_GV + GLA_VW]
    qe = q * jnp.exp(b)
    b_t = b.T
    s_gla = sgla_ref[0] if seq else None
    o_parts = []
    for j in range(n_seg):
        r = rows(j)
        qj, kj, vj = q[r], k[r], v[r]
        kblk0 = jnp.concatenate([kj] * GLA_H, axis=0) * hmk
        q_list, kb_list, masks = [], [], []
        for lv in range(n_lev):
            eq = ede[lv * R + j * L:lv * R + (j + 1) * L]
            ek = ede[(n_lev + lv) * R + j * L:(n_lev + lv) * R + (j + 1) * L]
            q_list.append(qj * eq)
            kb_list.append(kblk0 * jnp.concatenate([ek] * GLA_H, axis=0))
            masks.append(lmask_ref[lv])
        q_list.append(qj)
        kb_list.append(kblk0)
        masks.append(lmask_ref[n_lev])
        att = intra_scores(q_list, kb_list, masks)
        vblk = jnp.concatenate([vj] * GLA_H, axis=0) * hmv
        last = (j + 1) * L - 1
        bl = b[last:last + 1, :]
        kv = blkmask * _bdot_tl(kj * jnp.exp(bl - b[r]), vj)
        decc = jnp.exp(b_t[:, last:last + 1])
        if not seq:
            s_gla = sgla_ref[j]
        o_parts.append(_bdot(att, vblk) + _bdot(qe[r], s_gla))
        s_gla = s_gla * decc + kv
        if not seq:
            sgla_ref[j] = s_gla
    if seq:
        sgla_ref[0] = s_gla
    o_gla = jnp.concatenate(o_parts, axis=0)
    o_ref[:, 0:GLA_VW] = group_norm_gate(o_gla, glan_ref[...], z_ref[:, Z_GR:Z_GR + GLA_VW]).astype(o_ref.dtype)

    cos = cos_ref[...]
    sin = sin_ref[...]
    lane = lax.broadcasted_iota(jnp.int32, (R, LANES), 1)
    first_half = (lane % RET_DK) < (RET_DK // 2)

    def rot(xx):
        sw = jnp.where(first_half, pltpu.roll(xx, LANES - RET_DK // 2, 1), pltpu.roll(xx, RET_DK // 2, 1))
        return xx * cos + sw * sin

    q = rot(z_ref[:, Z_RQ:Z_RQ + RET_KW]) * (RET_DK ** -0.5)
    k = rot(z_ref[:, Z_RK:Z_RK + RET_KW])
    v = z_ref[:, Z_RV:Z_RV + RET_VW]
    retd = retd_ref[...]
    reteb = reteb_ref[...]
    retebl = retebl_ref[...]
    retsdec = retsdec_ref[...]
    s_ret = sret_ref[0] if seq else None
    o_parts = []
    for j in range(n_seg):
        r = rows(j)
        qj, kj, vj = q[r], k[r], v[r]
        kblk = jnp.concatenate([kj] * RET_H, axis=0) * hmk
        att = _bdot_t(qj, kblk) * retd
        vblk = jnp.concatenate([vj] * RET_H, axis=0) * hmv
        kv = blkmask * _bdot_tl(kj * retebl, vj)
        if not seq:
            s_ret = sret_ref[j]
        o_parts.append(_bdot(att, vblk) + _bdot(qj * reteb, s_ret))
        s_ret = s_ret * retsdec + kv
        if not seq:
            sret_ref[j] = s_ret
    if seq:
        sret_ref[0] = s_ret
    o_ret = jnp.concatenate(o_parts, axis=0)
    o_ref[:, GLA_VW:GLA_VW + RET_VW] = group_norm_gate(
        o_ret, retn_ref[...], z_ref[:, Z_RR:Z_RR + RET_VW]).astype(o_ref.dtype)

    n_cs = 1 if seq else n_seg
    Lc = R // n_cs
    cw = convw_ref[...]
    xin = z_ref[:, Z_CONV:Z_CONV + GDN_CONV_CH]
    c_parts = []
    for s in range(n_cs):
        xs = xin[s * Lc:(s + 1) * Lc]
        xpad_ref[s, 8 - (GDN_CONV_W - 1):8, :] = convo_ref[s]
        xpad_ref[s, 8:8 + Lc, :] = xs
        convo_ref[s] = xs[Lc - (GDN_CONV_W - 1):Lc, :]
        cacc = xs * cw[GDN_CONV_W - 1:GDN_CONV_W, :]
        for jj in range(GDN_CONV_W - 1):
            sh = GDN_CONV_W - 1 - jj
            cacc = cacc + xpad_ref[s, 8 - sh:8 - sh + Lc, :] * cw[jj:jj + 1, :]
        c_parts.append(cacc)
    cact = _silu(c_parts[0] if n_cs == 1 else jnp.concatenate(c_parts, axis=0))

    row = lax.broadcasted_iota(jnp.int32, (R, R), 0)
    col = lax.broadcasted_iota(jnp.int32, (R, R), 1)
    same = lax.shift_right_logical(row, n_lev) == lax.shift_right_logical(col, n_lev)
    tril = same & (row >= col)
    strict = same & (row > col)
    eye_f = (row == col).astype(F32)
    bg_t = bg.T
    beta_t = beta_all.T
    gdnn = gdnn_ref[...]

    def l2n(xx):
        return xx * lax.rsqrt(jnp.sum(xx * xx, axis=-1, keepdims=True) + EPS)

    for h in range(GDN_H):
        qh = l2n(cact[:, h * GDN_DK:(h + 1) * GDN_DK]) * (GDN_DK ** -0.5)
        kh = l2n(cact[:, GDN_KW + h * GDN_DK:GDN_KW + (h + 1) * GDN_DK])
        vh = cact[:, 2 * GDN_KW + h * GDN_DV:2 * GDN_KW + (h + 1) * GDN_DV]
        bcol = bg[:, SM_DA + h:SM_DA + h + 1]
        brow = bg_t[SM_DA + h:SM_DA + h + 1, :]
        beta_row = beta_t[SM_DB + h:SM_DB + h + 1, :]
        beta_col = beta_all[:, SM_DB + h:SM_DB + h + 1]
        dec = jnp.exp(jnp.where(tril, bcol - brow, -jnp.inf))
        khb = kh.astype(BF16)
        m = _bdot_t(khb, khb) * jnp.where(strict, dec, 0.0) * beta_row
        tinv = eye_f - m
        pw = m.astype(BF16)
        span = 2
        while span < L:
            pwf = jnp.dot(pw, pw, preferred_element_type=F32)
            pw = pwf.astype(BF16)
            tinv = tinv + jnp.dot(tinv.astype(BF16), pw, preferred_element_type=F32)
            span *= 2
        ebc = jnp.exp(bcol)
        x1 = _bdot(tinv, jnp.concatenate([vh, ebc * kh], axis=1)).astype(BF16)
        qk = _bdot_t(qh, khb) * dec * beta_row
        x2 = _bdot(qk, x1)
        o0 = x2[:, 0:GDN_DV]
        qeff = ebc * qh - x2[:, GDN_DV:2 * GDN_DV]
        s_h = sgdn_ref[0, h] if seq else None
        o_parts = []
        for j in range(n_seg):
            r = rows(j)
            last = (j + 1) * L - 1
            blg = bcol[last:last + 1, :]
            kt = kh[r] * (jnp.exp(blg - bcol[r]) * beta_col[r])
            x3 = _bdot_tl(kt, x1[r])
            if not seq:
                s_h = sgdn_ref[j, h]
            y = _bdot(jnp.concatenate([qeff[r], x3[:, GDN_DV:2 * GDN_DV]], axis=0), s_h)
            o_parts.append(o0[r] + y[0:L])
            s_h = jnp.exp(blg) * s_h - y[L:L + GDN_DK] + x3[:, 0:GDN_DV]
            if not seq:
                sgdn_ref[j, h] = s_h
        if seq:
            sgdn_ref[0, h] = s_h
        oh = jnp.concatenate(o_parts, axis=0)
        oh = _rms(oh, gdnn) * _silu(z_ref[:, Z_DZ + h * GDN_DV:Z_DZ + (h + 1) * GDN_DV])
        lo = GLA_VW + RET_VW + h * GDN_DV
        o_ref[:, lo:lo + GDN_DV] = oh.astype(o_ref.dtype)


def _head_block_mask(rows_per_head, cols_per_head, heads):
    r = np.arange(rows_per_head * heads)[:, None] // rows_per_head
    c = np.arange(cols_per_head * heads)[None, :] // cols_per_head
    return (r == c).astype(np.float32)


def _pack_consts(L, n_seg):
    R = L * n_seg
    t = np.arange(R)
    tl = t % L
    base = t - tl
    jj = np.arange(R)[None, :]
    tri = ((jj // L) == (t[:, None] // L)) & (jj <= t[:, None])
    tt = np.arange(L)[:, None]
    ss = np.arange(L)[None, :]
    d_list, e_list, masks = [], [], []
    m = L // 2
    while m >= 1:
        second = (tl // m) % 2 == 1
        ref = base + (tl // (2 * m)) * 2 * m + m - 1
        d_list.append(second[:, None] & (jj > ref[:, None]) & (jj <= t[:, None]))
        e_list.append((~second)[:, None] & (jj > t[:, None]) & (jj <= ref[:, None]))
        mk = (tt // (2 * m) == ss // (2 * m)) & ((tt // m) % 2 == 1) & ((ss // m) % 2 == 0)
        masks.append(np.tile(mk, (1, GLA_H)))
        m //= 2
    masks.append(np.tile(np.eye(L, dtype=bool), (1, GLA_H)))
    de = np.concatenate(d_list + e_list, axis=0)
    return (jnp.asarray(tri, BF16), jnp.asarray(de, BF16), jnp.asarray(np.stack(masks), F32))


def _mixer(zf, row0, B, T, pos0, seq, sgla0, sret0, sgdn0, conv0, prm):
    blk0 = row0 // MIX_ROWS
    if seq:
        L = CHUNK
        n_seg = MIX_ROWS // L
        grid = (B, T // MIX_ROWS)
        n_state = 1
    else:
        L = T
        n_seg = MIX_ROWS // L
        grid = (B // n_seg, 1)
        n_state = n_seg
    R = MIX_ROWS
    steps_per_b = grid[1]
    half = RET_DK // 2
    inv = ROPE_BASE ** (-jnp.arange(half, dtype=F32) / half)
    ang = (pos0 + jnp.arange(T, dtype=jnp.int32)).astype(F32)[:, None] * inv[None, :]
    cos_h = jnp.concatenate([jnp.cos(ang), jnp.cos(ang)], axis=1)
    sin_h = jnp.concatenate([-jnp.sin(ang), jnp.sin(ang)], axis=1)
    cos_t = jnp.tile(cos_h, (1, RET_H))
    sin_t = jnp.tile(sin_h, (1, RET_H))
    if not seq:
        cos_t = jnp.tile(cos_t, (n_seg, 1))
        sin_t = jnp.tile(sin_t, (n_seg, 1))
    log_gamma = jnp.log(1.0 - jnp.exp2(-5.0 - jnp.arange(RET_H, dtype=F32)))
    tpos = jnp.arange(L, dtype=F32)
    bret = (tpos[:, None] + 1.0) * log_gamma[None, :]
    dmat = tpos[:, None] - tpos[None, :]
    retd = jnp.where(dmat[None] >= 0, jnp.exp(dmat[None] * log_gamma[:, None, None]), 0.0)
    retd = jnp.transpose(retd, (1, 0, 2)).reshape(L, RET_H * L)
    reteb = jnp.repeat(jnp.exp(bret), RET_DK, axis=1)
    retebl = jnp.repeat(jnp.exp(bret[L - 1:L] - bret), RET_DK, axis=1)
    retsdec = jnp.broadcast_to(jnp.repeat(jnp.exp(bret[L - 1]), RET_DK)[:, None], (RET_KW, RET_VW))

    tri, de, lmask = _pack_consts(L, n_seg)
    gn64 = jnp.asarray(_head_block_mask(GLA_DV, GLA_DV, GLA_H) / GLA_DV, BF16)
    blkmask = jnp.asarray(_head_block_mask(GLA_DK, GLA_DV, GLA_H), F32)
    hmk = jnp.asarray(_head_block_mask(L, RET_DK, RET_H), F32)
    hmv = jnp.asarray(_head_block_mask(L, RET_DV, RET_H), F32)

    def full(a):
        nd = a.ndim
        return pl.BlockSpec(a.shape, lambda b, c: (0,) * nd)

    def per_b(a):
        nd = a.ndim
        return pl.BlockSpec((n_state,) + a.shape[1:], lambda b, c: (b,) + (0,) * (nd - 1))

    if seq:
        tab_spec = pl.BlockSpec((R, LANES), lambda b, c: (c, 0))
    else:
        tab_spec = pl.BlockSpec((R, LANES), lambda b, c: (0, 0))
    consts = [retd, reteb, retebl, retsdec]
    states = [sgla0, sret0, sgdn0, conv0]
    params = [prm["wgu"], prm["gbias"], prm["alog"], prm["dtb"], prm["glan"], prm["retn"], prm["gdnn"],
              prm["convw"], tri, de, lmask, gn64, blkmask, hmk, hmv]
    in_specs = ([pl.BlockSpec((R, Z_WIDTH), lambda b, c: (blk0 + b * steps_per_b + c, 0)), tab_spec, tab_spec]
                + [full(a) for a in consts] + [per_b(a) for a in states] + [full(a) for a in params])
    out_shape = [jax.ShapeDtypeStruct((B * T, D_MODEL), BF16)] + [
        jax.ShapeDtypeStruct(a.shape, F32) for a in states]
    out_specs = [pl.BlockSpec((R, D_MODEL), lambda b, c: (b * steps_per_b + c, 0))] + [per_b(a) for a in states]
    n_cs = 1 if seq else n_seg
    return pl.pallas_call(
        functools.partial(_mixer_kernel, L, n_seg, seq),
        out_shape=out_shape,
        grid=grid,
        in_specs=in_specs,
        out_specs=out_specs,
        scratch_shapes=[pltpu.VMEM((n_cs, R // n_cs + 8, GDN_CONV_CH), F32)],
        compiler_params=_cparams(("parallel", "arbitrary")),
        name="mixer",
    )(zf, cos_t, sin_t, *consts, *states, *params)


def _out_proj_kernel(split, h_ref, oa_ref, ob_ref, w_ref, out_ref):
    o = _two_src_load(oa_ref, ob_ref, split)
    out_ref[...] = h_ref[...] + jnp.dot(o, w_ref[...], preferred_element_type=F32)


def _out_proj(h, oa, ob, w, tm):
    n = h.shape[0]
    split = oa.shape[0] // tm
    return pl.pallas_call(
        functools.partial(_out_proj_kernel, split),
        out_shape=jax.ShapeDtypeStruct((n, D_MODEL), F32),
        grid=(n // tm,),
        in_specs=[pl.BlockSpec((tm, D_MODEL), lambda i: (i, 0))] + _two_src_specs(tm, D_MODEL, split)
        + [pl.BlockSpec((D_MODEL, D_MODEL), lambda i: (0, 0))],
        out_specs=pl.BlockSpec((tm, D_MODEL), lambda i: (i, 0)),
        compiler_params=_cparams(("parallel",)),
        name="out_proj",
    )(h, oa, ob, w)


FF_TILE = 1792
FF_SUB = 896


def _swiglu_partial(v, wg_ref, wu_ref, wd_ref):
    parts = []
    for c0 in range(0, wg_ref.shape[1], FF_SUB):
        a = jnp.dot(v, wg_ref[:, c0:c0 + FF_SUB], preferred_element_type=F32)
        u = jnp.dot(v, wu_ref[:, c0:c0 + FF_SUB], preferred_element_type=F32)
        parts.append((_silu(a) * u).astype(BF16))
    return jnp.dot(jnp.concatenate(parts, axis=1), wd_ref[...], preferred_element_type=F32)


def _ffn_kernel(h_ref, g_ref, wg_ref, wu_ref, wd_ref, out_ref, v_ref, acc_ref):
    f = pl.program_id(1)

    @pl.when(f == 0)
    def _():
        v_ref[...] = _rms(h_ref[...], g_ref[...]).astype(BF16)
        acc_ref[...] = h_ref[...]

    acc_ref[...] += _swiglu_partial(v_ref[...], wg_ref, wu_ref, wd_ref)

    @pl.when(f == pl.num_programs(1) - 1)
    def _():
        out_ref[...] = acc_ref[...]


def _ffn(h, g, wg, wu, wd, tm, tf):
    n = h.shape[0]
    return pl.pallas_call(
        _ffn_kernel,
        out_shape=jax.ShapeDtypeStruct((n, D_MODEL), F32),
        grid=(n // tm, D_FF // tf),
        in_specs=[pl.BlockSpec((tm, D_MODEL), lambda i, f: (i, 0)),
                  pl.BlockSpec((1, D_MODEL), lambda i, f: (0, 0)),
                  pl.BlockSpec((D_MODEL, tf), lambda i, f: (0, f)),
                  pl.BlockSpec((D_MODEL, tf), lambda i, f: (0, f)),
                  pl.BlockSpec((tf, D_MODEL), lambda i, f: (f, 0))],
        out_specs=pl.BlockSpec((tm, D_MODEL), lambda i, f: (i, 0)),
        scratch_shapes=[pltpu.VMEM((tm, D_MODEL), BF16), pltpu.VMEM((tm, D_MODEL), F32)],
        compiler_params=_cparams(("parallel", "arbitrary")),
        name="ffn",
    )(h, g, wg, wu, wd)


MOE_TM = 512
MOE_BLK = 512
MOE_TF = FF_TILE
SEG_ALIGN = 16
SEL_ROWS = 128


def _moe_rows(n):
    n_tiles = n // MOE_TM
    bound = 2 * n + n_tiles * N_EXPERTS * (SEG_ALIGN - 1) + N_EXPERTS * (MOE_BLK - 1)
    return -(-bound // MOE_BLK) * MOE_BLK


def _route_kernel(h_ref, g_ref, r_ref, v_ref, gate_ref, memb_ref, cnt_ref):
    vf = _rms(h_ref[...], g_ref[...])
    v_ref[...] = vf.astype(BF16)
    logits = _hdot(vf, r_ref[...])
    lane = lax.broadcasted_iota(jnp.int32, logits.shape, 1)
    neg = jnp.float32(-jnp.inf)
    lg = jnp.where(lane < N_EXPERTS, logits, neg)
    m1 = jnp.max(lg, axis=1, keepdims=True)
    i1 = jnp.min(jnp.where(lg == m1, lane, LANES), axis=1, keepdims=True)
    lg2 = jnp.where(lane == i1, neg, lg)
    m2 = jnp.max(lg2, axis=1, keepdims=True)
    i2 = jnp.min(jnp.where(lg2 == m2, lane, LANES), axis=1, keepdims=True)
    e2 = jnp.exp(m2 - m1)
    den = 1.0 + e2
    gate_ref[...] = jnp.where(lane == i1, 1.0 / den, 0.0) + jnp.where(lane == i2, e2 / den, 0.0)
    memb = jnp.where((lane == i1) | (lane == i2), 1.0, 0.0)
    memb_ref[...] = memb
    cnt_ref[...] = jnp.broadcast_to(jnp.sum(memb, axis=0, keepdims=True), cnt_ref.shape)


def _moe_route(h, g, router):
    n = h.shape[0]
    tm = MOE_TM
    return pl.pallas_call(
        _route_kernel,
        out_shape=[jax.ShapeDtypeStruct((n, D_MODEL), BF16), jax.ShapeDtypeStruct((n, LANES), F32),
                   jax.ShapeDtypeStruct((n, LANES), F32), jax.ShapeDtypeStruct((n // tm, 8, LANES), F32)],
        grid=(n // tm,),
        in_specs=[pl.BlockSpec((tm, D_MODEL), lambda i: (i, 0)),
                  pl.BlockSpec((1, D_MODEL), lambda i: (0, 0)),
                  pl.BlockSpec((D_MODEL, LANES), lambda i: (0, 0))],
        out_specs=[pl.BlockSpec((tm, D_MODEL), lambda i: (i, 0)),
                   pl.BlockSpec((tm, LANES), lambda i: (i, 0)),
                   pl.BlockSpec((tm, LANES), lambda i: (i, 0)),
                   pl.BlockSpec((None, 8, LANES), lambda i: (i, 0, 0))],
        compiler_params=_cparams(("parallel",)),
        name="moe_route",
    )(h, g, router)


def _seg_pad(c):
    return (c + (SEG_ALIGN - 1)) // SEG_ALIGN * SEG_ALIGN


def _gather_kernel(base_ref, cnt_ref, v_ref, memb_ref, tri_ref, xs_in_ref, xs_ref, stg_ref, sem, nout_ref):
    del xs_in_ref
    i = pl.program_id(0)
    n_steps = pl.num_programs(0)
    par = i % 2
    unit = SEG_ALIGN

    def unit_copy(p, src_row, dst_row):
        return pltpu.make_async_copy(stg_ref.at[p, pl.ds(src_row, unit)], xs_ref.at[pl.ds(dst_row, unit)], sem.at[p])

    def wait_units(p, count):
        def body(u, carry):
            unit_copy(p, 0, 0).wait()
            return carry
        lax.fori_loop(0, count, body, 0)

    @pl.when(i >= 2)
    def _():
        wait_units(par, nout_ref[par])

    tm = v_ref.shape[0]
    memb = memb_ref[...]
    rank = jnp.dot(tri_ref[...], memb.astype(BF16), preferred_element_type=F32)
    rank_t = rank.T
    memb_t = memb.T
    v = v_ref[...]
    ridx = lax.broadcasted_iota(jnp.int32, (SEL_ROWS, tm), 0).astype(F32)

    so = jnp.int32(0)
    for e in range(N_EXPERTS):
        c = cnt_ref[i, e]
        rrow = rank_t[e:e + 1, :]
        mrow = memb_t[e:e + 1, :]

        def chunk(qq, carry, rrow=rrow, mrow=mrow, so=so):
            sel = jnp.where((ridx + (qq * SEL_ROWS).astype(F32) == rrow) & (mrow > 0.0), 1.0, 0.0).astype(BF16)
            xc = jnp.dot(sel, v, preferred_element_type=F32).astype(BF16)
            stg_ref[par, pl.ds(pl.multiple_of(so + qq * SEL_ROWS, unit), SEL_ROWS), :] = xc
            return carry

        lax.fori_loop(0, (c + (SEL_ROWS - 1)) // SEL_ROWS, chunk, 0)
        so = so + _seg_pad(c)

    so = jnp.int32(0)
    for e in range(N_EXPERTS):
        cp = _seg_pad(cnt_ref[i, e])
        dst = base_ref[i, e]

        def send(u, carry, so=so, dst=dst):
            unit_copy(par, pl.multiple_of(so + u * unit, unit), pl.multiple_of(dst + u * unit, unit)).start()
            return carry

        lax.fori_loop(0, cp // unit, send, 0)
        so = so + cp
    nout_ref[par] = so // unit

    @pl.when(i == n_steps - 1)
    def _():
        wait_units(par, nout_ref[par])

        @pl.when(i >= 1)
        def _():
            wait_units(1 - par, nout_ref[1 - par])


def _moe_gather(base, cnt, v, memb, tri, xs_rows):
    n = v.shape[0]
    tm = MOE_TM
    stg_rows = 2 * tm + N_EXPERTS * SEG_ALIGN + SEL_ROWS
    xs0 = jnp.zeros((xs_rows, D_MODEL), BF16)
    grid_spec = pltpu.PrefetchScalarGridSpec(
        num_scalar_prefetch=2,
        grid=(n // tm,),
        in_specs=[pl.BlockSpec((tm, D_MODEL), lambda i, b, c: (i, 0)),
                  pl.BlockSpec((tm, LANES), lambda i, b, c: (i, 0)),
                  pl.BlockSpec((tm, tm), lambda i, b, c: (0, 0)),
                  pl.BlockSpec(memory_space=pl.ANY)],
        out_specs=pl.BlockSpec(memory_space=pl.ANY),
        scratch_shapes=[pltpu.VMEM((2, stg_rows, D_MODEL), BF16), pltpu.SemaphoreType.DMA((2,)),
                        pltpu.SMEM((2,), jnp.int32)],
    )
    return pl.pallas_call(
        _gather_kernel,
        out_shape=jax.ShapeDtypeStruct((xs_rows, D_MODEL), BF16),
        grid_spec=grid_spec,
        input_output_aliases={5: 0},
        compiler_params=_cparams(("arbitrary",)),
        name="moe_gather",
    )(base, cnt, v, memb, tri, xs0)


def _expert_kernel(bexp_ref, bval_ref, xs_ref, wg_ref, wu_ref, wd_ref, ys_ref, acc_ref):
    del bexp_ref
    k = pl.program_id(0)
    f = pl.program_id(1)

    @pl.when(f == 0)
    def _():
        acc_ref[...] = jnp.zeros_like(acc_ref)

    @pl.when(bval_ref[k] > 0)
    def _():
        acc_ref[...] += _swiglu_partial(xs_ref[...], wg_ref, wu_ref, wd_ref)

    @pl.when(f == pl.num_programs(1) - 1)
    def _():
        ys_ref[...] = acc_ref[...]


def _moe_experts(bexp, bval, xs, wg, wu, wd):
    n_blk = xs.shape[0] // MOE_BLK
    n_f = D_FF // MOE_TF

    def f_idx(k, f, bval):
        return jnp.where(bval[k] > 0, f, n_f - 1)

    grid_spec = pltpu.PrefetchScalarGridSpec(
        num_scalar_prefetch=2,
        grid=(n_blk + 1, n_f),
        in_specs=[pl.BlockSpec((MOE_BLK, D_MODEL), lambda k, f, be, bv: (jnp.minimum(k, n_blk - 1), 0)),
                  pl.BlockSpec((None, D_MODEL, MOE_TF), lambda k, f, be, bv: (be[k], 0, f_idx(k, f, bv))),
                  pl.BlockSpec((None, D_MODEL, MOE_TF), lambda k, f, be, bv: (be[k], 0, f_idx(k, f, bv))),
                  pl.BlockSpec((None, MOE_TF, D_MODEL), lambda k, f, be, bv: (be[k], f_idx(k, f, bv), 0))],
        out_specs=pl.BlockSpec((MOE_BLK, D_MODEL), lambda k, f, be, bv: (k, 0)),
        scratch_shapes=[pltpu.VMEM((MOE_BLK, D_MODEL), F32)],
    )
    return pl.pallas_call(
        _expert_kernel,
        out_shape=jax.ShapeDtypeStruct(((n_blk + 1) * MOE_BLK, D_MODEL), F32),
        grid_spec=grid_spec,
        compiler_params=_cparams(("arbitrary", "arbitrary")),
        name="moe_experts",
    )(bexp, bval, xs, wg, wu, wd)


def _combine_kernel(base_ref, cnt_ref, h_ref, gate_ref, memb_ref, tri_ref, ys_ref, out_ref, ybuf_ref, acc_ref, sem):
    i = pl.program_id(0)
    n_steps = pl.num_programs(0)
    par = i % 2
    tm = h_ref.shape[0]

    def chunk_copy(p, src_row, slot_row):
        return pltpu.make_async_copy(ys_ref.at[pl.ds(src_row, SEL_ROWS)], ybuf_ref.at[p, pl.ds(slot_row, SEL_ROWS)],
                                     sem.at[p])

    def n_chunks(c):
        return (c + (SEL_ROWS - 1)) // SEL_ROWS

    def fetch(step, p):
        slot = jnp.int32(0)
        for e in range(N_EXPERTS):
            nq = n_chunks(cnt_ref[step, e])
            src = base_ref[step, e]

            def body(qq, carry, slot=slot, src=src):
                chunk_copy(p, pl.multiple_of(src + qq * SEL_ROWS, SEG_ALIGN),
                           pl.multiple_of((slot + qq) * SEL_ROWS, SEL_ROWS)).start()
                return carry

            lax.fori_loop(0, nq, body, 0)
            slot = slot + nq
        return slot

    @pl.when(i == 0)
    def _():
        fetch(0, 0)

    @pl.when(i + 1 < n_steps)
    def _():
        fetch(i + 1, 1 - par)

    total = jnp.int32(0)
    for e in range(N_EXPERTS):
        total = total + n_chunks(cnt_ref[i, e])

    def wait_body(u, carry):
        chunk_copy(par, 0, 0).wait()
        return carry

    lax.fori_loop(0, total, wait_body, 0)

    memb = memb_ref[...]
    gates = gate_ref[...]
    rank = jnp.dot(tri_ref[...], memb.astype(BF16), preferred_element_type=F32)
    cidx = lax.broadcasted_iota(jnp.int32, (tm, SEL_ROWS), 1).astype(F32)
    acc_ref[...] = h_ref[...]

    slot = jnp.int32(0)
    for e in range(N_EXPERTS):
        nq = n_chunks(cnt_ref[i, e])
        rcol = rank[:, e:e + 1]
        mcol = memb[:, e:e + 1]
        gcol = gates[:, e:e + 1]

        def chunk(qq, carry, slot=slot, rcol=rcol, mcol=mcol, gcol=gcol):
            y = ybuf_ref[par, pl.ds(pl.multiple_of((slot + qq) * SEL_ROWS, SEL_ROWS), SEL_ROWS), :]
            sel = jnp.where((cidx + (qq * SEL_ROWS).astype(F32) == rcol) & (mcol > 0.0), 1.0, 0.0).astype(BF16)
            yh = y.astype(BF16)
            yl = (y - yh.astype(F32)).astype(BF16)
            got = jnp.dot(sel, yh, preferred_element_type=F32) + jnp.dot(sel, yl, preferred_element_type=F32)
            acc_ref[...] += gcol * got
            return carry

        lax.fori_loop(0, nq, chunk, 0)
        slot = slot + nq
    out_ref[...] = acc_ref[...]


def _moe_combine(base, cnt, h, gates, memb, tri, ys):
    n = h.shape[0]
    tm = MOE_TM
    max_chunks = 2 * tm // SEL_ROWS + N_EXPERTS
    grid_spec = pltpu.PrefetchScalarGridSpec(
        num_scalar_prefetch=2,
        grid=(n // tm,),
        in_specs=[pl.BlockSpec((tm, D_MODEL), lambda i, b, c: (i, 0)),
                  pl.BlockSpec((tm, LANES), lambda i, b, c: (i, 0)),
                  pl.BlockSpec((tm, LANES), lambda i, b, c: (i, 0)),
                  pl.BlockSpec((tm, tm), lambda i, b, c: (0, 0)),
                  pl.BlockSpec(memory_space=pl.ANY)],
        out_specs=pl.BlockSpec((tm, D_MODEL), lambda i, b, c: (i, 0)),
        scratch_shapes=[pltpu.VMEM((2, max_chunks * SEL_ROWS, D_MODEL), F32), pltpu.VMEM((tm, D_MODEL), F32),
                        pltpu.SemaphoreType.DMA((2,))],
    )
    return pl.pallas_call(
        _combine_kernel,
        out_shape=jax.ShapeDtypeStruct((n, D_MODEL), F32),
        grid_spec=grid_spec,
        compiler_params=_cparams(("arbitrary",)),
        name="moe_combine",
    )(base, cnt, h, gates, memb, tri, ys)


def _moe(h, g, router, wg, wu, wd):
    n = h.shape[0]
    n_tiles = n // MOE_TM
    xs_rows = _moe_rows(n)
    n_blk = xs_rows // MOE_BLK
    v, gates, memb, cnt_f = _moe_route(h, g, router)
    cnt = cnt_f[:, 0, :N_EXPERTS].astype(jnp.int32)
    cp = _seg_pad(cnt)
    exp_rows = -(-jnp.sum(cp, axis=0) // MOE_BLK) * MOE_BLK
    exp_end = jnp.cumsum(exp_rows)
    exp_start = exp_end - exp_rows
    base = (exp_start[None, :] + jnp.cumsum(cp, axis=0) - cp).astype(jnp.int32)
    blk_row = jnp.arange(n_blk + 1, dtype=jnp.int32) * MOE_BLK
    bval = (blk_row < exp_end[-1]).astype(jnp.int32)
    bexp = jnp.minimum(jnp.sum((blk_row[:, None] >= exp_end[None, :]).astype(jnp.int32), axis=1), N_EXPERTS - 1)
    last_valid = jnp.maximum(jnp.sum(bval) - 1, 0)
    bexp = jnp.where(bval > 0, bexp, bexp[last_valid]).astype(jnp.int32)
    tri = jnp.asarray(np.tril(np.ones((MOE_TM, MOE_TM), np.float32), -1), BF16)
    xs = _moe_gather(base, cnt, v, memb, tri, xs_rows)
    ys = _moe_experts(bexp, bval, xs, wg, wu, wd)
    return _moe_combine(base, cnt, h, gates, memb, tri, ys)


def _ple_kernel(final, split, h_ref, pa_ref, pb_ref, g_ref, wup_ref, wgate_ref, gf_ref, *out_refs):
    h = h_ref[...]
    p = _two_src_load(pa_ref, pb_ref, split)
    up = jnp.dot(p.astype(BF16), wup_ref[...], preferred_element_type=F32)
    gt = jnp.dot(_rms(h, g_ref[...]).astype(BF16), wgate_ref[...], preferred_element_type=F32)
    hn = h + up * jax.nn.sigmoid(gt)
    if not final:
        out_refs[0][...] = hn
        return
    y = _rms(hn, gf_ref[...])
    ya_ref, yb_ref = out_refs
    i = pl.program_id(0)

    @pl.when(i < split)
    def _():
        ya_ref[...] = y

    @pl.when(i >= split)
    def _():
        yb_ref[...] = y


def _ple(h, pa, pb, g, wup, wgate, gf, final, tm):
    n = h.shape[0]
    split = pa.shape[0] // tm
    if final:
        out_shape = [jax.ShapeDtypeStruct((pa.shape[0], D_MODEL), F32), jax.ShapeDtypeStruct((pb.shape[0], D_MODEL), F32)]
        out_specs = _two_src_specs(tm, D_MODEL, split)
        sem = ("arbitrary",)
    else:
        out_shape = jax.ShapeDtypeStruct((n, D_MODEL), F32)
        out_specs = pl.BlockSpec((tm, D_MODEL), lambda i: (i, 0))
        sem = ("parallel",)
    return pl.pallas_call(
        functools.partial(_ple_kernel, final, split),
        out_shape=out_shape,
        grid=(n // tm,),
        in_specs=[pl.BlockSpec((tm, D_MODEL), lambda i: (i, 0))] + _two_src_specs(tm, PLE_DIM, split)
        + [pl.BlockSpec((1, D_MODEL), lambda i: (0, 0)),
           pl.BlockSpec((PLE_DIM, D_MODEL), lambda i: (0, 0)),
           pl.BlockSpec((D_MODEL, D_MODEL), lambda i: (0, 0)),
           pl.BlockSpec((1, D_MODEL), lambda i: (0, 0))],
        out_specs=out_specs,
        compiler_params=_cparams(sem),
        name="ple",
    )(h, pa, pb, g, wup, wgate, gf)


def _reorder_w_in(w):
    sizes = (GLA_KW, GLA_KW, GLA_VW, GLA_RANK, GLA_VW, RET_KW, RET_KW, RET_VW, RET_VW,
             GDN_CONV_CH, GDN_H, GDN_H, GDN_VW)
    pts = np.cumsum(np.array(sizes))[:-1].tolist()
    gq, gk, gv, glr, gr, rq, rk, rv, rr, dqkv, da, db, dz = jnp.split(w, pts, axis=1)
    pad = jnp.zeros((w.shape[0], LANES - GLA_RANK - 2 * GDN_H), w.dtype)
    return jnp.concatenate([gq, gk, gv, gr, rq, rk, rv, rr, dqkv, dz, glr, da, db, pad], axis=1)


def _lane_row(vals, offset):
    return jnp.zeros((1, LANES), F32).at[0, offset:offset + vals.shape[0]].set(vals.astype(F32))


def _to_blockdiag(s):
    B, H, dk, dv = s.shape
    eye = jnp.eye(H, dtype=s.dtype)
    return jnp.einsum("bhde,hg->bhdge", s, eye).reshape(B, H * dk, H * dv)


def _from_blockdiag(s, H):
    B, R, C = s.shape
    dk, dv = R // H, C // H
    s5 = s.reshape(B, H, dk, H, dv)
    return jnp.stack([s5[:, h, :, h, :] for h in range(H)], axis=1)


def kernel(x_prompt, x_sample, state_gla, state_ret, state_gdn, state_gdn_conv, p_prompt, p_sample, norm_mix, w_in, gla_w_gate_up, gla_b_gate, gla_norm, ret_norm, gdn_conv, gdn_a_log, gdn_dt_bias, gdn_norm, w_out, norm_ffn, ffn_w_gate, ffn_w_up, ffn_w_down, moe_router, moe_w_gate, moe_w_up, moe_w_down, ple_w_up, ple_norm, ple_w_gate, norm_final):
    depth = w_in.shape[0]
    Bp, Tp, _ = x_prompt.shape
    Bs, Ts, _ = x_sample.shape
    n_p, n_s = Bp * Tp, Bs * Ts
    tm = TOKEN_TILE
    xp = x_prompt.reshape(n_p, D_MODEL)
    xs = x_sample.reshape(n_s, D_MODEL)
    pp = p_prompt.reshape(depth, n_p, PLE_DIM)
    ps = p_sample.reshape(depth, n_s, PLE_DIM)

    outs_p = [[], [], [], []]
    outs_s = [[], [], [], []]
    h = None
    for i in range(depth):
        w_in_k = _reorder_w_in(w_in[i]).astype(BF16)
        prm = dict(
            wgu=jnp.zeros((LANES, GLA_KW), F32).at[:GLA_RANK].set(gla_w_gate_up[i]).astype(BF16),
            gbias=gla_b_gate[i].reshape(1, GLA_KW).astype(F32),
            alog=_lane_row(gdn_a_log[i], SM_DA),
            dtb=_lane_row(gdn_dt_bias[i], SM_DA),
            glan=jnp.tile(gla_norm[i].astype(F32), GLA_H).reshape(1, GLA_VW),
            retn=jnp.tile(ret_norm[i].astype(F32), RET_H).reshape(1, RET_VW),
            gdnn=gdn_norm[i].astype(F32).reshape(1, GDN_DV),
            convw=gdn_conv[i].astype(F32),
        )
        if i == 0:
            z, h = _norm_proj2(xp, xs, norm_mix[i].reshape(1, D_MODEL), w_in_k, tm)
        else:
            z = _norm_proj(h, norm_mix[i].reshape(1, D_MODEL), w_in_k, tm)
        o_p, gla_p, ret_p, gdn_p, conv_p = _mixer(
            z, 0, Bp, Tp, 0, True,
            jnp.zeros((Bp, GLA_KW, GLA_VW), F32), jnp.zeros((Bp, RET_KW, RET_VW), F32),
            jnp.zeros((Bp, GDN_H, GDN_DK, GDN_DV), F32), jnp.zeros((Bp, GDN_CONV_W - 1, GDN_CONV_CH), F32), prm)
        o_s, gla_s, ret_s, gdn_s, conv_s = _mixer(
            z, n_p, Bs, Ts, PAST_LEN, False,
            _to_blockdiag(state_gla[i].astype(F32)), _to_blockdiag(state_ret[i].astype(F32)),
            state_gdn[i].astype(F32), state_gdn_conv[i].astype(F32), prm)
        for lst, val in zip(outs_p, (_from_blockdiag(gla_p, GLA_H), _from_blockdiag(ret_p, RET_H), gdn_p, conv_p)):
            lst.append(val)
        for lst, val in zip(outs_s, (_from_blockdiag(gla_s, GLA_H), _from_blockdiag(ret_s, RET_H), gdn_s, conv_s)):
            lst.append(val)
        h = _out_proj(h, o_p, o_s, w_out[i].astype(BF16), tm)
        j = i // 2
        if i % 2 == 0:
            h = _ffn(h, norm_ffn[i].reshape(1, D_MODEL), ffn_w_gate[j].astype(BF16), ffn_w_up[j].astype(BF16),
                     ffn_w_down[j].astype(BF16), tm, FF_TILE)
        else:
            router = jnp.zeros((D_MODEL, LANES), F32).at[:, :N_EXPERTS].set(moe_router[j])
            h = _moe(h, norm_ffn[i].reshape(1, D_MODEL), router, moe_w_gate[j].astype(BF16),
                     moe_w_up[j].astype(BF16), moe_w_down[j].astype(BF16))
        h = _ple(h, pp[i], ps[i], ple_norm[i].reshape(1, D_MODEL), ple_w_up[i].astype(BF16),
                 ple_w_gate[i].astype(BF16), norm_final.reshape(1, D_MODEL), i == depth - 1, tm)

    y_p, y_s = h
    return (y_p.reshape(Bp, Tp, D_MODEL), y_s.reshape(Bs, Ts, D_MODEL),
            jnp.stack(outs_p[0]), jnp.stack(outs_p[1]), jnp.stack(outs_p[2]), jnp.stack(outs_p[3]),
            jnp.stack(outs_s[0]), jnp.stack(outs_s[1]), jnp.stack(outs_s[2]), jnp.stack(outs_s[3]))
```

```python
import functools
import math

import jax
import jax.numpy as jnp
import numpy as np
from jax import lax
from jax.experimental import pallas as pl
from jax.experimental.pallas import tpu as pltpu

F32 = jnp.float32
BF16 = jnp.bfloat16

D_MODEL = 1024
CHUNK = 64
PLE_DIM = 256
EPS = 1e-6
PAST_LEN = 4096

GLA_H, GLA_DK, GLA_DV, GLA_RANK, GLA_TAU = 4, 32, 64, 16, 16.0
RET_H, RET_DK, RET_DV = 4, 32, 64
ROPE_BASE = 10000.0
GDN_H, GDN_DK, GDN_DV, GDN_CONV_W = 4, 128, 128, 4
GLA_KW, GLA_VW = GLA_H * GLA_DK, GLA_H * GLA_DV
RET_KW, RET_VW = RET_H * RET_DK, RET_H * RET_DV
GDN_KW, GDN_VW = GDN_H * GDN_DK, GDN_H * GDN_DV
GDN_CONV_CH = 2 * GDN_KW + GDN_VW
D_FF = 3584
N_EXPERTS = 8

LANES = 128

Z_GQ, Z_GK, Z_GV, Z_GR = 0, 128, 256, 512
Z_RQ, Z_RK, Z_RV, Z_RR = 768, 896, 1024, 1280
Z_CONV = 1536
Z_DZ = Z_CONV + GDN_CONV_CH
Z_SMALL = Z_DZ + GDN_VW
Z_WIDTH = Z_SMALL + LANES
SM_DA, SM_DB = GLA_RANK, GLA_RANK + GDN_H

MIX_ROWS = 256
TOKEN_TILE = 512
VMEM_LIMIT = 56 * 1024 * 1024


def _cparams(sem):
    return pltpu.CompilerParams(dimension_semantics=sem, vmem_limit_bytes=VMEM_LIMIT)


def _rms(x, g):
    return x * lax.rsqrt(jnp.mean(x * x, axis=-1, keepdims=True) + EPS) * g


def _bdot(a, b):
    return jnp.dot(a.astype(BF16), b.astype(BF16), preferred_element_type=F32)


def _bdot_t(a, b):
    return lax.dot_general(a.astype(BF16), b.astype(BF16), (((1,), (1,)), ((), ())),
                           preferred_element_type=F32)


def _bdot_tl(a, b):
    return lax.dot_general(a.astype(BF16), b.astype(BF16), (((0,), (0,)), ((), ())),
                           preferred_element_type=F32)


def _silu(x):
    return x * jax.nn.sigmoid(x)


def _split3(a):
    hi = a.astype(BF16)
    r1 = a - hi.astype(F32)
    mid = r1.astype(BF16)
    lo = (r1 - mid.astype(F32)).astype(BF16)
    return [hi, mid, lo]


def _sum3(x):
    return x[:, 0:LANES] + x[:, LANES:2 * LANES] + x[:, 2 * LANES:3 * LANES]


def _norm_proj_kernel(h_ref, g_ref, w_ref, z_ref):
    u = _rms(h_ref[...], g_ref[...])
    z_ref[...] = jnp.dot(u.astype(BF16), w_ref[...], preferred_element_type=F32)


def _norm_proj(h, g, w, tm):
    n = h.shape[0]
    return pl.pallas_call(
        _norm_proj_kernel,
        out_shape=jax.ShapeDtypeStruct((n, w.shape[1]), F32),
        grid=(n // tm,),
        in_specs=[pl.BlockSpec((tm, D_MODEL), lambda i: (i, 0)),
                  pl.BlockSpec((1, D_MODEL), lambda i: (0, 0)),
                  pl.BlockSpec(w.shape, lambda i: (0, 0))],
        out_specs=pl.BlockSpec((tm, w.shape[1]), lambda i: (i, 0)),
        compiler_params=_cparams(("parallel",)),
        name="norm_proj",
    )(h, g, w)


def _two_src_specs(tm, width, split):
    return [pl.BlockSpec((tm, width), lambda i, *_: (jnp.minimum(i, split - 1), 0)),
            pl.BlockSpec((tm, width), lambda i, *_: (jnp.maximum(i - split, 0), 0))]


def _two_src_load(a_ref, b_ref, split):
    return jnp.where(pl.program_id(0) < split, a_ref[...], b_ref[...])


def _norm_proj2_kernel(split, xa_ref, xb_ref, g_ref, w_ref, z_ref, h_ref):
    x = _two_src_load(xa_ref, xb_ref, split)
    h_ref[...] = x
    z_ref[...] = jnp.dot(_rms(x, g_ref[...]).astype(BF16), w_ref[...], preferred_element_type=F32)


def _norm_proj2(xa, xb, g, w, tm):
    n = xa.shape[0] + xb.shape[0]
    split = xa.shape[0] // tm
    return pl.pallas_call(
        functools.partial(_norm_proj2_kernel, split),
        out_shape=[jax.ShapeDtypeStruct((n, w.shape[1]), F32), jax.ShapeDtypeStruct((n, D_MODEL), F32)],
        grid=(n // tm,),
        in_specs=_two_src_specs(tm, D_MODEL, split) + [pl.BlockSpec((1, D_MODEL), lambda i: (0, 0)),
                                                      pl.BlockSpec(w.shape, lambda i: (0, 0))],
        out_specs=[pl.BlockSpec((tm, w.shape[1]), lambda i: (i, 0)),
                   pl.BlockSpec((tm, D_MODEL), lambda i: (i, 0))],
        compiler_params=_cparams(("parallel",)),
        name="norm_proj_first",
    )(xa, xb, g, w)


def _mixer_kernel(L, n_seg, seq,
                  z_ref, cos_ref, sin_ref, retd_ref, reteb_ref, retebl_ref, retsdec_ref,
                  sgla0_ref, sret0_ref, sgdn0_ref, conv0_ref,
                  wgu_ref, gbias_ref, alog_ref, dtb_ref, glan_ref, retn_ref, gdnn_ref, convw_ref,
                  tri_ref, de_ref, lmask_ref, gn64_ref, blkmask_ref, hmk_ref, hmv_ref,
                  o_ref, sgla_ref, sret_ref, sgdn_ref, convo_ref,
                  xpad_ref):
    R = L * n_seg
    n_lev = int(math.log2(L))
    c = pl.program_id(1)

    @pl.when(c == 0)
    def _():
        sgla_ref[...] = sgla0_ref[...]
        sret_ref[...] = sret0_ref[...]
        sgdn_ref[...] = sgdn0_ref[...]
        convo_ref[...] = conv0_ref[...]

    blkmask = blkmask_ref[...]
    hmk = hmk_ref[...]
    hmv = hmv_ref[...]
    small = z_ref[:, Z_SMALL:Z_SMALL + LANES]

    def slot(j):
        return 0 if seq else j

    def rows(j):
        return slice(j * L, (j + 1) * L)

    la = jax.nn.log_sigmoid(_bdot(small, wgu_ref[...]) + gbias_ref[...]) * (1.0 / GLA_TAU)
    g_all = -jnp.exp(alog_ref[...]) * jax.nn.softplus(small + dtb_ref[...])
    beta_all = jax.nn.sigmoid(small)
    la_parts = _split3(la)
    la3 = jnp.concatenate(la_parts, axis=1)
    g3 = jnp.concatenate(_split3(g_all), axis=1)
    cs = jnp.dot(tri_ref[...], jnp.concatenate([la3, g3], axis=1), preferred_element_type=F32)
    b = _sum3(cs[:, 0:3 * LANES])
    bg = _sum3(cs[:, 3 * LANES:6 * LANES])
    de = jnp.dot(de_ref[...], jnp.concatenate(la_parts[0:2], axis=1), preferred_element_type=F32)
    ede = jnp.exp(de[:, 0:LANES] + de[:, LANES:2 * LANES])

    def group_norm_gate(o, g, gate):
        sq = o * o
        hi = sq.astype(BF16)
        lo = (sq - hi.astype(F32)).astype(BF16)
        gn = gn64_ref[...]
        ms = (jnp.dot(hi, gn, preferred_element_type=F32) + jnp.dot(lo, gn, preferred_element_type=F32))
        return o * lax.rsqrt(ms + EPS) * g * _silu(gate)

    def intra_scores(q_list, kblk_list, masks):
        att = None
        for qm, kb, mk in zip(q_list, kblk_list, masks):
            term = mk * _bdot_t(qm, kb)
            att = term if att is None else att + term
        return att

    q = z_ref[:, Z_GQ:Z_GQ + GLA_KW] * (GLA_DK ** -0.5)
    k = z_ref[:, Z_GK:Z_GK + GLA_KW]
    v = z_ref[:, Z_GV:Z_GV + GLA_VW]
    qe = q * jnp.exp(b)
    b_t = b.T
    s_gla = sgla_ref[0] if seq else None
    o_parts = []
    for j in range(n_seg):
        r = rows(j)
        qj, kj, vj = q[r], k[r], v[r]
        kblk0 = jnp.concatenate([kj] * GLA_H, axis=0) * hmk
        q_list, kb_list, masks = [], [], []
        for lv in range(n_lev):
            eq = ede[lv * R + j * L:lv * R + (j + 1) * L]
            ek = ede[(n_lev + lv) * R + j * L:(n_lev + lv) * R + (j + 1) * L]
            q_list.append(qj * eq)
            kb_list.append(kblk0 * jnp.concatenate([ek] * GLA_H, axis=0))
            masks.append(lmask_ref[lv])
        q_list.append(qj)
        kb_list.append(kblk0)
        masks.append(lmask_ref[n_lev])
        att = intra_scores(q_list, kb_list, masks)
        vblk = jnp.concatenate([vj] * GLA_H, axis=0) * hmv
        last = (j + 1) * L - 1
        bl = b[last:last + 1, :]
        kv = blkmask * _bdot_tl(kj * jnp.exp(bl - b[r]), vj)
        decc = jnp.exp(b_t[:, last:last + 1])
        if not seq:
            s_gla = sgla_ref[j]
        o_parts.append(_bdot(att, vblk) + _bdot(qe[r], s_gla))
        s_gla = s_gla * decc + kv
        if not seq:
            sgla_ref[j] = s_gla
    if seq:
        sgla_ref[0] = s_gla
    o_gla = jnp.concatenate(o_parts, axis=0)
    o_ref[:, 0:GLA_VW] = group_norm_gate(o_gla, glan_ref[...], z_ref[:, Z_GR:Z_GR + GLA_VW]).astype(o_ref.dtype)

    cos = cos_ref[...]
    sin = sin_ref[...]
    lane = lax.broadcasted_iota(jnp.int32, (R, LANES), 1)
    first_half = (lane % RET_DK) < (RET_DK // 2)

    def rot(xx):
        sw = jnp.where(first_half, pltpu.roll(xx, LANES - RET_DK // 2, 1), pltpu.roll(xx, RET_DK // 2, 1))
        return xx * cos + sw * sin

    q = rot(z_ref[:, Z_RQ:Z_RQ + RET_KW]) * (RET_DK ** -0.5)
    k = rot(z_ref[:, Z_RK:Z_RK + RET_KW])
    v = z_ref[:, Z_RV:Z_RV + RET_VW]
    retd = retd_ref[...]
    reteb = reteb_ref[...]
    retebl = retebl_ref[...]
    retsdec = retsdec_ref[...]
    s_ret = sret_ref[0] if seq else None
    o_parts = []
    for j in range(n_seg):
        r = rows(j)
        qj, kj, vj = q[r], k[r], v[r]
        kblk = jnp.concatenate([kj] * RET_H, axis=0) * hmk
        att = _bdot_t(qj, kblk) * retd
        vblk = jnp.concatenate([vj] * RET_H, axis=0) * hmv
        kv = blkmask * _bdot_tl(kj * retebl, vj)
        if not seq:
            s_ret = sret_ref[j]
        o_parts.append(_bdot(att, vblk) + _bdot(qj * reteb, s_ret))
        s_ret = s_ret * retsdec + kv
        if not seq:
            sret_ref[j] = s_ret
    if seq:
        sret_ref[0] = s_ret
    o_ret = jnp.concatenate(o_parts, axis=0)
    o_ref[:, GLA_VW:GLA_VW + RET_VW] = group_norm_gate(
        o_ret, retn_ref[...], z_ref[:, Z_RR:Z_RR + RET_VW]).astype(o_ref.dtype)

    n_cs = 1 if seq else n_seg
    Lc = R // n_cs
    cw = convw_ref[...]
    xin = z_ref[:, Z_CONV:Z_CONV + GDN_CONV_CH]
    c_parts = []
    for s in range(n_cs):
        xs = xin[s * Lc:(s + 1) * Lc]
        xpad_ref[s, 8 - (GDN_CONV_W - 1):8, :] = convo_ref[s]
        xpad_ref[s, 8:8 + Lc, :] = xs
        convo_ref[s] = xs[Lc - (GDN_CONV_W - 1):Lc, :]
        cacc = xs * cw[GDN_CONV_W - 1:GDN_CONV_W, :]
        for jj in range(GDN_CONV_W - 1):
            sh = GDN_CONV_W - 1 - jj
            cacc = cacc + xpad_ref[s, 8 - sh:8 - sh + Lc, :] * cw[jj:jj + 1, :]
        c_parts.append(cacc)
    cact = _silu(c_parts[0] if n_cs == 1 else jnp.concatenate(c_parts, axis=0))

    row = lax.broadcasted_iota(jnp.int32, (R, R), 0)
    col = lax.broadcasted_iota(jnp.int32, (R, R), 1)
    same = lax.shift_right_logical(row, n_lev) == lax.shift_right_logical(col, n_lev)
    tril = same & (row >= col)
    strict = same & (row > col)
    eye_f = (row == col).astype(F32)
    bg_t = bg.T
    beta_t = beta_all.T
    gdnn = gdnn_ref[...]

    def l2n(xx):
        return xx * lax.rsqrt(jnp.sum(xx * xx, axis=-1, keepdims=True) + EPS)

    HS = range(GDN_H)
    qh = [l2n(cact[:, h * GDN_DK:(h + 1) * GDN_DK]) * (GDN_DK ** -0.5) for h in HS]
    kh = [l2n(cact[:, GDN_KW + h * GDN_DK:GDN_KW + (h + 1) * GDN_DK]) for h in HS]
    vh = [cact[:, 2 * GDN_KW + h * GDN_DV:2 * GDN_KW + (h + 1) * GDN_DV] for h in HS]
    bcol = [bg[:, SM_DA + h:SM_DA + h + 1] for h in HS]
    beta_row = [beta_t[SM_DB + h:SM_DB + h + 1, :] for h in HS]
    beta_col = [beta_all[:, SM_DB + h:SM_DB + h + 1] for h in HS]
    dec = [jnp.exp(jnp.where(tril, bcol[h] - bg_t[SM_DA + h:SM_DA + h + 1, :], -jnp.inf)) for h in HS]
    khb = [kh[h].astype(BF16) for h in HS]
    m = [_bdot_t(khb[h], khb[h]) * jnp.where(strict, dec[h], 0.0) * beta_row[h] for h in HS]
    tinv = [eye_f - m[h] for h in HS]
    pw = [m[h].astype(BF16) for h in HS]
    span = 2
    while span < L:
        pw = [jnp.dot(pw[h], pw[h], preferred_element_type=F32).astype(BF16) for h in HS]
        tinv = [tinv[h] + jnp.dot(tinv[h].astype(BF16), pw[h], preferred_element_type=F32) for h in HS]
        span *= 2
    ebc = [jnp.exp(bcol[h]) for h in HS]
    x1 = [_bdot(tinv[h], jnp.concatenate([vh[h], ebc[h] * kh[h]], axis=1)).astype(BF16) for h in HS]
    qk = [_bdot_t(qh[h], khb[h]) * dec[h] * beta_row[h] for h in HS]
    x2 = [_bdot(qk[h], x1[h]) for h in HS]
    o0 = [x2[h][:, 0:GDN_DV] for h in HS]
    qeff = [ebc[h] * qh[h] - x2[h][:, GDN_DV:2 * GDN_DV] for h in HS]
    x3 = [[None] * n_seg for h in HS]
    blg = [[None] * n_seg for h in HS]
    for j in range(n_seg):
        r = rows(j)
        last = (j + 1) * L - 1
        for h in HS:
            blg[h][j] = bcol[h][last:last + 1, :]
            kt = kh[h][r] * (jnp.exp(blg[h][j] - bcol[h][r]) * beta_col[h][r])
            x3[h][j] = _bdot_tl(kt, x1[h][r])
    s_h = [sgdn_ref[0, h] if seq else None for h in HS]
    o_parts = [[] for h in HS]
    for j in range(n_seg):
        r = rows(j)
        for h in HS:
            if not seq:
                s_h[h] = sgdn_ref[j, h]
            y = _bdot(jnp.concatenate([qeff[h][r], x3[h][j][:, GDN_DV:2 * GDN_DV]], axis=0), s_h[h])
            o_parts[h].append(o0[h][r] + y[0:L])
            s_h[h] = jnp.exp(blg[h][j]) * s_h[h] - y[L:L + GDN_DK] + x3[h][j][:, 0:GDN_DV]
            if not seq:
                sgdn_ref[j, h] = s_h[h]
    for h in HS:
        if seq:
            sgdn_ref[0, h] = s_h[h]
        oh = jnp.concatenate(o_parts[h], axis=0)
        oh = _rms(oh, gdnn) * _silu(z_ref[:, Z_DZ + h * GDN_DV:Z_DZ + (h + 1) * GDN_DV])
        lo = GLA_VW + RET_VW + h * GDN_DV
        o_ref[:, lo:lo + GDN_DV] = oh.astype(o_ref.dtype)


def _head_block_mask(rows_per_head, cols_per_head, heads):
    r = np.arange(rows_per_head * heads)[:, None] // rows_per_head
    c = np.arange(cols_per_head * heads)[None, :] // cols_per_head
    return (r == c).astype(np.float32)


def _pack_consts(L, n_seg):
    R = L * n_seg
    t = np.arange(R)
    tl = t % L
    base = t - tl
    jj = np.arange(R)[None, :]
    tri = ((jj // L) == (t[:, None] // L)) & (jj <= t[:, None])
    tt = np.arange(L)[:, None]
    ss = np.arange(L)[None, :]
    d_list, e_list, masks = [], [], []
    m = L // 2
    while m >= 1:
        second = (tl // m) % 2 == 1
        ref = base + (tl // (2 * m)) * 2 * m + m - 1
        d_list.append(second[:, None] & (jj > ref[:, None]) & (jj <= t[:, None]))
        e_list.append((~second)[:, None] & (jj > t[:, None]) & (jj <= ref[:, None]))
        mk = (tt // (2 * m) == ss // (2 * m)) & ((tt // m) % 2 == 1) & ((ss // m) % 2 == 0)
        masks.append(np.tile(mk, (1, GLA_H)))
        m //= 2
    masks.append(np.tile(np.eye(L, dtype=bool), (1, GLA_H)))
    de = np.concatenate(d_list + e_list, axis=0)
    return (jnp.asarray(tri, BF16), jnp.asarray(de, BF16), jnp.asarray(np.stack(masks), F32))


def _mixer(zf, row0, B, T, pos0, seq, sgla0, sret0, sgdn0, conv0, prm):
    blk0 = row0 // MIX_ROWS
    if seq:
        L = CHUNK
        n_seg = MIX_ROWS // L
        grid = (B, T // MIX_ROWS)
        n_state = 1
    else:
        L = T
        n_seg = MIX_ROWS // L
        grid = (B // n_seg, 1)
        n_state = n_seg
    R = MIX_ROWS
    steps_per_b = grid[1]
    half = RET_DK // 2
    inv = ROPE_BASE ** (-jnp.arange(half, dtype=F32) / half)
    ang = (pos0 + jnp.arange(T, dtype=jnp.int32)).astype(F32)[:, None] * inv[None, :]
    cos_h = jnp.concatenate([jnp.cos(ang), jnp.cos(ang)], axis=1)
    sin_h = jnp.concatenate([-jnp.sin(ang), jnp.sin(ang)], axis=1)
    cos_t = jnp.tile(cos_h, (1, RET_H))
    sin_t = jnp.tile(sin_h, (1, RET_H))
    if not seq:
        cos_t = jnp.tile(cos_t, (n_seg, 1))
        sin_t = jnp.tile(sin_t, (n_seg, 1))
    log_gamma = jnp.log(1.0 - jnp.exp2(-5.0 - jnp.arange(RET_H, dtype=F32)))
    tpos = jnp.arange(L, dtype=F32)
    bret = (tpos[:, None] + 1.0) * log_gamma[None, :]
    dmat = tpos[:, None] - tpos[None, :]
    retd = jnp.where(dmat[None] >= 0, jnp.exp(dmat[None] * log_gamma[:, None, None]), 0.0)
    retd = jnp.transpose(retd, (1, 0, 2)).reshape(L, RET_H * L)
    reteb = jnp.repeat(jnp.exp(bret), RET_DK, axis=1)
    retebl = jnp.repeat(jnp.exp(bret[L - 1:L] - bret), RET_DK, axis=1)
    retsdec = jnp.broadcast_to(jnp.repeat(jnp.exp(bret[L - 1]), RET_DK)[:, None], (RET_KW, RET_VW))

    tri, de, lmask = _pack_consts(L, n_seg)
    gn64 = jnp.asarray(_head_block_mask(GLA_DV, GLA_DV, GLA_H) / GLA_DV, BF16)
    blkmask = jnp.asarray(_head_block_mask(GLA_DK, GLA_DV, GLA_H), F32)
    hmk = jnp.asarray(_head_block_mask(L, RET_DK, RET_H), F32)
    hmv = jnp.asarray(_head_block_mask(L, RET_DV, RET_H), F32)

    def full(a):
        nd = a.ndim
        return pl.BlockSpec(a.shape, lambda b, c: (0,) * nd)

    def per_b(a):
        nd = a.ndim
        return pl.BlockSpec((n_state,) + a.shape[1:], lambda b, c: (b,) + (0,) * (nd - 1))

    if seq:
        tab_spec = pl.BlockSpec((R, LANES), lambda b, c: (c, 0))
    else:
        tab_spec = pl.BlockSpec((R, LANES), lambda b, c: (0, 0))
    consts = [retd, reteb, retebl, retsdec]
    states = [sgla0, sret0, sgdn0, conv0]
    params = [prm["wgu"], prm["gbias"], prm["alog"], prm["dtb"], prm["glan"], prm["retn"], prm["gdnn"],
              prm["convw"], tri, de, lmask, gn64, blkmask, hmk, hmv]
    in_specs = ([pl.BlockSpec((R, Z_WIDTH), lambda b, c: (blk0 + b * steps_per_b + c, 0)), tab_spec, tab_spec]
                + [full(a) for a in consts] + [per_b(a) for a in states] + [full(a) for a in params])
    out_shape = [jax.ShapeDtypeStruct((B * T, D_MODEL), BF16)] + [
        jax.ShapeDtypeStruct(a.shape, F32) for a in states]
    out_specs = [pl.BlockSpec((R, D_MODEL), lambda b, c: (b * steps_per_b + c, 0))] + [per_b(a) for a in states]
    n_cs = 1 if seq else n_seg
    return pl.pallas_call(
        functools.partial(_mixer_kernel, L, n_seg, seq),
        out_shape=out_shape,
        grid=grid,
        in_specs=in_specs,
        out_specs=out_specs,
        scratch_shapes=[pltpu.VMEM((n_cs, R // n_cs + 8, GDN_CONV_CH), F32)],
        compiler_params=_cparams(("parallel", "arbitrary")),
        name="mixer",
    )(zf, cos_t, sin_t, *consts, *states, *params)


def _out_proj_kernel(split, h_ref, oa_ref, ob_ref, w_ref, out_ref):
    o = _two_src_load(oa_ref, ob_ref, split)
    out_ref[...] = h_ref[...] + jnp.dot(o, w_ref[...], preferred_element_type=F32)


def _out_proj(h, oa, ob, w, tm):
    n = h.shape[0]
    split = oa.shape[0] // tm
    return pl.pallas_call(
        functools.partial(_out_proj_kernel, split),
        out_shape=jax.ShapeDtypeStruct((n, D_MODEL), F32),
        grid=(n // tm,),
        in_specs=[pl.BlockSpec((tm, D_MODEL), lambda i: (i, 0))] + _two_src_specs(tm, D_MODEL, split)
        + [pl.BlockSpec((D_MODEL, D_MODEL), lambda i: (0, 0))],
        out_specs=pl.BlockSpec((tm, D_MODEL), lambda i: (i, 0)),
        compiler_params=_cparams(("parallel",)),
        name="out_proj",
    )(h, oa, ob, w)


FF_TILE = 1792
FF_SUB = 896


def _swiglu_partial(v, wg_ref, wu_ref, wd_ref):
    parts = []
    for c0 in range(0, wg_ref.shape[1], FF_SUB):
        a = jnp.dot(v, wg_ref[:, c0:c0 + FF_SUB], preferred_element_type=F32)
        u = jnp.dot(v, wu_ref[:, c0:c0 + FF_SUB], preferred_element_type=F32)
        parts.append((_silu(a) * u).astype(BF16))
    return jnp.dot(jnp.concatenate(parts, axis=1), wd_ref[...], preferred_element_type=F32)


def _ffn_kernel(h_ref, g_ref, wg_ref, wu_ref, wd_ref, out_ref, v_ref, acc_ref):
    f = pl.program_id(1)

    @pl.when(f == 0)
    def _():
        v_ref[...] = _rms(h_ref[...], g_ref[...]).astype(BF16)
        acc_ref[...] = h_ref[...]

    acc_ref[...] += _swiglu_partial(v_ref[...], wg_ref, wu_ref, wd_ref)

    @pl.when(f == pl.num_programs(1) - 1)
    def _():
        out_ref[...] = acc_ref[...]


def _ffn(h, g, wg, wu, wd, tm, tf):
    n = h.shape[0]
    return pl.pallas_call(
        _ffn_kernel,
        out_shape=jax.ShapeDtypeStruct((n, D_MODEL), F32),
        grid=(n // tm, D_FF // tf),
        in_specs=[pl.BlockSpec((tm, D_MODEL), lambda i, f: (i, 0)),
                  pl.BlockSpec((1, D_MODEL), lambda i, f: (0, 0)),
                  pl.BlockSpec((D_MODEL, tf), lambda i, f: (0, f)),
                  pl.BlockSpec((D_MODEL, tf), lambda i, f: (0, f)),
                  pl.BlockSpec((tf, D_MODEL), lambda i, f: (f, 0))],
        out_specs=pl.BlockSpec((tm, D_MODEL), lambda i, f: (i, 0)),
        scratch_shapes=[pltpu.VMEM((tm, D_MODEL), BF16), pltpu.VMEM((tm, D_MODEL), F32)],
        compiler_params=_cparams(("parallel", "arbitrary")),
        name="ffn",
    )(h, g, wg, wu, wd)


MOE_TM = 512
MOE_BLK = 512
MOE_TF = FF_TILE
SEG_ALIGN = 16
SEL_ROWS = 128


def _moe_rows(n):
    n_tiles = n // MOE_TM
    bound = 2 * n + n_tiles * N_EXPERTS * (SEG_ALIGN - 1) + N_EXPERTS * (MOE_BLK - 1)
    return -(-bound // MOE_BLK) * MOE_BLK


def _route_kernel(h_ref, g_ref, r_ref, v_ref, gate_ref, memb_ref, cnt_ref):
    vf = _rms(h_ref[...], g_ref[...])
    v_hi = vf.astype(BF16)
    v_ref[...] = v_hi
    v_mid = (vf - v_hi.astype(F32)).astype(BF16)
    logits = jnp.dot(jnp.concatenate([v_hi, v_hi, v_mid], axis=1), r_ref[...],
                     preferred_element_type=F32)
    lane = lax.broadcasted_iota(jnp.int32, logits.shape, 1)
    neg = jnp.float32(-jnp.inf)
    lg = jnp.where(lane < N_EXPERTS, logits, neg)
    m1 = jnp.max(lg, axis=1, keepdims=True)
    i1 = jnp.min(jnp.where(lg == m1, lane, LANES), axis=1, keepdims=True)
    lg2 = jnp.where(lane == i1, neg, lg)
    m2 = jnp.max(lg2, axis=1, keepdims=True)
    i2 = jnp.min(jnp.where(lg2 == m2, lane, LANES), axis=1, keepdims=True)
    e2 = jnp.exp(m2 - m1)
    den = 1.0 + e2
    gate_ref[...] = jnp.where(lane == i1, 1.0 / den, 0.0) + jnp.where(lane == i2, e2 / den, 0.0)
    memb = jnp.where((lane == i1) | (lane == i2), 1.0, 0.0)
    memb_ref[...] = memb
    cnt_ref[...] = jnp.broadcast_to(jnp.sum(memb, axis=0, keepdims=True), cnt_ref.shape)


def _moe_route(h, g, router):
    n = h.shape[0]
    tm = MOE_TM
    r_hi = router.astype(BF16)
    r_mid = (router - r_hi.astype(F32)).astype(BF16)
    r3 = jnp.concatenate([r_hi, r_mid, r_hi], axis=0)
    return pl.pallas_call(
        _route_kernel,
        out_shape=[jax.ShapeDtypeStruct((n, D_MODEL), BF16), jax.ShapeDtypeStruct((n, LANES), F32),
                   jax.ShapeDtypeStruct((n, LANES), F32), jax.ShapeDtypeStruct((n // tm, 8, LANES), F32)],
        grid=(n // tm,),
        in_specs=[pl.BlockSpec((tm, D_MODEL), lambda i: (i, 0)),
                  pl.BlockSpec((1, D_MODEL), lambda i: (0, 0)),
                  pl.BlockSpec((3 * D_MODEL, LANES), lambda i: (0, 0))],
        out_specs=[pl.BlockSpec((tm, D_MODEL), lambda i: (i, 0)),
                   pl.BlockSpec((tm, LANES), lambda i: (i, 0)),
                   pl.BlockSpec((tm, LANES), lambda i: (i, 0)),
                   pl.BlockSpec((None, 8, LANES), lambda i: (i, 0, 0))],
        compiler_params=_cparams(("parallel",)),
        name="moe_route",
    )(h, g, r3)


def _seg_pad(c):
    return (c + (SEG_ALIGN - 1)) // SEG_ALIGN * SEG_ALIGN


def _gather_kernel(base_ref, cnt_ref, v_ref, memb_ref, tri_ref, xs_in_ref, xs_ref, stg_ref, sem, nout_ref):
    del xs_in_ref
    i = pl.program_id(0)
    n_steps = pl.num_programs(0)
    par = i % 2
    unit = SEG_ALIGN

    def unit_copy(p, src_row, dst_row):
        return pltpu.make_async_copy(stg_ref.at[p, pl.ds(src_row, unit)], xs_ref.at[pl.ds(dst_row, unit)], sem.at[p])

    def wait_units(p, count):
        def body(u, carry):
            unit_copy(p, 0, 0).wait()
            return carry
        lax.fori_loop(0, count, body, 0)

    @pl.when(i >= 2)
    def _():
        wait_units(par, nout_ref[par])

    tm = v_ref.shape[0]
    memb = memb_ref[...]
    rank = jnp.dot(tri_ref[...], memb.astype(BF16), preferred_element_type=F32)
    rank_t = rank.T[0:N_EXPERTS, :]
    is_m = memb.T[0:N_EXPERTS, :] > 0.0
    sub = lax.broadcasted_iota(jnp.int32, (N_EXPERTS, tm), 0)
    pos = rank_t
    so = jnp.int32(0)
    for e in range(N_EXPERTS):
        pos = pos + jnp.where(sub == e, so.astype(F32), 0.0)
        so = so + _seg_pad(cnt_ref[i, e])
    stg_rows = stg_ref.shape[1]
    pos_a = jnp.min(jnp.where(is_m, pos, float(stg_rows)), axis=0, keepdims=True)
    pos_b = jnp.max(jnp.where(is_m, pos, -1.0), axis=0, keepdims=True)
    ridx = lax.broadcasted_iota(jnp.int32, (stg_rows, tm), 0).astype(F32)
    sel = jnp.where((ridx == pos_a) | (ridx == pos_b), 1.0, 0.0).astype(BF16)
    stg_ref[par] = jnp.dot(sel, v_ref[...], preferred_element_type=F32).astype(BF16)

    so = jnp.int32(0)
    for e in range(N_EXPERTS):
        cp = _seg_pad(cnt_ref[i, e])
        dst = base_ref[i, e]

        def send(u, carry, so=so, dst=dst):
            unit_copy(par, pl.multiple_of(so + u * unit, unit), pl.multiple_of(dst + u * unit, unit)).start()
            return carry

        lax.fori_loop(0, cp // unit, send, 0)
        so = so + cp
    nout_ref[par] = so // unit

    @pl.when(i == n_steps - 1)
    def _():
        wait_units(par, nout_ref[par])

        @pl.when(i >= 1)
        def _():
            wait_units(1 - par, nout_ref[1 - par])


def _moe_gather(base, cnt, v, memb, tri, xs_rows):
    n = v.shape[0]
    tm = MOE_TM
    stg_rows = 2 * tm + N_EXPERTS * SEG_ALIGN
    xs0 = jnp.zeros((xs_rows, D_MODEL), BF16)
    grid_spec = pltpu.PrefetchScalarGridSpec(
        num_scalar_prefetch=2,
        grid=(n // tm,),
        in_specs=[pl.BlockSpec((tm, D_MODEL), lambda i, b, c: (i, 0)),
                  pl.BlockSpec((tm, LANES), lambda i, b, c: (i, 0)),
                  pl.BlockSpec((tm, tm), lambda i, b, c: (0, 0)),
                  pl.BlockSpec(memory_space=pl.ANY)],
        out_specs=pl.BlockSpec(memory_space=pl.ANY),
        scratch_shapes=[pltpu.VMEM((2, stg_rows, D_MODEL), BF16), pltpu.SemaphoreType.DMA((2,)),
                        pltpu.SMEM((2,), jnp.int32)],
    )
    return pl.pallas_call(
        _gather_kernel,
        out_shape=jax.ShapeDtypeStruct((xs_rows, D_MODEL), BF16),
        grid_spec=grid_spec,
        input_output_aliases={5: 0},
        compiler_params=_cparams(("arbitrary",)),
        name="moe_gather",
    )(base, cnt, v, memb, tri, xs0)


def _expert_kernel(bexp_ref, bval_ref, xs_ref, wg_ref, wu_ref, wd_ref, ys_ref, acc_ref):
    del bexp_ref
    k = pl.program_id(0)
    f = pl.program_id(1)

    @pl.when(f == 0)
    def _():
        acc_ref[...] = jnp.zeros_like(acc_ref)

    @pl.when(bval_ref[k] > 0)
    def _():
        acc_ref[...] += _swiglu_partial(xs_ref[...], wg_ref, wu_ref, wd_ref)

    @pl.when(f == pl.num_programs(1) - 1)
    def _():
        ys_ref[...] = acc_ref[...]


def _moe_experts(bexp, bval, xs, wg, wu, wd):
    n_blk = xs.shape[0] // MOE_BLK
    n_f = D_FF // MOE_TF

    def f_idx(k, f, bval):
        return jnp.where(bval[k] > 0, f, n_f - 1)

    grid_spec = pltpu.PrefetchScalarGridSpec(
        num_scalar_prefetch=2,
        grid=(n_blk + 1, n_f),
        in_specs=[pl.BlockSpec((MOE_BLK, D_MODEL), lambda k, f, be, bv: (jnp.minimum(k, n_blk - 1), 0)),
                  pl.BlockSpec((None, D_MODEL, MOE_TF), lambda k, f, be, bv: (be[k], 0, f_idx(k, f, bv))),
                  pl.BlockSpec((None, D_MODEL, MOE_TF), lambda k, f, be, bv: (be[k], 0, f_idx(k, f, bv))),
                  pl.BlockSpec((None, MOE_TF, D_MODEL), lambda k, f, be, bv: (be[k], f_idx(k, f, bv), 0))],
        out_specs=pl.BlockSpec((MOE_BLK, D_MODEL), lambda k, f, be, bv: (k, 0)),
        scratch_shapes=[pltpu.VMEM((MOE_BLK, D_MODEL), F32)],
    )
    return pl.pallas_call(
        _expert_kernel,
        out_shape=jax.ShapeDtypeStruct(((n_blk + 1) * MOE_BLK, D_MODEL), F32),
        grid_spec=grid_spec,
        compiler_params=_cparams(("arbitrary", "arbitrary")),
        name="moe_experts",
    )(bexp, bval, xs, wg, wu, wd)


def _combine_kernel(base_ref, cnt_ref, h_ref, gate_ref, memb_ref, tri_ref, ys_ref, out_ref, ybuf_ref, acc_ref, sem):
    i = pl.program_id(0)
    n_steps = pl.num_programs(0)
    par = i % 2
    tm = h_ref.shape[0]

    def chunk_copy(p, src_row, slot_row):
        return pltpu.make_async_copy(ys_ref.at[pl.ds(src_row, SEL_ROWS)], ybuf_ref.at[p, pl.ds(slot_row, SEL_ROWS)],
                                     sem.at[p])

    def n_chunks(c):
        return (c + (SEL_ROWS - 1)) // SEL_ROWS

    def fetch(step, p):
        slot = jnp.int32(0)
        for e in range(N_EXPERTS):
            nq = n_chunks(cnt_ref[step, e])
            src = base_ref[step, e]

            def body(qq, carry, slot=slot, src=src):
                chunk_copy(p, pl.multiple_of(src + qq * SEL_ROWS, SEG_ALIGN),
                           pl.multiple_of((slot + qq) * SEL_ROWS, SEL_ROWS)).start()
                return carry

            lax.fori_loop(0, nq, body, 0)
            slot = slot + nq
        return slot

    @pl.when(i == 0)
    def _():
        fetch(0, 0)

    @pl.when(i + 1 < n_steps)
    def _():
        fetch(i + 1, 1 - par)

    total = jnp.int32(0)
    for e in range(N_EXPERTS):
        total = total + n_chunks(cnt_ref[i, e])

    def wait_body(u, carry):
        chunk_copy(par, 0, 0).wait()
        return carry

    lax.fori_loop(0, total, wait_body, 0)

    memb = memb_ref[...]
    gates = gate_ref[...]
    rank = jnp.dot(tri_ref[...], memb.astype(BF16), preferred_element_type=F32)
    cidx = lax.broadcasted_iota(jnp.int32, (tm, SEL_ROWS), 1).astype(F32)
    acc_ref[...] = h_ref[...]

    slot = jnp.int32(0)
    for e in range(N_EXPERTS):
        nq = n_chunks(cnt_ref[i, e])
        rcol = rank[:, e:e + 1]
        mcol = memb[:, e:e + 1]
        gcol = gates[:, e:e + 1]

        def chunk(qq, carry, slot=slot, rcol=rcol, mcol=mcol, gcol=gcol):
            y = ybuf_ref[par, pl.ds(pl.multiple_of((slot + qq) * SEL_ROWS, SEL_ROWS), SEL_ROWS), :]
            sel = jnp.where((cidx + (qq * SEL_ROWS).astype(F32) == rcol) & (mcol > 0.0), 1.0, 0.0).astype(BF16)
            yh = y.astype(BF16)
            yl = (y - yh.astype(F32)).astype(BF16)
            got = jnp.dot(jnp.concatenate([sel, sel], axis=1), jnp.concatenate([yh, yl], axis=0),
                          preferred_element_type=F32)
            acc_ref[...] += gcol * got
            return carry

        lax.fori_loop(0, nq, chunk, 0)
        slot = slot + nq
    out_ref[...] = acc_ref[...]


def _moe_combine(base, cnt, h, gates, memb, tri, ys):
    n = h.shape[0]
    tm = MOE_TM
    max_chunks = 2 * tm // SEL_ROWS + N_EXPERTS
    grid_spec = pltpu.PrefetchScalarGridSpec(
        num_scalar_prefetch=2,
        grid=(n // tm,),
        in_specs=[pl.BlockSpec((tm, D_MODEL), lambda i, b, c: (i, 0)),
                  pl.BlockSpec((tm, LANES), lambda i, b, c: (i, 0)),
                  pl.BlockSpec((tm, LANES), lambda i, b, c: (i, 0)),
                  pl.BlockSpec((tm, tm), lambda i, b, c: (0, 0)),
                  pl.BlockSpec(memory_space=pl.ANY)],
        out_specs=pl.BlockSpec((tm, D_MODEL), lambda i, b, c: (i, 0)),
        scratch_shapes=[pltpu.VMEM((2, max_chunks * SEL_ROWS, D_MODEL), F32), pltpu.VMEM((tm, D_MODEL), F32),
                        pltpu.SemaphoreType.DMA((2,))],
    )
    return pl.pallas_call(
        _combine_kernel,
        out_shape=jax.ShapeDtypeStruct((n, D_MODEL), F32),
        grid_spec=grid_spec,
        compiler_params=_cparams(("arbitrary",)),
        name="moe_combine",
    )(base, cnt, h, gates, memb, tri, ys)


def _moe(h, g, router, wg, wu, wd):
    n = h.shape[0]
    xs_rows = _moe_rows(n)
    n_blk = xs_rows // MOE_BLK
    v, gates, memb, cnt_f = _moe_route(h, g, router)
    cnt = cnt_f[:, 0, :N_EXPERTS].astype(jnp.int32)
    cp = _seg_pad(cnt)
    exp_rows = -(-jnp.sum(cp, axis=0) // MOE_BLK) * MOE_BLK
    exp_end = jnp.cumsum(exp_rows)
    exp_start = exp_end - exp_rows
    base = (exp_start[None, :] + jnp.cumsum(cp, axis=0) - cp).astype(jnp.int32)
    blk_row = jnp.arange(n_blk + 1, dtype=jnp.int32) * MOE_BLK
    bval = (blk_row < exp_end[-1]).astype(jnp.int32)
    bexp = jnp.minimum(jnp.sum((blk_row[:, None] >= exp_end[None, :]).astype(jnp.int32), axis=1), N_EXPERTS - 1)
    last_valid = jnp.maximum(jnp.sum(bval) - 1, 0)
    bexp = jnp.where(bval > 0, bexp, bexp[last_valid]).astype(jnp.int32)
    tri = jnp.asarray(np.tril(np.ones((MOE_TM, MOE_TM), np.float32), -1), BF16)
    xs = _moe_gather(base, cnt, v, memb, tri, xs_rows)
    ys = _moe_experts(bexp, bval, xs, wg, wu, wd)
    return _moe_combine(base, cnt, h, gates, memb, tri, ys)


def _ple_kernel(final, split, h_ref, pa_ref, pb_ref, g_ref, wup_ref, wgate_ref, gf_ref, *out_refs):
    h = h_ref[...]
    p = _two_src_load(pa_ref, pb_ref, split)
    up = jnp.dot(p.astype(BF16), wup_ref[...], preferred_element_type=F32)
    gt = jnp.dot(_rms(h, g_ref[...]).astype(BF16), wgate_ref[...], preferred_element_type=F32)
    hn = h + up * jax.nn.sigmoid(gt)
    if not final:
        out_refs[0][...] = hn
        return
    y = _rms(hn, gf_ref[...])
    ya_ref, yb_ref = out_refs
    i = pl.program_id(0)

    @pl.when(i < split)
    def _():
        ya_ref[...] = y

    @pl.when(i >= split)
    def _():
        yb_ref[...] = y


def _ple(h, pa, pb, g, wup, wgate, gf, final, tm):
    n = h.shape[0]
    split = pa.shape[0] // tm
    if final:
        out_shape = [jax.ShapeDtypeStruct((pa.shape[0], D_MODEL), F32), jax.ShapeDtypeStruct((pb.shape[0], D_MODEL), F32)]
        out_specs = _two_src_specs(tm, D_MODEL, split)
        sem = ("arbitrary",)
    else:
        out_shape = jax.ShapeDtypeStruct((n, D_MODEL), F32)
        out_specs = pl.BlockSpec((tm, D_MODEL), lambda i: (i, 0))
        sem = ("parallel",)
    return pl.pallas_call(
        functools.partial(_ple_kernel, final, split),
        out_shape=out_shape,
        grid=(n // tm,),
        in_specs=[pl.BlockSpec((tm, D_MODEL), lambda i: (i, 0))] + _two_src_specs(tm, PLE_DIM, split)
        + [pl.BlockSpec((1, D_MODEL), lambda i: (0, 0)),
           pl.BlockSpec((PLE_DIM, D_MODEL), lambda i: (0, 0)),
           pl.BlockSpec((D_MODEL, D_MODEL), lambda i: (0, 0)),
           pl.BlockSpec((1, D_MODEL), lambda i: (0, 0))],
        out_specs=out_specs,
        compiler_params=_cparams(sem),
        name="ple",
    )(h, pa, pb, g, wup, wgate, gf)


def _reorder_w_in(w):
    sizes = (GLA_KW, GLA_KW, GLA_VW, GLA_RANK, GLA_VW, RET_KW, RET_KW, RET_VW, RET_VW,
             GDN_CONV_CH, GDN_H, GDN_H, GDN_VW)
    pts = np.cumsum(np.array(sizes))[:-1].tolist()
    gq, gk, gv, glr, gr, rq, rk, rv, rr, dqkv, da, db, dz = jnp.split(w, pts, axis=1)
    pad = jnp.zeros((w.shape[0], LANES - GLA_RANK - 2 * GDN_H), w.dtype)
    return jnp.concatenate([gq, gk, gv, gr, rq, rk, rv, rr, dqkv, dz, glr, da, db, pad], axis=1)


def _lane_row(vals, offset):
    return jnp.zeros((1, LANES), F32).at[0, offset:offset + vals.shape[0]].set(vals.astype(F32))


def _to_blockdiag(s):
    B, H, dk, dv = s.shape
    eye = jnp.eye(H, dtype=s.dtype)
    return jnp.einsum("bhde,hg->bhdge", s, eye).reshape(B, H * dk, H * dv)


def _from_blockdiag(s, H):
    B, R, C = s.shape
    dk, dv = R // H, C // H
    s5 = s.reshape(B, H, dk, H, dv)
    return jnp.stack([s5[:, h, :, h, :] for h in range(H)], axis=1)


def kernel(x_prompt, x_sample, state_gla, state_ret, state_gdn, state_gdn_conv, p_prompt, p_sample, norm_mix, w_in, gla_w_gate_up, gla_b_gate, gla_norm, ret_norm, gdn_conv, gdn_a_log, gdn_dt_bias, gdn_norm, w_out, norm_ffn, ffn_w_gate, ffn_w_up, ffn_w_down, moe_router, moe_w_gate, moe_w_up, moe_w_down, ple_w_up, ple_norm, ple_w_gate, norm_final):
    depth = w_in.shape[0]
    Bp, Tp, _ = x_prompt.shape
    Bs, Ts, _ = x_sample.shape
    n_p, n_s = Bp * Tp, Bs * Ts
    tm = TOKEN_TILE
    xp = x_prompt.reshape(n_p, D_MODEL)
    xs = x_sample.reshape(n_s, D_MODEL)
    pp = p_prompt.reshape(depth, n_p, PLE_DIM)
    ps = p_sample.reshape(depth, n_s, PLE_DIM)

    outs_p = [[], [], [], []]
    outs_s = [[], [], [], []]
    h = None
    for i in range(depth):
        w_in_k = _reorder_w_in(w_in[i]).astype(BF16)
        prm = dict(
            wgu=jnp.zeros((LANES, GLA_KW), F32).at[:GLA_RANK].set(gla_w_gate_up[i]).astype(BF16),
            gbias=gla_b_gate[i].reshape(1, GLA_KW).astype(F32),
            alog=_lane_row(gdn_a_log[i], SM_DA),
            dtb=_lane_row(gdn_dt_bias[i], SM_DA),
            glan=jnp.tile(gla_norm[i].astype(F32), GLA_H).reshape(1, GLA_VW),
            retn=jnp.tile(ret_norm[i].astype(F32), RET_H).reshape(1, RET_VW),
            gdnn=gdn_norm[i].astype(F32).reshape(1, GDN_DV),
            convw=gdn_conv[i].astype(F32),
        )
        if i == 0:
            z, h = _norm_proj2(xp, xs, norm_mix[i].reshape(1, D_MODEL), w_in_k, tm)
        else:
            z = _norm_proj(h, norm_mix[i].reshape(1, D_MODEL), w_in_k, tm)
        o_p, gla_p, ret_p, gdn_p, conv_p = _mixer(
            z, 0, Bp, Tp, 0, True,
            jnp.zeros((Bp, GLA_KW, GLA_VW), F32), jnp.zeros((Bp, RET_KW, RET_VW), F32),
            jnp.zeros((Bp, GDN_H, GDN_DK, GDN_DV), F32), jnp.zeros((Bp, GDN_CONV_W - 1, GDN_CONV_CH), F32), prm)
        o_s, gla_s, ret_s, gdn_s, conv_s = _mixer(
            z, n_p, Bs, Ts, PAST_LEN, False,
            _to_blockdiag(state_gla[i].astype(F32)), _to_blockdiag(state_ret[i].astype(F32)),
            state_gdn[i].astype(F32), state_gdn_conv[i].astype(F32), prm)
        for lst, val in zip(outs_p, (_from_blockdiag(gla_p, GLA_H), _from_blockdiag(ret_p, RET_H), gdn_p, conv_p)):
            lst.append(val)
        for lst, val in zip(outs_s, (_from_blockdiag(gla_s, GLA_H), _from_blockdiag(ret_s, RET_H), gdn_s, conv_s)):
            lst.append(val)
        h = _out_proj(h, o_p, o_s, w_out[i].astype(BF16), tm)
        j = i // 2
        if i % 2 == 0:
            h = _ffn(h, norm_ffn[i].reshape(1, D_MODEL), ffn_w_gate[j].astype(BF16), ffn_w_up[j].astype(BF16),
                     ffn_w_down[j].astype(BF16), tm, FF_TILE)
        else:
            router = jnp.zeros((D_MODEL, LANES), F32).at[:, :N_EXPERTS].set(moe_router[j].astype(F32))
            h = _moe(h, norm_ffn[i].reshape(1, D_MODEL), router, moe_w_gate[j].astype(BF16),
                     moe_w_up[j].astype(BF16), moe_w_down[j].astype(BF16))
        h = _ple(h, pp[i], ps[i], ple_norm[i].reshape(1, D_MODEL), ple_w_up[i].astype(BF16),
                 ple_w_gate[i].astype(BF16), norm_final.reshape(1, D_MODEL), i == depth - 1, tm)

    y_p, y_s = h
    return (y_p.reshape(Bp, Tp, D_MODEL), y_s.reshape(Bs, Ts, D_MODEL),
            jnp.stack(outs_p[0]), jnp.stack(outs_p[1]), jnp.stack(outs_p[2]), jnp.stack(outs_p[3]),
            jnp.stack(outs_s[0]), jnp.stack(outs_s[1]), jnp.stack(outs_s[2]), jnp.stack(outs_s[3]))
```

```python
import functools
import math

import jax
import jax.numpy as jnp
import numpy as np
from jax import lax
from jax.experimental import pallas as pl
from jax.experimental.pallas import tpu as pltpu

F32 = jnp.float32
BF16 = jnp.bfloat16

D_MODEL = 1024
CHUNK = 64
PLE_DIM = 256
EPS = 1e-6
PAST_LEN = 4096

GLA_H, GLA_DK, GLA_DV, GLA_RANK, GLA_TAU = 4, 32, 64, 16, 16.0
RET_H, RET_DK, RET_DV = 4, 32, 64
ROPE_BASE = 10000.0
GDN_H, GDN_DK, GDN_DV, GDN_CONV_W = 4, 128, 128, 4
GLA_KW, GLA_VW = GLA_H * GLA_DK, GLA_H * GLA_DV
RET_KW, RET_VW = RET_H * RET_DK, RET_H * RET_DV
GDN_KW, GDN_VW = GDN_H * GDN_DK, GDN_H * GDN_DV
GDN_CONV_CH = 2 * GDN_KW + GDN_VW
D_FF = 3584
N_EXPERTS = 8

LANES = 128

Z_GQ, Z_GK, Z_GV, Z_GR = 0, 128, 256, 512
Z_RQ, Z_RK, Z_RV, Z_RR = 768, 896, 1024, 1280
Z_CONV = 1536
Z_DZ = Z_CONV + GDN_CONV_CH
Z_SMALL = Z_DZ + GDN_VW
Z_WIDTH = Z_SMALL + LANES
SM_DA, SM_DB = GLA_RANK, GLA_RANK + GDN_H

MIX_ROWS = 256
TOKEN_TILE = 512
VMEM_LIMIT = 56 * 1024 * 1024


def _cparams(sem):
    return pltpu.CompilerParams(dimension_semantics=sem, vmem_limit_bytes=VMEM_LIMIT)


def _rms(x, g):
    return x * lax.rsqrt(jnp.mean(x * x, axis=-1, keepdims=True) + EPS) * g


def _bdot(a, b):
    return jnp.dot(a.astype(BF16), b.astype(BF16), preferred_element_type=F32)


def _bdot_t(a, b):
    return lax.dot_general(a.astype(BF16), b.astype(BF16), (((1,), (1,)), ((), ())),
                           preferred_element_type=F32)


def _bdot_tl(a, b):
    return lax.dot_general(a.astype(BF16), b.astype(BF16), (((0,), (0,)), ((), ())),
                           preferred_element_type=F32)


def _silu(x):
    return x * jax.nn.sigmoid(x)


def _split3(a):
    hi = a.astype(BF16)
    r1 = a - hi.astype(F32)
    mid = r1.astype(BF16)
    lo = (r1 - mid.astype(F32)).astype(BF16)
    return [hi, mid, lo]


def _sum3(x):
    return x[:, 0:LANES] + x[:, LANES:2 * LANES] + x[:, 2 * LANES:3 * LANES]


def _two_src_specs(tm, width, split, layer=None):
    if layer is None:
        return [pl.BlockSpec((tm, width), lambda i, *_: (jnp.minimum(i, split - 1), 0)),
                pl.BlockSpec((tm, width), lambda i, *_: (jnp.maximum(i - split, 0), 0))]
    return [pl.BlockSpec((None, tm, width), lambda i, *_: (layer, jnp.minimum(i, split - 1), 0)),
            pl.BlockSpec((None, tm, width), lambda i, *_: (layer, jnp.maximum(i - split, 0), 0))]


def _two_src_load(a_ref, b_ref, split):
    return jnp.where(pl.program_id(0) < split, a_ref[...], b_ref[...])


def _norm_proj2_kernel(split, xa_ref, xb_ref, g_ref, w_ref, z_ref, h_ref):
    x = _two_src_load(xa_ref, xb_ref, split)
    h_ref[...] = x
    z_ref[...] = jnp.dot(_rms(x, g_ref[...]).astype(BF16), w_ref[...], preferred_element_type=F32)


def _norm_proj2(xa, xb, g, w, tm):
    n = xa.shape[0] + xb.shape[0]
    split = xa.shape[0] // tm
    return pl.pallas_call(
        functools.partial(_norm_proj2_kernel, split),
        out_shape=[jax.ShapeDtypeStruct((n, w.shape[1]), F32), jax.ShapeDtypeStruct((n, D_MODEL), F32)],
        grid=(n // tm,),
        in_specs=_two_src_specs(tm, D_MODEL, split) + [pl.BlockSpec((1, D_MODEL), lambda i: (0, 0)),
                                                      pl.BlockSpec(w.shape, lambda i: (0, 0))],
        out_specs=[pl.BlockSpec((tm, w.shape[1]), lambda i: (i, 0)),
                   pl.BlockSpec((tm, D_MODEL), lambda i: (i, 0))],
        compiler_params=_cparams(("parallel",)),
        name="norm_proj_first",
    )(xa, xb, g, w)


def _ple_update(h, p, g_ref, wup_ref, wgate_ref):
    up = jnp.dot(p.astype(BF16), wup_ref[...], preferred_element_type=F32)
    gt = jnp.dot(_rms(h, g_ref[...]).astype(BF16), wgate_ref[...], preferred_element_type=F32)
    return h + up * jax.nn.sigmoid(gt)


def _ple_norm_proj_kernel(split, h_ref, pa_ref, pb_ref, gp_ref, wup_ref, wgate_ref, g_ref, w_ref, z_ref, hn_ref):
    hn = _ple_update(h_ref[...], _two_src_load(pa_ref, pb_ref, split), gp_ref, wup_ref, wgate_ref)
    hn_ref[...] = hn
    z_ref[...] = jnp.dot(_rms(hn, g_ref[...]).astype(BF16), w_ref[...], preferred_element_type=F32)


def _ple_norm_proj(h, pa, pb, layer, gp, wup, wgate, g, w, tm):
    n = h.shape[0]
    split = pa.shape[1] // tm
    return pl.pallas_call(
        functools.partial(_ple_norm_proj_kernel, split),
        out_shape=[jax.ShapeDtypeStruct((n, w.shape[1]), F32), jax.ShapeDtypeStruct((n, D_MODEL), F32)],
        grid=(n // tm,),
        in_specs=[pl.BlockSpec((tm, D_MODEL), lambda i: (i, 0))] + _two_src_specs(tm, PLE_DIM, split, layer)
        + [pl.BlockSpec((1, D_MODEL), lambda i: (0, 0)),
           pl.BlockSpec((PLE_DIM, D_MODEL), lambda i: (0, 0)),
           pl.BlockSpec((D_MODEL, D_MODEL), lambda i: (0, 0)),
           pl.BlockSpec((1, D_MODEL), lambda i: (0, 0)),
           pl.BlockSpec(w.shape, lambda i: (0, 0))],
        out_specs=[pl.BlockSpec((tm, w.shape[1]), lambda i: (i, 0)),
                   pl.BlockSpec((tm, D_MODEL), lambda i: (i, 0))],
        compiler_params=_cparams(("parallel",)),
        name="ple_norm_proj",
    )(h, pa, pb, gp, wup, wgate, g, w)


def _mixer_kernel(L, n_seg, seq,
                  z_ref, cos_ref, sin_ref, retd_ref, reteb_ref, retebl_ref, retsdec_ref,
                  sgla0_ref, sret0_ref, sgdn0_ref, conv0_ref,
                  wgu_ref, gbias_ref, alog_ref, dtb_ref, glan_ref, retn_ref, gdnn_ref, convw_ref,
                  tri_ref, de_ref, lmask_ref, gn64_ref, blkmask_ref, hmk_ref, hmv_ref,
                  o_ref, sgla_ref, sret_ref, sgdn_ref, convo_ref,
                  xpad_ref):
    R = L * n_seg
    n_lev = int(math.log2(L))
    c = pl.program_id(1)

    @pl.when(c == 0)
    def _():
        sgla_ref[...] = sgla0_ref[...]
        sret_ref[...] = sret0_ref[...]
        sgdn_ref[...] = sgdn0_ref[...]
        convo_ref[...] = conv0_ref[...]

    blkmask = blkmask_ref[...]
    hmk = hmk_ref[...]
    hmv = hmv_ref[...]
    small = z_ref[:, Z_SMALL:Z_SMALL + LANES]

    def slot(j):
        return 0 if seq else j

    def rows(j):
        return slice(j * L, (j + 1) * L)

    la = jax.nn.log_sigmoid(_bdot(small, wgu_ref[...]) + gbias_ref[...]) * (1.0 / GLA_TAU)
    g_all = -jnp.exp(alog_ref[...]) * jax.nn.softplus(small + dtb_ref[...])
    beta_all = jax.nn.sigmoid(small)
    la_parts = _split3(la)
    la3 = jnp.concatenate(la_parts, axis=1)
    g3 = jnp.concatenate(_split3(g_all), axis=1)
    cs = jnp.dot(tri_ref[...], jnp.concatenate([la3, g3], axis=1), preferred_element_type=F32)
    b = _sum3(cs[:, 0:3 * LANES])
    bg = _sum3(cs[:, 3 * LANES:6 * LANES])
    de = jnp.dot(de_ref[...], jnp.concatenate(la_parts[0:2], axis=1), preferred_element_type=F32)
    ede = jnp.exp(de[:, 0:LANES] + de[:, LANES:2 * LANES])

    def group_norm_gate(o, g, gate):
        sq = o * o
        hi = sq.astype(BF16)
        lo = (sq - hi.astype(F32)).astype(BF16)
        gn = gn64_ref[...]
        ms = (jnp.dot(hi, gn, preferred_element_type=F32) + jnp.dot(lo, gn, preferred_element_type=F32))
        return o * lax.rsqrt(ms + EPS) * g * _silu(gate)

    def intra_scores(q_list, kblk_list, masks):
        att = None
        for qm, kb, mk in zip(q_list, kblk_list, masks):
            term = mk * _bdot_t(qm, kb)
            att = term if att is None else att + term
        return att

    q = z_ref[:, Z_GQ:Z_GQ + GLA_KW] * (GLA_DK ** -0.5)
    k = z_ref[:, Z_GK:Z_GK + GLA_KW]
    v = z_ref[:, Z_GV:Z_GV + GLA_VW]
    qe = q * jnp.exp(b)
    b_t = b.T
    pending = []
    state = {"gla": sgla_ref[0] if seq else None, "ret": sret_ref[0] if seq else None}
    o_gla_parts, o_ret_parts = [], []

    def gla_chunk(j, q=q, k=k, v=v):
        r = rows(j)
        qj, kj, vj = q[r], k[r], v[r]
        kblk0 = jnp.concatenate([kj] * GLA_H, axis=0) * hmk
        q_list, kb_list, masks = [], [], []
        for lv in range(n_lev):
            eq = ede[lv * R + j * L:lv * R + (j + 1) * L]
            ek = ede[(n_lev + lv) * R + j * L:(n_lev + lv) * R + (j + 1) * L]
            q_list.append(qj * eq)
            kb_list.append(kblk0 * jnp.concatenate([ek] * GLA_H, axis=0))
            masks.append(lmask_ref[lv])
        q_list.append(qj)
        kb_list.append(kblk0)
        masks.append(lmask_ref[n_lev])
        att = intra_scores(q_list, kb_list, masks)
        vblk = jnp.concatenate([vj] * GLA_H, axis=0) * hmv
        last = (j + 1) * L - 1
        bl = b[last:last + 1, :]
        kv = blkmask * _bdot_tl(kj * jnp.exp(bl - b[r]), vj)
        decc = jnp.exp(b_t[:, last:last + 1])
        s_gla = state["gla"] if seq else sgla_ref[j]
        o_gla_parts.append(_bdot(att, vblk) + _bdot(qe[r], s_gla))
        s_gla = s_gla * decc + kv
        if seq:
            state["gla"] = s_gla
        else:
            sgla_ref[j] = s_gla

    def gla_finish():
        if seq:
            sgla_ref[0] = state["gla"]
        o_gla = jnp.concatenate(o_gla_parts, axis=0)
        o_ref[:, 0:GLA_VW] = group_norm_gate(o_gla, glan_ref[...], z_ref[:, Z_GR:Z_GR + GLA_VW]).astype(o_ref.dtype)

    cos = cos_ref[...]
    sin = sin_ref[...]
    lane = lax.broadcasted_iota(jnp.int32, (R, LANES), 1)
    first_half = (lane % RET_DK) < (RET_DK // 2)

    def rot(xx):
        sw = jnp.where(first_half, pltpu.roll(xx, LANES - RET_DK // 2, 1), pltpu.roll(xx, RET_DK // 2, 1))
        return xx * cos + sw * sin

    rq = rot(z_ref[:, Z_RQ:Z_RQ + RET_KW]) * (RET_DK ** -0.5)
    rk = rot(z_ref[:, Z_RK:Z_RK + RET_KW])
    rv = z_ref[:, Z_RV:Z_RV + RET_VW]
    retd = retd_ref[...]
    reteb = reteb_ref[...]
    retebl = retebl_ref[...]
    retsdec = retsdec_ref[...]

    def ret_chunk(j):
        r = rows(j)
        qj, kj, vj = rq[r], rk[r], rv[r]
        kblk = jnp.concatenate([kj] * RET_H, axis=0) * hmk
        att = _bdot_t(qj, kblk) * retd
        vblk = jnp.concatenate([vj] * RET_H, axis=0) * hmv
        kv = blkmask * _bdot_tl(kj * retebl, vj)
        s_ret = state["ret"] if seq else sret_ref[j]
        o_ret_parts.append(_bdot(att, vblk) + _bdot(qj * reteb, s_ret))
        s_ret = s_ret * retsdec + kv
        if seq:
            state["ret"] = s_ret
        else:
            sret_ref[j] = s_ret

    def ret_finish():
        if seq:
            sret_ref[0] = state["ret"]
        o_ret = jnp.concatenate(o_ret_parts, axis=0)
        o_ref[:, GLA_VW:GLA_VW + RET_VW] = group_norm_gate(
            o_ret, retn_ref[...], z_ref[:, Z_RR:Z_RR + RET_VW]).astype(o_ref.dtype)

    for j in range(n_seg):
        pending.append(functools.partial(gla_chunk, j))
        pending.append(functools.partial(ret_chunk, j))

    n_cs = 1 if seq else n_seg
    Lc = R // n_cs
    cw = convw_ref[...]
    xin = z_ref[:, Z_CONV:Z_CONV + GDN_CONV_CH]
    c_parts = []
    for s in range(n_cs):
        xs = xin[s * Lc:(s + 1) * Lc]
        xpad_ref[s, 8 - (GDN_CONV_W - 1):8, :] = convo_ref[s]
        xpad_ref[s, 8:8 + Lc, :] = xs
        convo_ref[s] = xs[Lc - (GDN_CONV_W - 1):Lc, :]
        cacc = xs * cw[GDN_CONV_W - 1:GDN_CONV_W, :]
        for jj in range(GDN_CONV_W - 1):
            sh = GDN_CONV_W - 1 - jj
            cacc = cacc + xpad_ref[s, 8 - sh:8 - sh + Lc, :] * cw[jj:jj + 1, :]
        c_parts.append(cacc)
    cact = _silu(c_parts[0] if n_cs == 1 else jnp.concatenate(c_parts, axis=0))

    row = lax.broadcasted_iota(jnp.int32, (R, R), 0)
    col = lax.broadcasted_iota(jnp.int32, (R, R), 1)
    same = lax.shift_right_logical(row, n_lev) == lax.shift_right_logical(col, n_lev)
    tril = same & (row >= col)
    strict = same & (row > col)
    eye_f = (row == col).astype(F32)
    bg_t = bg.T
    beta_t = beta_all.T
    gdnn = gdnn_ref[...]

    def l2n(xx):
        return xx * lax.rsqrt(jnp.sum(xx * xx, axis=-1, keepdims=True) + EPS)

    HS = range(GDN_H)
    qh = [l2n(cact[:, h * GDN_DK:(h + 1) * GDN_DK]) * (GDN_DK ** -0.5) for h in HS]
    kh = [l2n(cact[:, GDN_KW + h * GDN_DK:GDN_KW + (h + 1) * GDN_DK]) for h in HS]
    vh = [cact[:, 2 * GDN_KW + h * GDN_DV:2 * GDN_KW + (h + 1) * GDN_DV] for h in HS]
    bcol = [bg[:, SM_DA + h:SM_DA + h + 1] for h in HS]
    beta_row = [beta_t[SM_DB + h:SM_DB + h + 1, :] for h in HS]
    beta_col = [beta_all[:, SM_DB + h:SM_DB + h + 1] for h in HS]
    dec = [jnp.exp(jnp.where(tril, bcol[h] - bg_t[SM_DA + h:SM_DA + h + 1, :], -jnp.inf)) for h in HS]
    khb = [kh[h].astype(BF16) for h in HS]
    m = [_bdot_t(khb[h], khb[h]) * jnp.where(strict, dec[h], 0.0) * beta_row[h] for h in HS]
    tinv = [eye_f - m[h] for h in HS]
    pw = [m[h].astype(BF16) for h in HS]
    per_stage = -(-len(pending) // (n_lev - 1))
    span = 2
    while span < L:
        pw = [jnp.dot(pw[h], pw[h], preferred_element_type=F32).astype(BF16) for h in HS]
        tinv = [tinv[h] + jnp.dot(tinv[h].astype(BF16), pw[h], preferred_element_type=F32) for h in HS]
        for thunk in pending[:per_stage]:
            thunk()
        del pending[:per_stage]
        span *= 2
    assert not pending
    gla_finish()
    ret_finish()
    ebc = [jnp.exp(bcol[h]) for h in HS]
    x1 = [_bdot(tinv[h], jnp.concatenate([vh[h], ebc[h] * kh[h]], axis=1)).astype(BF16) for h in HS]
    qk = [_bdot_t(qh[h], khb[h]) * dec[h] * beta_row[h] for h in HS]
    x2 = [_bdot(qk[h], x1[h]) for h in HS]
    o0 = [x2[h][:, 0:GDN_DV] for h in HS]
    qeff = [ebc[h] * qh[h] - x2[h][:, GDN_DV:2 * GDN_DV] for h in HS]
    x3 = [[None] * n_seg for h in HS]
    blg = [[None] * n_seg for h in HS]
    for j in range(n_seg):
        r = rows(j)
        last = (j + 1) * L - 1
        for h in HS:
            blg[h][j] = bcol[h][last:last + 1, :]
            kt = kh[h][r] * (jnp.exp(blg[h][j] - bcol[h][r]) * beta_col[h][r])
            x3[h][j] = _bdot_tl(kt, x1[h][r])
    s_h = [sgdn_ref[0, h] if seq else None for h in HS]
    o_parts = [[] for h in HS]
    for j in range(n_seg):
        r = rows(j)
        for h in HS:
            if not seq:
                s_h[h] = sgdn_ref[j, h]
            y = _bdot(jnp.concatenate([qeff[h][r], x3[h][j][:, GDN_DV:2 * GDN_DV]], axis=0), s_h[h])
            o_parts[h].append(o0[h][r] + y[0:L])
            s_h[h] = jnp.exp(blg[h][j]) * s_h[h] - y[L:L + GDN_DK] + x3[h][j][:, 0:GDN_DV]
            if not seq:
                sgdn_ref[j, h] = s_h[h]
    for h in HS:
        if seq:
            sgdn_ref[0, h] = s_h[h]
        oh = jnp.concatenate(o_parts[h], axis=0)
        oh = _rms(oh, gdnn) * _silu(z_ref[:, Z_DZ + h * GDN_DV:Z_DZ + (h + 1) * GDN_DV])
        lo = GLA_VW + RET_VW + h * GDN_DV
        o_ref[:, lo:lo + GDN_DV] = oh.astype(o_ref.dtype)


def _head_block_mask(rows_per_head, cols_per_head, heads):
    r = np.arange(rows_per_head * heads)[:, None] // rows_per_head
    c = np.arange(cols_per_head * heads)[None, :] // cols_per_head
    return (r == c).astype(np.float32)


def _pack_consts(L, n_seg):
    R = L * n_seg
    t = np.arange(R)
    tl = t % L
    base = t - tl
    jj = np.arange(R)[None, :]
    tri = ((jj // L) == (t[:, None] // L)) & (jj <= t[:, None])
    tt = np.arange(L)[:, None]
    ss = np.arange(L)[None, :]
    d_list, e_list, masks = [], [], []
    m = L // 2
    while m >= 1:
        second = (tl // m) % 2 == 1
        ref = base + (tl // (2 * m)) * 2 * m + m - 1
        d_list.append(second[:, None] & (jj > ref[:, None]) & (jj <= t[:, None]))
        e_list.append((~second)[:, None] & (jj > t[:, None]) & (jj <= ref[:, None]))
        mk = (tt // (2 * m) == ss // (2 * m)) & ((tt // m) % 2 == 1) & ((ss // m) % 2 == 0)
        masks.append(np.tile(mk, (1, GLA_H)))
        m //= 2
    masks.append(np.tile(np.eye(L, dtype=bool), (1, GLA_H)))
    de = np.concatenate(d_list + e_list, axis=0)
    return (jnp.asarray(tri, BF16), jnp.asarray(de, BF16), jnp.asarray(np.stack(masks), F32))


def _mixer(zf, row0, B, T, pos0, seq, sgla0, sret0, sgdn0, conv0, prm):
    blk0 = row0 // MIX_ROWS
    if seq:
        L = CHUNK
        n_seg = MIX_ROWS // L
        grid = (B, T // MIX_ROWS)
        n_state = 1
    else:
        L = T
        n_seg = MIX_ROWS // L
        grid = (B // n_seg, 1)
        n_state = n_seg
    R = MIX_ROWS
    steps_per_b = grid[1]
    half = RET_DK // 2
    inv = ROPE_BASE ** (-jnp.arange(half, dtype=F32) / half)
    ang = (pos0 + jnp.arange(T, dtype=jnp.int32)).astype(F32)[:, None] * inv[None, :]
    cos_h = jnp.concatenate([jnp.cos(ang), jnp.cos(ang)], axis=1)
    sin_h = jnp.concatenate([-jnp.sin(ang), jnp.sin(ang)], axis=1)
    cos_t = jnp.tile(cos_h, (1, RET_H))
    sin_t = jnp.tile(sin_h, (1, RET_H))
    if not seq:
        cos_t = jnp.tile(cos_t, (n_seg, 1))
        sin_t = jnp.tile(sin_t, (n_seg, 1))
    log_gamma = jnp.log(1.0 - jnp.exp2(-5.0 - jnp.arange(RET_H, dtype=F32)))
    tpos = jnp.arange(L, dtype=F32)
    bret = (tpos[:, None] + 1.0) * log_gamma[None, :]
    dmat = tpos[:, None] - tpos[None, :]
    retd = jnp.where(dmat[None] >= 0, jnp.exp(dmat[None] * log_gamma[:, None, None]), 0.0)
    retd = jnp.transpose(retd, (1, 0, 2)).reshape(L, RET_H * L)
    reteb = jnp.repeat(jnp.exp(bret), RET_DK, axis=1)
    retebl = jnp.repeat(jnp.exp(bret[L - 1:L] - bret), RET_DK, axis=1)
    retsdec = jnp.broadcast_to(jnp.repeat(jnp.exp(bret[L - 1]), RET_DK)[:, None], (RET_KW, RET_VW))

    tri, de, lmask = _pack_consts(L, n_seg)
    gn64 = jnp.asarray(_head_block_mask(GLA_DV, GLA_DV, GLA_H) / GLA_DV, BF16)
    blkmask = jnp.asarray(_head_block_mask(GLA_DK, GLA_DV, GLA_H), F32)
    hmk = jnp.asarray(_head_block_mask(L, RET_DK, RET_H), F32)
    hmv = jnp.asarray(_head_block_mask(L, RET_DV, RET_H), F32)

    def full(a):
        nd = a.ndim
        return pl.BlockSpec(a.shape, lambda b, c: (0,) * nd)

    def per_b(a):
        nd = a.ndim
        return pl.BlockSpec((n_state,) + a.shape[1:], lambda b, c: (b,) + (0,) * (nd - 1))

    if seq:
        tab_spec = pl.BlockSpec((R, LANES), lambda b, c: (c, 0))
    else:
        tab_spec = pl.BlockSpec((R, LANES), lambda b, c: (0, 0))
    consts = [retd, reteb, retebl, retsdec]
    states = [sgla0, sret0, sgdn0, conv0]
    params = [prm["wgu"], prm["gbias"], prm["alog"], prm["dtb"], prm["glan"], prm["retn"], prm["gdnn"],
              prm["convw"], tri, de, lmask, gn64, blkmask, hmk, hmv]
    in_specs = ([pl.BlockSpec((R, Z_WIDTH), lambda b, c: (blk0 + b * steps_per_b + c, 0)), tab_spec, tab_spec]
                + [full(a) for a in consts] + [per_b(a) for a in states] + [full(a) for a in params])
    out_shape = [jax.ShapeDtypeStruct((B * T, D_MODEL), BF16)] + [
        jax.ShapeDtypeStruct(a.shape, F32) for a in states]
    out_specs = [pl.BlockSpec((R, D_MODEL), lambda b, c: (b * steps_per_b + c, 0))] + [per_b(a) for a in states]
    n_cs = 1 if seq else n_seg
    return pl.pallas_call(
        functools.partial(_mixer_kernel, L, n_seg, seq),
        out_shape=out_shape,
        grid=grid,
        in_specs=in_specs,
        out_specs=out_specs,
        scratch_shapes=[pltpu.VMEM((n_cs, R // n_cs + 8, GDN_CONV_CH), F32)],
        compiler_params=_cparams(("parallel", "arbitrary")),
        name="mixer",
    )(zf, cos_t, sin_t, *consts, *states, *params)


def _out_proj_kernel(split, h_ref, oa_ref, ob_ref, w_ref, out_ref):
    o = _two_src_load(oa_ref, ob_ref, split)
    out_ref[...] = h_ref[...] + jnp.dot(o, w_ref[...], preferred_element_type=F32)


def _out_proj(h, oa, ob, w, tm):
    n = h.shape[0]
    split = oa.shape[0] // tm
    return pl.pallas_call(
        functools.partial(_out_proj_kernel, split),
        out_shape=jax.ShapeDtypeStruct((n, D_MODEL), F32),
        grid=(n // tm,),
        in_specs=[pl.BlockSpec((tm, D_MODEL), lambda i: (i, 0))] + _two_src_specs(tm, D_MODEL, split)
        + [pl.BlockSpec((D_MODEL, D_MODEL), lambda i: (0, 0))],
        out_specs=pl.BlockSpec((tm, D_MODEL), lambda i: (i, 0)),
        compiler_params=_cparams(("parallel",)),
        name="out_proj",
    )(h, oa, ob, w)


FF_TILE = 1792
FF_SUB = 896


def _swiglu_partial(v, wg_ref, wu_ref, wd_ref):
    parts = []
    for c0 in range(0, wg_ref.shape[1], FF_SUB):
        a = jnp.dot(v, wg_ref[:, c0:c0 + FF_SUB], preferred_element_type=F32)
        u = jnp.dot(v, wu_ref[:, c0:c0 + FF_SUB], preferred_element_type=F32)
        parts.append((_silu(a) * u).astype(BF16))
    return jnp.dot(jnp.concatenate(parts, axis=1), wd_ref[...], preferred_element_type=F32)


def _ffn_kernel(split, h_ref, oa_ref, ob_ref, wo_ref, g_ref, wg_ref, wu_ref, wd_ref, out_ref, v_ref, acc_ref):
    f = pl.program_id(1)

    @pl.when(f == 0)
    def _():
        h2 = h_ref[...] + jnp.dot(_two_src_load(oa_ref, ob_ref, split), wo_ref[...], preferred_element_type=F32)
        v_ref[...] = _rms(h2, g_ref[...]).astype(BF16)
        acc_ref[...] = h2

    acc_ref[...] += _swiglu_partial(v_ref[...], wg_ref, wu_ref, wd_ref)

    @pl.when(f == pl.num_programs(1) - 1)
    def _():
        out_ref[...] = acc_ref[...]


def _ffn(h, oa, ob, wo, g, wg, wu, wd, tm, tf):
    n = h.shape[0]
    split = oa.shape[0] // tm
    return pl.pallas_call(
        functools.partial(_ffn_kernel, split),
        out_shape=jax.ShapeDtypeStruct((n, D_MODEL), F32),
        grid=(n // tm, D_FF // tf),
        in_specs=[pl.BlockSpec((tm, D_MODEL), lambda i, f: (i, 0))] + _two_src_specs(tm, D_MODEL, split)
        + [pl.BlockSpec((D_MODEL, D_MODEL), lambda i, f: (0, 0)),
           pl.BlockSpec((1, D_MODEL), lambda i, f: (0, 0)),
           pl.BlockSpec((D_MODEL, tf), lambda i, f: (0, f)),
           pl.BlockSpec((D_MODEL, tf), lambda i, f: (0, f)),
           pl.BlockSpec((tf, D_MODEL), lambda i, f: (f, 0))],
        out_specs=pl.BlockSpec((tm, D_MODEL), lambda i, f: (i, 0)),
        scratch_shapes=[pltpu.VMEM((tm, D_MODEL), BF16), pltpu.VMEM((tm, D_MODEL), F32)],
        compiler_params=_cparams(("parallel", "arbitrary")),
        name="ffn",
    )(h, oa, ob, wo, g, wg, wu, wd)


MOE_TM = 512
MOE_BLK = 512
MOE_TF = FF_TILE
SEG_ALIGN = 16
SEL_ROWS = 128


def _moe_rows(n):
    n_tiles = n // MOE_TM
    bound = 2 * n + n_tiles * N_EXPERTS * (SEG_ALIGN - 1) + N_EXPERTS * (MOE_BLK - 1)
    return -(-bound // MOE_BLK) * MOE_BLK


def _route_kernel(h_ref, g_ref, r_ref, v_ref, gate_ref, memb_ref, cnt_ref):
    vf = _rms(h_ref[...], g_ref[...])
    v_hi = vf.astype(BF16)
    v_ref[...] = v_hi
    v_mid = (vf - v_hi.astype(F32)).astype(BF16)
    logits = jnp.dot(jnp.concatenate([v_hi, v_hi, v_mid], axis=1), r_ref[...],
                     preferred_element_type=F32)
    lane = lax.broadcasted_iota(jnp.int32, logits.shape, 1)
    neg = jnp.float32(-jnp.inf)
    lg = jnp.where(lane < N_EXPERTS, logits, neg)
    m1 = jnp.max(lg, axis=1, keepdims=True)
    i1 = jnp.min(jnp.where(lg == m1, lane, LANES), axis=1, keepdims=True)
    lg2 = jnp.where(lane == i1, neg, lg)
    m2 = jnp.max(lg2, axis=1, keepdims=True)
    i2 = jnp.min(jnp.where(lg2 == m2, lane, LANES), axis=1, keepdims=True)
    e2 = jnp.exp(m2 - m1)
    den = 1.0 + e2
    gate_ref[...] = jnp.where(lane == i1, 1.0 / den, 0.0) + jnp.where(lane == i2, e2 / den, 0.0)
    memb = jnp.where((lane == i1) | (lane == i2), 1.0, 0.0)
    memb_ref[...] = memb
    cnt_ref[...] = jnp.broadcast_to(jnp.sum(memb, axis=0, keepdims=True), cnt_ref.shape)


def _moe_route(h, g, router):
    n = h.shape[0]
    tm = MOE_TM
    r_hi = router.astype(BF16)
    r_mid = (router - r_hi.astype(F32)).astype(BF16)
    r3 = jnp.concatenate([r_hi, r_mid, r_hi], axis=0)
    return pl.pallas_call(
        _route_kernel,
        out_shape=[jax.ShapeDtypeStruct((n, D_MODEL), BF16), jax.ShapeDtypeStruct((n, LANES), F32),
                   jax.ShapeDtypeStruct((n, LANES), F32), jax.ShapeDtypeStruct((n // tm, 8, LANES), F32)],
        grid=(n // tm,),
        in_specs=[pl.BlockSpec((tm, D_MODEL), lambda i: (i, 0)),
                  pl.BlockSpec((1, D_MODEL), lambda i: (0, 0)),
                  pl.BlockSpec((3 * D_MODEL, LANES), lambda i: (0, 0))],
        out_specs=[pl.BlockSpec((tm, D_MODEL), lambda i: (i, 0)),
                   pl.BlockSpec((tm, LANES), lambda i: (i, 0)),
                   pl.BlockSpec((tm, LANES), lambda i: (i, 0)),
                   pl.BlockSpec((None, 8, LANES), lambda i: (i, 0, 0))],
        compiler_params=_cparams(("parallel",)),
        name="moe_route",
    )(h, g, r3)


def _seg_pad(c):
    return (c + (SEG_ALIGN - 1)) // SEG_ALIGN * SEG_ALIGN


def _gather_kernel(base_ref, cnt_ref, v_ref, memb_ref, tri_ref, xs_in_ref, xs_ref, stg_ref, sem, nout_ref):
    del xs_in_ref
    i = pl.program_id(0)
    n_steps = pl.num_programs(0)
    par = i % 2
    unit = SEG_ALIGN

    def unit_copy(p, src_row, dst_row):
        return pltpu.make_async_copy(stg_ref.at[p, pl.ds(src_row, unit)], xs_ref.at[pl.ds(dst_row, unit)], sem.at[p])

    def wait_units(p, count):
        def body(u, carry):
            unit_copy(p, 0, 0).wait()
            return carry
        lax.fori_loop(0, count, body, 0)

    @pl.when(i >= 2)
    def _():
        wait_units(par, nout_ref[par])

    tm = v_ref.shape[0]
    memb = memb_ref[...]
    rank = jnp.dot(tri_ref[...], memb.astype(BF16), preferred_element_type=F32)
    rank_t = rank.T[0:N_EXPERTS, :]
    is_m = memb.T[0:N_EXPERTS, :] > 0.0
    sub = lax.broadcasted_iota(jnp.int32, (N_EXPERTS, tm), 0)
    pos = rank_t
    so = jnp.int32(0)
    for e in range(N_EXPERTS):
        pos = pos + jnp.where(sub == e, so.astype(F32), 0.0)
        so = so + _seg_pad(cnt_ref[i, e])
    stg_rows = stg_ref.shape[1]
    pos_a = jnp.min(jnp.where(is_m, pos, float(stg_rows)), axis=0, keepdims=True)
    pos_b = jnp.max(jnp.where(is_m, pos, -1.0), axis=0, keepdims=True)
    ridx = lax.broadcasted_iota(jnp.int32, (stg_rows, tm), 0).astype(F32)
    sel = jnp.where((ridx == pos_a) | (ridx == pos_b), 1.0, 0.0).astype(BF16)
    stg_ref[par] = jnp.dot(sel, v_ref[...], preferred_element_type=F32).astype(BF16)

    so = jnp.int32(0)
    for e in range(N_EXPERTS):
        cp = _seg_pad(cnt_ref[i, e])
        dst = base_ref[i, e]

        def send(u, carry, so=so, dst=dst):
            unit_copy(par, pl.multiple_of(so + u * unit, unit), pl.multiple_of(dst + u * unit, unit)).start()
            return carry

        lax.fori_loop(0, cp // unit, send, 0)
        so = so + cp
    nout_ref[par] = so // unit

    @pl.when(i == n_steps - 1)
    def _():
        wait_units(par, nout_ref[par])

        @pl.when(i >= 1)
        def _():
            wait_units(1 - par, nout_ref[1 - par])


def _moe_gather(base, cnt, v, memb, tri, xs_rows):
    n = v.shape[0]
    tm = MOE_TM
    stg_rows = 2 * tm + N_EXPERTS * SEG_ALIGN
    xs0 = jnp.zeros((xs_rows, D_MODEL), BF16)
    grid_spec = pltpu.PrefetchScalarGridSpec(
        num_scalar_prefetch=2,
        grid=(n // tm,),
        in_specs=[pl.BlockSpec((tm, D_MODEL), lambda i, b, c: (i, 0)),
                  pl.BlockSpec((tm, LANES), lambda i, b, c: (i, 0)),
                  pl.BlockSpec((tm, tm), lambda i, b, c: (0, 0)),
                  pl.BlockSpec(memory_space=pl.ANY)],
        out_specs=pl.BlockSpec(memory_space=pl.ANY),
        scratch_shapes=[pltpu.VMEM((2, stg_rows, D_MODEL), BF16), pltpu.SemaphoreType.DMA((2,)),
                        pltpu.SMEM((2,), jnp.int32)],
    )
    return pl.pallas_call(
        _gather_kernel,
        out_shape=jax.ShapeDtypeStruct((xs_rows, D_MODEL), BF16),
        grid_spec=grid_spec,
        input_output_aliases={5: 0},
        compiler_params=_cparams(("arbitrary",)),
        name="moe_gather",
    )(base, cnt, v, memb, tri, xs0)


def _expert_kernel(bexp_ref, bval_ref, xs_ref, wg_ref, wu_ref, wd_ref, ys_ref, acc_ref):
    del bexp_ref
    k = pl.program_id(0)
    f = pl.program_id(1)

    @pl.when(f == 0)
    def _():
        acc_ref[...] = jnp.zeros_like(acc_ref)

    @pl.when(bval_ref[k] > 0)
    def _():
        acc_ref[...] += _swiglu_partial(xs_ref[...], wg_ref, wu_ref, wd_ref)

    @pl.when(f == pl.num_programs(1) - 1)
    def _():
        ys_ref[...] = acc_ref[...]


def _moe_experts(bexp, bval, xs, wg, wu, wd):
    n_blk = xs.shape[0] // MOE_BLK
    n_f = D_FF // MOE_TF

    def f_idx(k, f, bval):
        return jnp.where(bval[k] > 0, f, n_f - 1)

    grid_spec = pltpu.PrefetchScalarGridSpec(
        num_scalar_prefetch=2,
        grid=(n_blk + 1, n_f),
        in_specs=[pl.BlockSpec((MOE_BLK, D_MODEL), lambda k, f, be, bv: (jnp.minimum(k, n_blk - 1), 0)),
                  pl.BlockSpec((None, D_MODEL, MOE_TF), lambda k, f, be, bv: (be[k], 0, f_idx(k, f, bv))),
                  pl.BlockSpec((None, D_MODEL, MOE_TF), lambda k, f, be, bv: (be[k], 0, f_idx(k, f, bv))),
                  pl.BlockSpec((None, MOE_TF, D_MODEL), lambda k, f, be, bv: (be[k], f_idx(k, f, bv), 0))],
        out_specs=pl.BlockSpec((MOE_BLK, D_MODEL), lambda k, f, be, bv: (k, 0)),
        scratch_shapes=[pltpu.VMEM((MOE_BLK, D_MODEL), F32)],
    )
    return pl.pallas_call(
        _expert_kernel,
        out_shape=jax.ShapeDtypeStruct(((n_blk + 1) * MOE_BLK, D_MODEL), F32),
        grid_spec=grid_spec,
        compiler_params=_cparams(("arbitrary", "arbitrary")),
        name="moe_experts",
    )(bexp, bval, xs, wg, wu, wd)


def _combine_kernel(base_ref, cnt_ref, h_ref, gate_ref, memb_ref, tri_ref, ys_ref, out_ref, ybuf_ref, acc_ref, sem):
    i = pl.program_id(0)
    n_steps = pl.num_programs(0)
    par = i % 2
    tm = h_ref.shape[0]

    def chunk_copy(p, src_row, slot_row):
        return pltpu.make_async_copy(ys_ref.at[pl.ds(src_row, SEL_ROWS)], ybuf_ref.at[p, pl.ds(slot_row, SEL_ROWS)],
                                     sem.at[p])

    def n_chunks(c):
        return (c + (SEL_ROWS - 1)) // SEL_ROWS

    def fetch(step, p):
        slot = jnp.int32(0)
        for e in range(N_EXPERTS):
            nq = n_chunks(cnt_ref[step, e])
            src = base_ref[step, e]

            def body(qq, carry, slot=slot, src=src):
                chunk_copy(p, pl.multiple_of(src + qq * SEL_ROWS, SEG_ALIGN),
                           pl.multiple_of((slot + qq) * SEL_ROWS, SEL_ROWS)).start()
                return carry

            lax.fori_loop(0, nq, body, 0)
            slot = slot + nq
        return slot

    @pl.when(i == 0)
    def _():
        fetch(0, 0)

    @pl.when(i + 1 < n_steps)
    def _():
        fetch(i + 1, 1 - par)

    total = jnp.int32(0)
    for e in range(N_EXPERTS):
        total = total + n_chunks(cnt_ref[i, e])

    def wait_body(u, carry):
        chunk_copy(par, 0, 0).wait()
        return carry

    lax.fori_loop(0, total, wait_body, 0)

    memb = memb_ref[...]
    gates = gate_ref[...]
    rank = jnp.dot(tri_ref[...], memb.astype(BF16), preferred_element_type=F32)
    cidx = lax.broadcasted_iota(jnp.int32, (tm, SEL_ROWS), 1).astype(F32)
    acc_ref[...] = h_ref[...]

    slot = jnp.int32(0)
    for e in range(N_EXPERTS):
        nq = n_chunks(cnt_ref[i, e])
        rcol = rank[:, e:e + 1]
        mcol = memb[:, e:e + 1]
        gcol = gates[:, e:e + 1]

        def chunk(qq, carry, slot=slot, rcol=rcol, mcol=mcol, gcol=gcol):
            y = ybuf_ref[par, pl.ds(pl.multiple_of((slot + qq) * SEL_ROWS, SEL_ROWS), SEL_ROWS), :]
            sel = jnp.where((cidx + (qq * SEL_ROWS).astype(F32) == rcol) & (mcol > 0.0), 1.0, 0.0).astype(BF16)
            yh = y.astype(BF16)
            yl = (y - yh.astype(F32)).astype(BF16)
            got = jnp.dot(jnp.concatenate([sel, sel], axis=1), jnp.concatenate([yh, yl], axis=0),
                          preferred_element_type=F32)
            acc_ref[...] += gcol * got
            return carry

        lax.fori_loop(0, nq, chunk, 0)
        slot = slot + nq
    out_ref[...] = acc_ref[...]


def _moe_combine(base, cnt, h, gates, memb, tri, ys):
    n = h.shape[0]
    tm = MOE_TM
    max_chunks = 2 * tm // SEL_ROWS + N_EXPERTS
    grid_spec = pltpu.PrefetchScalarGridSpec(
        num_scalar_prefetch=2,
        grid=(n // tm,),
        in_specs=[pl.BlockSpec((tm, D_MODEL), lambda i, b, c: (i, 0)),
                  pl.BlockSpec((tm, LANES), lambda i, b, c: (i, 0)),
                  pl.BlockSpec((tm, LANES), lambda i, b, c: (i, 0)),
                  pl.BlockSpec((tm, tm), lambda i, b, c: (0, 0)),
                  pl.BlockSpec(memory_space=pl.ANY)],
        out_specs=pl.BlockSpec((tm, D_MODEL), lambda i, b, c: (i, 0)),
        scratch_shapes=[pltpu.VMEM((2, max_chunks * SEL_ROWS, D_MODEL), F32), pltpu.VMEM((tm, D_MODEL), F32),
                        pltpu.SemaphoreType.DMA((2,))],
    )
    return pl.pallas_call(
        _combine_kernel,
        out_shape=jax.ShapeDtypeStruct((n, D_MODEL), F32),
        grid_spec=grid_spec,
        compiler_params=_cparams(("arbitrary",)),
        name="moe_combine",
    )(base, cnt, h, gates, memb, tri, ys)


def _moe(h, g, router, wg, wu, wd):
    n = h.shape[0]
    xs_rows = _moe_rows(n)
    n_blk = xs_rows // MOE_BLK
    v, gates, memb, cnt_f = _moe_route(h, g, router)
    cnt = cnt_f[:, 0, :N_EXPERTS].astype(jnp.int32)
    cp = _seg_pad(cnt)
    exp_rows = -(-jnp.sum(cp, axis=0) // MOE_BLK) * MOE_BLK
    exp_end = jnp.cumsum(exp_rows)
    exp_start = exp_end - exp_rows
    base = (exp_start[None, :] + jnp.cumsum(cp, axis=0) - cp).astype(jnp.int32)
    blk_row = jnp.arange(n_blk + 1, dtype=jnp.int32) * MOE_BLK
    bval = (blk_row < exp_end[-1]).astype(jnp.int32)
    bexp = jnp.minimum(jnp.sum((blk_row[:, None] >= exp_end[None, :]).astype(jnp.int32), axis=1), N_EXPERTS - 1)
    last_valid = jnp.maximum(jnp.sum(bval) - 1, 0)
    bexp = jnp.where(bval > 0, bexp, bexp[last_valid]).astype(jnp.int32)
    tri = jnp.asarray(np.tril(np.ones((MOE_TM, MOE_TM), np.float32), -1), BF16)
    xs = _moe_gather(base, cnt, v, memb, tri, xs_rows)
    ys = _moe_experts(bexp, bval, xs, wg, wu, wd)
    return _moe_combine(base, cnt, h, gates, memb, tri, ys)


def _ple_final_kernel(split, h_ref, pa_ref, pb_ref, g_ref, wup_ref, wgate_ref, gf_ref, ya_ref, yb_ref):
    hn = _ple_update(h_ref[...], _two_src_load(pa_ref, pb_ref, split), g_ref, wup_ref, wgate_ref)
    y = _rms(hn, gf_ref[...])
    i = pl.program_id(0)

    @pl.when(i < split)
    def _():
        ya_ref[...] = y

    @pl.when(i >= split)
    def _():
        yb_ref[...] = y


def _ple_final(h, pa, pb, layer, g, wup, wgate, gf, tm):
    n = h.shape[0]
    split = pa.shape[1] // tm
    return pl.pallas_call(
        functools.partial(_ple_final_kernel, split),
        out_shape=[jax.ShapeDtypeStruct((pa.shape[1], D_MODEL), F32), jax.ShapeDtypeStruct((pb.shape[1], D_MODEL), F32)],
        grid=(n // tm,),
        in_specs=[pl.BlockSpec((tm, D_MODEL), lambda i: (i, 0))] + _two_src_specs(tm, PLE_DIM, split, layer)
        + [pl.BlockSpec((1, D_MODEL), lambda i: (0, 0)),
           pl.BlockSpec((PLE_DIM, D_MODEL), lambda i: (0, 0)),
           pl.BlockSpec((D_MODEL, D_MODEL), lambda i: (0, 0)),
           pl.BlockSpec((1, D_MODEL), lambda i: (0, 0))],
        out_specs=_two_src_specs(tm, D_MODEL, split),
        compiler_params=_cparams(("arbitrary",)),
        name="ple_final",
    )(h, pa, pb, g, wup, wgate, gf)


def _reorder_w_in(w):
    sizes = (GLA_KW, GLA_KW, GLA_VW, GLA_RANK, GLA_VW, RET_KW, RET_KW, RET_VW, RET_VW,
             GDN_CONV_CH, GDN_H, GDN_H, GDN_VW)
    pts = np.cumsum(np.array(sizes))[:-1].tolist()
    gq, gk, gv, glr, gr, rq, rk, rv, rr, dqkv, da, db, dz = jnp.split(w, pts, axis=1)
    pad = jnp.zeros((w.shape[0], LANES - GLA_RANK - 2 * GDN_H), w.dtype)
    return jnp.concatenate([gq, gk, gv, gr, rq, rk, rv, rr, dqkv, dz, glr, da, db, pad], axis=1)


def _lane_row(vals, offset):
    return jnp.zeros((1, LANES), F32).at[0, offset:offset + vals.shape[0]].set(vals.astype(F32))


def _to_blockdiag(s):
    B, H, dk, dv = s.shape
    eye = jnp.eye(H, dtype=s.dtype)
    return jnp.einsum("bhde,hg->bhdge", s, eye).reshape(B, H * dk, H * dv)


def _from_blockdiag(s, H):
    B, R, C = s.shape
    dk, dv = R // H, C // H
    s5 = s.reshape(B, H, dk, H, dv)
    return jnp.stack([s5[:, h, :, h, :] for h in range(H)], axis=1)


def kernel(x_prompt, x_sample, state_gla, state_ret, state_gdn, state_gdn_conv, p_prompt, p_sample, norm_mix, w_in, gla_w_gate_up, gla_b_gate, gla_norm, ret_norm, gdn_conv, gdn_a_log, gdn_dt_bias, gdn_norm, w_out, norm_ffn, ffn_w_gate, ffn_w_up, ffn_w_down, moe_router, moe_w_gate, moe_w_up, moe_w_down, ple_w_up, ple_norm, ple_w_gate, norm_final):
    depth = w_in.shape[0]
    Bp, Tp, _ = x_prompt.shape
    Bs, Ts, _ = x_sample.shape
    n_p, n_s = Bp * Tp, Bs * Ts
    tm = TOKEN_TILE
    xp = x_prompt.reshape(n_p, D_MODEL)
    xs = x_sample.reshape(n_s, D_MODEL)
    pp = p_prompt.reshape(depth, n_p, PLE_DIM)
    ps = p_sample.reshape(depth, n_s, PLE_DIM)

    outs_p = [[], [], [], []]
    outs_s = [[], [], [], []]
    h = None
    for i in range(depth):
        w_in_k = _reorder_w_in(w_in[i].astype(BF16))
        prm = dict(
            wgu=jnp.zeros((LANES, GLA_KW), F32).at[:GLA_RANK].set(gla_w_gate_up[i]).astype(BF16),
            gbias=gla_b_gate[i].reshape(1, GLA_KW).astype(F32),
            alog=_lane_row(gdn_a_log[i], SM_DA),
            dtb=_lane_row(gdn_dt_bias[i], SM_DA),
            glan=jnp.tile(gla_norm[i].astype(F32), GLA_H).reshape(1, GLA_VW),
            retn=jnp.tile(ret_norm[i].astype(F32), RET_H).reshape(1, RET_VW),
            gdnn=gdn_norm[i].astype(F32).reshape(1, GDN_DV),
            convw=gdn_conv[i].astype(F32),
        )
        if i == 0:
            z, h = _norm_proj2(xp, xs, norm_mix[i].reshape(1, D_MODEL), w_in_k, tm)
        else:
            z, h = _ple_norm_proj(h, pp, ps, i - 1, ple_norm[i - 1].reshape(1, D_MODEL),
                                  ple_w_up[i - 1].astype(BF16), ple_w_gate[i - 1].astype(BF16),
                                  norm_mix[i].reshape(1, D_MODEL), w_in_k, tm)
        o_p, gla_p, ret_p, gdn_p, conv_p = _mixer(
            z, 0, Bp, Tp, 0, True,
            jnp.zeros((Bp, GLA_KW, GLA_VW), F32), jnp.zeros((Bp, RET_KW, RET_VW), F32),
            jnp.zeros((Bp, GDN_H, GDN_DK, GDN_DV), F32), jnp.zeros((Bp, GDN_CONV_W - 1, GDN_CONV_CH), F32), prm)
        o_s, gla_s, ret_s, gdn_s, conv_s = _mixer(
            z, n_p, Bs, Ts, PAST_LEN, False,
            _to_blockdiag(state_gla[i].astype(F32)), _to_blockdiag(state_ret[i].astype(F32)),
            state_gdn[i].astype(F32), state_gdn_conv[i].astype(F32), prm)
        for lst, val in zip(outs_p, (_from_blockdiag(gla_p, GLA_H), _from_blockdiag(ret_p, RET_H), gdn_p, conv_p)):
            lst.append(val)
        for lst, val in zip(outs_s, (_from_blockdiag(gla_s, GLA_H), _from_blockdiag(ret_s, RET_H), gdn_s, conv_s)):
            lst.append(val)
        wo = w_out[i].astype(BF16)
        j = i // 2
        if i % 2 == 0:
            h = _ffn(h, o_p, o_s, wo, norm_ffn[i].reshape(1, D_MODEL), ffn_w_gate[j].astype(BF16),
                     ffn_w_up[j].astype(BF16), ffn_w_down[j].astype(BF16), tm, FF_TILE)
        else:
            router = jnp.zeros((D_MODEL, LANES), F32).at[:, :N_EXPERTS].set(moe_router[j].astype(F32))
            h = _out_proj(h, o_p, o_s, wo, tm)
            h = _moe(h, norm_ffn[i].reshape(1, D_MODEL), router, moe_w_gate[j].astype(BF16),
                     moe_w_up[j].astype(BF16), moe_w_down[j].astype(BF16))

    last = depth - 1
    y_p, y_s = _ple_final(h, pp, ps, last, ple_norm[last].reshape(1, D_MODEL), ple_w_up[last].astype(BF16),
                          ple_w_gate[last].astype(BF16), norm_final.reshape(1, D_MODEL), tm)
    return (y_p.reshape(Bp, Tp, D_MODEL), y_s.reshape(Bs, Ts, D_MODEL),
            jnp.stack(outs_p[0]), jnp.stack(outs_p[1]), jnp.stack(outs_p[2]), jnp.stack(outs_p[3]),
            jnp.stack(outs_s[0]), jnp.stack(outs_s[1]), jnp.stack(outs_s[2]), jnp.stack(outs_s[3]))
```

```python
import functools
import math

import jax
import jax.numpy as jnp
import numpy as np
from jax import lax
from jax.experimental import pallas as pl
from jax.experimental.pallas import tpu as pltpu

F32 = jnp.float32
BF16 = jnp.bfloat16

D_MODEL = 1024
CHUNK = 64
PLE_DIM = 256
EPS = 1e-6
PAST_LEN = 4096

GLA_H, GLA_DK, GLA_DV, GLA_RANK, GLA_TAU = 4, 32, 64, 16, 16.0
RET_H, RET_DK, RET_DV = 4, 32, 64
ROPE_BASE = 10000.0
GDN_H, GDN_DK, GDN_DV, GDN_CONV_W = 4, 128, 128, 4
GLA_KW, GLA_VW = GLA_H * GLA_DK, GLA_H * GLA_DV
RET_KW, RET_VW = RET_H * RET_DK, RET_H * RET_DV
GDN_KW, GDN_VW = GDN_H * GDN_DK, GDN_H * GDN_DV
GDN_CONV_CH = 2 * GDN_KW + GDN_VW
D_FF = 3584
N_EXPERTS = 8

LANES = 128

Z_GQ, Z_GK, Z_GV, Z_GR = 0, 128, 256, 512
Z_RQ, Z_RK, Z_RV, Z_RR = 768, 896, 1024, 1280
Z_CONV = 1536
Z_DZ = Z_CONV + GDN_CONV_CH
Z_SMALL = Z_DZ + GDN_VW
Z_WIDTH = Z_SMALL + LANES
SM_DA, SM_DB = GLA_RANK, GLA_RANK + GDN_H

MIX_ROWS = 256
GDN_SUB = 128
TOKEN_TILE = 512
VMEM_LIMIT = 56 * 1024 * 1024


def _cparams(sem):
    return pltpu.CompilerParams(dimension_semantics=sem, vmem_limit_bytes=VMEM_LIMIT)


def _rms(x, g):
    return x * lax.rsqrt(jnp.mean(x * x, axis=-1, keepdims=True) + EPS) * g


def _bdot(a, b):
    return jnp.dot(a.astype(BF16), b.astype(BF16), preferred_element_type=F32)


def _bdot_t(a, b):
    return lax.dot_general(a.astype(BF16), b.astype(BF16), (((1,), (1,)), ((), ())),
                           preferred_element_type=F32)


def _bdot_tl(a, b):
    return lax.dot_general(a.astype(BF16), b.astype(BF16), (((0,), (0,)), ((), ())),
                           preferred_element_type=F32)


def _silu(x):
    return x * jax.nn.sigmoid(x)


def _split3(a):
    hi = a.astype(BF16)
    r1 = a - hi.astype(F32)
    mid = r1.astype(BF16)
    lo = (r1 - mid.astype(F32)).astype(BF16)
    return [hi, mid, lo]


def _sum3(x):
    return x[:, 0:LANES] + x[:, LANES:2 * LANES] + x[:, 2 * LANES:3 * LANES]


def _two_src_specs(tm, width, split, layer=None):
    if layer is None:
        return [pl.BlockSpec((tm, width), lambda i, *_: (jnp.minimum(i, split - 1), 0)),
                pl.BlockSpec((tm, width), lambda i, *_: (jnp.maximum(i - split, 0), 0))]
    return [pl.BlockSpec((None, tm, width), lambda i, *_: (layer, jnp.minimum(i, split - 1), 0)),
            pl.BlockSpec((None, tm, width), lambda i, *_: (layer, jnp.maximum(i - split, 0), 0))]


def _two_src_load(a_ref, b_ref, split):
    return jnp.where(pl.program_id(0) < split, a_ref[...], b_ref[...])


def _norm_proj2_kernel(split, xa_ref, xb_ref, g_ref, w_ref, z_ref, h_ref):
    x = _two_src_load(xa_ref, xb_ref, split)
    h_ref[...] = x
    z_ref[...] = jnp.dot(_rms(x, g_ref[...]).astype(BF16), w_ref[...], preferred_element_type=F32)


def _norm_proj2(xa, xb, g, w, tm):
    n = xa.shape[0] + xb.shape[0]
    split = xa.shape[0] // tm
    return pl.pallas_call(
        functools.partial(_norm_proj2_kernel, split),
        out_shape=[jax.ShapeDtypeStruct((n, w.shape[1]), F32), jax.ShapeDtypeStruct((n, D_MODEL), F32)],
        grid=(n // tm,),
        in_specs=_two_src_specs(tm, D_MODEL, split) + [pl.BlockSpec((1, D_MODEL), lambda i: (0, 0)),
                                                      pl.BlockSpec(w.shape, lambda i: (0, 0))],
        out_specs=[pl.BlockSpec((tm, w.shape[1]), lambda i: (i, 0)),
                   pl.BlockSpec((tm, D_MODEL), lambda i: (i, 0))],
        compiler_params=_cparams(("parallel",)),
        name="norm_proj_first",
    )(xa, xb, g, w)


def _ple_update(h, p, g_ref, wup_ref, wgate_ref):
    up = jnp.dot(p.astype(BF16), wup_ref[...], preferred_element_type=F32)
    gt = jnp.dot(_rms(h, g_ref[...]).astype(BF16), wgate_ref[...], preferred_element_type=F32)
    return h + up * jax.nn.sigmoid(gt)


def _ple_norm_proj_kernel(split, h_ref, pa_ref, pb_ref, gp_ref, wup_ref, wgate_ref, g_ref, w_ref, z_ref, hn_ref):
    hn = _ple_update(h_ref[...], _two_src_load(pa_ref, pb_ref, split), gp_ref, wup_ref, wgate_ref)
    hn_ref[...] = hn
    z_ref[...] = jnp.dot(_rms(hn, g_ref[...]).astype(BF16), w_ref[...], preferred_element_type=F32)


def _ple_norm_proj(h, pa, pb, layer, gp, wup, wgate, g, w, tm):
    n = h.shape[0]
    split = pa.shape[1] // tm
    return pl.pallas_call(
        functools.partial(_ple_norm_proj_kernel, split),
        out_shape=[jax.ShapeDtypeStruct((n, w.shape[1]), F32), jax.ShapeDtypeStruct((n, D_MODEL), F32)],
        grid=(n // tm,),
        in_specs=[pl.BlockSpec((tm, D_MODEL), lambda i: (i, 0))] + _two_src_specs(tm, PLE_DIM, split, layer)
        + [pl.BlockSpec((1, D_MODEL), lambda i: (0, 0)),
           pl.BlockSpec((PLE_DIM, D_MODEL), lambda i: (0, 0)),
           pl.BlockSpec((D_MODEL, D_MODEL), lambda i: (0, 0)),
           pl.BlockSpec((1, D_MODEL), lambda i: (0, 0)),
           pl.BlockSpec(w.shape, lambda i: (0, 0))],
        out_specs=[pl.BlockSpec((tm, w.shape[1]), lambda i: (i, 0)),
                   pl.BlockSpec((tm, D_MODEL), lambda i: (i, 0))],
        compiler_params=_cparams(("parallel",)),
        name="ple_norm_proj",
    )(h, pa, pb, gp, wup, wgate, g, w)


def _mixer_kernel(L, n_seg, seq,
                  z_ref, cos_ref, sin_ref, retd_ref, reteb_ref, retebl_ref, retsdec_ref,
                  sgla0_ref, sret0_ref, sgdn0_ref, conv0_ref,
                  wgu_ref, gbias_ref, alog_ref, dtb_ref, glan_ref, retn_ref, gdnn_ref, convw_ref,
                  tri_ref, de_ref, lmask_ref, gn64_ref, blkmask_ref, hmk_ref, hmv_ref,
                  o_ref, sgla_out_ref, sret_out_ref, sgdn_ref, convo_ref,
                  xpad_ref, sgla_ref, sret_ref):
    R = L * n_seg
    n_lev = int(math.log2(L))
    c = pl.program_id(1)
    n_state = sgla_ref.shape[0]

    def load_blockdiag(src_ref, dst_ref, dk, dv):
        for s in range(n_state):
            for h in range(GLA_H):
                pieces = []
                if h > 0:
                    pieces.append(jnp.zeros((dk, h * dv), F32))
                pieces.append(src_ref[s, h])
                if h < GLA_H - 1:
                    pieces.append(jnp.zeros((dk, (GLA_H - 1 - h) * dv), F32))
                dst_ref[s, h * dk:(h + 1) * dk, :] = jnp.concatenate(pieces, axis=1)

    def store_blockdiag(src_ref, dst_ref, dk, dv):
        for s in range(n_state):
            for h in range(GLA_H):
                dst_ref[s, h] = src_ref[s, h * dk:(h + 1) * dk, h * dv:(h + 1) * dv]

    @pl.when(c == 0)
    def _():
        load_blockdiag(sgla0_ref, sgla_ref, GLA_DK, GLA_DV)
        load_blockdiag(sret0_ref, sret_ref, RET_DK, RET_DV)
        sgdn_ref[...] = sgdn0_ref[...]
        convo_ref[...] = conv0_ref[...]

    blkmask = blkmask_ref[...]
    hmk = hmk_ref[...]
    hmv = hmv_ref[...]
    small = z_ref[:, Z_SMALL:Z_SMALL + LANES]

    def slot(j):
        return 0 if seq else j

    def rows(j):
        return slice(j * L, (j + 1) * L)

    la = jax.nn.log_sigmoid(_bdot(small, wgu_ref[...]) + gbias_ref[...]) * (1.0 / GLA_TAU)
    g_all = -jnp.exp(alog_ref[...]) * jax.nn.softplus(small + dtb_ref[...])
    beta_all = jax.nn.sigmoid(small)
    la_parts = _split3(la)
    la3 = jnp.concatenate(la_parts, axis=1)
    g3 = jnp.concatenate(_split3(g_all), axis=1)
    cs = jnp.dot(tri_ref[...], jnp.concatenate([la3, g3], axis=1), preferred_element_type=F32)
    b = _sum3(cs[:, 0:3 * LANES])
    bg = _sum3(cs[:, 3 * LANES:6 * LANES])
    de = jnp.dot(de_ref[...], jnp.concatenate(la_parts[0:2], axis=1), preferred_element_type=F32)
    ede = jnp.exp(de[:, 0:LANES] + de[:, LANES:2 * LANES])

    def group_norm_gate(o, g, gate):
        sq = o * o
        hi = sq.astype(BF16)
        lo = (sq - hi.astype(F32)).astype(BF16)
        gn = gn64_ref[...]
        ms = (jnp.dot(hi, gn, preferred_element_type=F32) + jnp.dot(lo, gn, preferred_element_type=F32))
        return o * lax.rsqrt(ms + EPS) * g * _silu(gate)

    def intra_scores(q_list, kblk_list, masks):
        att = None
        for qm, kb, mk in zip(q_list, kblk_list, masks):
            term = mk * _bdot_t(qm, kb)
            att = term if att is None else att + term
        return att

    q = z_ref[:, Z_GQ:Z_GQ + GLA_KW] * (GLA_DK ** -0.5)
    k = z_ref[:, Z_GK:Z_GK + GLA_KW]
    v = z_ref[:, Z_GV:Z_GV + GLA_VW]
    qe = q * jnp.exp(b)
    b_t = b.T
    pending = []
    state = {"gla": sgla_ref[0] if seq else None, "ret": sret_ref[0] if seq else None}
    o_gla_parts, o_ret_parts = [], []

    def gla_chunk(j, q=q, k=k, v=v):
        r = rows(j)
        qj, kj, vj = q[r], k[r], v[r]
        kblk0 = jnp.concatenate([kj] * GLA_H, axis=0) * hmk
        q_list, kb_list, masks = [], [], []
        for lv in range(n_lev):
            eq = ede[lv * R + j * L:lv * R + (j + 1) * L]
            ek = ede[(n_lev + lv) * R + j * L:(n_lev + lv) * R + (j + 1) * L]
            q_list.append(qj * eq)
            kb_list.append(kblk0 * jnp.concatenate([ek] * GLA_H, axis=0))
            masks.append(lmask_ref[lv])
        q_list.append(qj)
        kb_list.append(kblk0)
        masks.append(lmask_ref[n_lev])
        att = intra_scores(q_list, kb_list, masks)
        vblk = jnp.concatenate([vj] * GLA_H, axis=0) * hmv
        last = (j + 1) * L - 1
        bl = b[last:last + 1, :]
        kv = blkmask * _bdot_tl(kj * jnp.exp(bl - b[r]), vj)
        decc = jnp.exp(b_t[:, last:last + 1])
        s_gla = state["gla"] if seq else sgla_ref[j]
        o_gla_parts.append(_bdot(att, vblk) + _bdot(qe[r], s_gla))
        s_gla = s_gla * decc + kv
        if seq:
            state["gla"] = s_gla
        else:
            sgla_ref[j] = s_gla

    def gla_finish():
        if seq:
            sgla_ref[0] = state["gla"]
        o_gla = jnp.concatenate(o_gla_parts, axis=0)
        o_ref[:, 0:GLA_VW] = group_norm_gate(o_gla, glan_ref[...], z_ref[:, Z_GR:Z_GR + GLA_VW]).astype(o_ref.dtype)

    cos = cos_ref[...]
    sin = sin_ref[...]
    lane = lax.broadcasted_iota(jnp.int32, (R, LANES), 1)
    first_half = (lane % RET_DK) < (RET_DK // 2)

    def rot(xx):
        sw = jnp.where(first_half, pltpu.roll(xx, LANES - RET_DK // 2, 1), pltpu.roll(xx, RET_DK // 2, 1))
        return xx * cos + sw * sin

    rq = rot(z_ref[:, Z_RQ:Z_RQ + RET_KW]) * (RET_DK ** -0.5)
    rk = rot(z_ref[:, Z_RK:Z_RK + RET_KW])
    rv = z_ref[:, Z_RV:Z_RV + RET_VW]
    retd = retd_ref[...]
    reteb = reteb_ref[...]
    retebl = retebl_ref[...]
    retsdec = retsdec_ref[...]

    def ret_chunk(j):
        r = rows(j)
        qj, kj, vj = rq[r], rk[r], rv[r]
        kblk = jnp.concatenate([kj] * RET_H, axis=0) * hmk
        att = _bdot_t(qj, kblk) * retd
        vblk = jnp.concatenate([vj] * RET_H, axis=0) * hmv
        kv = blkmask * _bdot_tl(kj * retebl, vj)
        s_ret = state["ret"] if seq else sret_ref[j]
        o_ret_parts.append(_bdot(att, vblk) + _bdot(qj * reteb, s_ret))
        s_ret = s_ret * retsdec + kv
        if seq:
            state["ret"] = s_ret
        else:
            sret_ref[j] = s_ret

    def ret_finish():
        if seq:
            sret_ref[0] = state["ret"]
        o_ret = jnp.concatenate(o_ret_parts, axis=0)
        o_ref[:, GLA_VW:GLA_VW + RET_VW] = group_norm_gate(
            o_ret, retn_ref[...], z_ref[:, Z_RR:Z_RR + RET_VW]).astype(o_ref.dtype)

    for j in range(n_seg):
        pending.append(functools.partial(gla_chunk, j))
        pending.append(functools.partial(ret_chunk, j))

    n_cs = 1 if seq else n_seg
    Lc = R // n_cs
    cw = convw_ref[...]
    xin = z_ref[:, Z_CONV:Z_CONV + GDN_CONV_CH]
    c_parts = []
    for s in range(n_cs):
        xs = xin[s * Lc:(s + 1) * Lc]
        xpad_ref[s, 8 - (GDN_CONV_W - 1):8, :] = convo_ref[s]
        xpad_ref[s, 8:8 + Lc, :] = xs
        convo_ref[s] = xs[Lc - (GDN_CONV_W - 1):Lc, :]
        cacc = xs * cw[GDN_CONV_W - 1:GDN_CONV_W, :]
        for jj in range(GDN_CONV_W - 1):
            sh = GDN_CONV_W - 1 - jj
            cacc = cacc + xpad_ref[s, 8 - sh:8 - sh + Lc, :] * cw[jj:jj + 1, :]
        c_parts.append(cacc)
    cact = _silu(c_parts[0] if n_cs == 1 else jnp.concatenate(c_parts, axis=0))

    n_sub = R // GDN_SUB
    row = lax.broadcasted_iota(jnp.int32, (GDN_SUB, GDN_SUB), 0)
    col = lax.broadcasted_iota(jnp.int32, (GDN_SUB, GDN_SUB), 1)
    same = lax.shift_right_logical(row, n_lev) == lax.shift_right_logical(col, n_lev)
    tril = same & (row >= col)
    strict = same & (row > col)
    eye_f = (row == col).astype(F32)
    bg_t = bg.T
    beta_t = beta_all.T
    gdnn = gdnn_ref[...]

    def l2n(xx):
        return xx * lax.rsqrt(jnp.sum(xx * xx, axis=-1, keepdims=True) + EPS)

    HS = range(GDN_H)
    PS = [(h, sp) for h in HS for sp in range(n_sub)]

    def sub(a, sp):
        return a[sp * GDN_SUB:(sp + 1) * GDN_SUB]

    qh = [l2n(cact[:, h * GDN_DK:(h + 1) * GDN_DK]) * (GDN_DK ** -0.5) for h in HS]
    kh = [l2n(cact[:, GDN_KW + h * GDN_DK:GDN_KW + (h + 1) * GDN_DK]) for h in HS]
    vh = [cact[:, 2 * GDN_KW + h * GDN_DV:2 * GDN_KW + (h + 1) * GDN_DV] for h in HS]
    bcol = [bg[:, SM_DA + h:SM_DA + h + 1] for h in HS]
    beta_col = [beta_all[:, SM_DB + h:SM_DB + h + 1] for h in HS]
    ebc = [jnp.exp(bcol[h]) for h in HS]
    khb = [kh[h].astype(BF16) for h in HS]
    beta_row = {(h, sp): beta_t[SM_DB + h:SM_DB + h + 1, sp * GDN_SUB:(sp + 1) * GDN_SUB] for h, sp in PS}
    dec = {(h, sp): jnp.exp(jnp.where(
        tril, sub(bcol[h], sp) - bg_t[SM_DA + h:SM_DA + h + 1, sp * GDN_SUB:(sp + 1) * GDN_SUB], -jnp.inf))
        for h, sp in PS}
    m = {(h, sp): _bdot_t(sub(khb[h], sp), sub(khb[h], sp)) * jnp.where(strict, dec[h, sp], 0.0) * beta_row[h, sp]
         for h, sp in PS}
    tinv = {p: eye_f - m[p] for p in PS}
    pw = {p: m[p].astype(BF16) for p in PS}
    per_stage = -(-len(pending) // (n_lev - 1))
    span = 2
    while span < L:
        pw = {p: jnp.dot(pw[p], pw[p], preferred_element_type=F32).astype(BF16) for p in PS}
        tinv = {p: tinv[p] + jnp.dot(tinv[p].astype(BF16), pw[p], preferred_element_type=F32) for p in PS}
        for thunk in pending[:per_stage]:
            thunk()
        del pending[:per_stage]
        span *= 2
    assert not pending
    gla_finish()
    ret_finish()
    store_blockdiag(sgla_ref, sgla_out_ref, GLA_DK, GLA_DV)
    store_blockdiag(sret_ref, sret_out_ref, RET_DK, RET_DV)
    x1 = {(h, sp): _bdot(tinv[h, sp], jnp.concatenate([sub(vh[h], sp), sub(ebc[h] * kh[h], sp)], axis=1)).astype(BF16)
          for h, sp in PS}
    qk = {(h, sp): _bdot_t(sub(qh[h], sp), sub(khb[h], sp)) * dec[h, sp] * beta_row[h, sp] for h, sp in PS}
    x2 = {p: _bdot(qk[p], x1[p]) for p in PS}
    qeff = {(h, sp): sub(ebc[h] * qh[h], sp) - x2[h, sp][:, GDN_DV:2 * GDN_DV] for h, sp in PS}
    x3 = [[None] * n_seg for h in HS]
    blg = [[None] * n_seg for h in HS]
    for j in range(n_seg):
        r = rows(j)
        sp, lo_r = divmod(j * L, GDN_SUB)
        last = (j + 1) * L - 1
        for h in HS:
            blg[h][j] = bcol[h][last:last + 1, :]
            kt = kh[h][r] * (jnp.exp(blg[h][j] - bcol[h][r]) * beta_col[h][r])
            x3[h][j] = _bdot_tl(kt, x1[h, sp][lo_r:lo_r + L])
    s_h = [sgdn_ref[0, h] if seq else None for h in HS]
    o_parts = [[] for h in HS]
    for j in range(n_seg):
        sp, lo_r = divmod(j * L, GDN_SUB)
        for h in HS:
            if not seq:
                s_h[h] = sgdn_ref[j, h]
            y = _bdot(jnp.concatenate([qeff[h, sp][lo_r:lo_r + L], x3[h][j][:, GDN_DV:2 * GDN_DV]], axis=0),
                      s_h[h])
            o_parts[h].append(x2[h, sp][lo_r:lo_r + L, 0:GDN_DV] + y[0:L])
            s_h[h] = jnp.exp(blg[h][j]) * s_h[h] - y[L:L + GDN_DK] + x3[h][j][:, 0:GDN_DV]
            if not seq:
                sgdn_ref[j, h] = s_h[h]
    for h in HS:
        if seq:
            sgdn_ref[0, h] = s_h[h]
        oh = jnp.concatenate(o_parts[h], axis=0)
        oh = _rms(oh, gdnn) * _silu(z_ref[:, Z_DZ + h * GDN_DV:Z_DZ + (h + 1) * GDN_DV])
        lo = GLA_VW + RET_VW + h * GDN_DV
        o_ref[:, lo:lo + GDN_DV] = oh.astype(o_ref.dtype)


def _head_block_mask(rows_per_head, cols_per_head, heads):
    r = np.arange(rows_per_head * heads)[:, None] // rows_per_head
    c = np.arange(cols_per_head * heads)[None, :] // cols_per_head
    return (r == c).astype(np.float32)


def _pack_consts(L, n_seg):
    R = L * n_seg
    t = np.arange(R)
    tl = t % L
    base = t - tl
    jj = np.arange(R)[None, :]
    tri = ((jj // L) == (t[:, None] // L)) & (jj <= t[:, None])
    tt = np.arange(L)[:, None]
    ss = np.arange(L)[None, :]
    d_list, e_list, masks = [], [], []
    m = L // 2
    while m >= 1:
        second = (tl // m) % 2 == 1
        ref = base + (tl // (2 * m)) * 2 * m + m - 1
        d_list.append(second[:, None] & (jj > ref[:, None]) & (jj <= t[:, None]))
        e_list.append((~second)[:, None] & (jj > t[:, None]) & (jj <= ref[:, None]))
        mk = (tt // (2 * m) == ss // (2 * m)) & ((tt // m) % 2 == 1) & ((ss // m) % 2 == 0)
        masks.append(np.tile(mk, (1, GLA_H)))
        m //= 2
    masks.append(np.tile(np.eye(L, dtype=bool), (1, GLA_H)))
    de = np.concatenate(d_list + e_list, axis=0)
    return (jnp.asarray(tri, BF16), jnp.asarray(de, BF16), jnp.asarray(np.stack(masks), F32))


def _mixer(zf, row0, B, T, pos0, seq, sgla0, sret0, sgdn0, conv0, prm):
    blk0 = row0 // MIX_ROWS
    if seq:
        L = CHUNK
        n_seg = MIX_ROWS // L
        grid = (B, T // MIX_ROWS)
        n_state = 1
    else:
        L = T
        n_seg = MIX_ROWS // L
        grid = (B // n_seg, 1)
        n_state = n_seg
    R = MIX_ROWS
    steps_per_b = grid[1]
    half = RET_DK // 2
    inv = ROPE_BASE ** (-jnp.arange(half, dtype=F32) / half)
    ang = (pos0 + jnp.arange(T, dtype=jnp.int32)).astype(F32)[:, None] * inv[None, :]
    cos_h = jnp.concatenate([jnp.cos(ang), jnp.cos(ang)], axis=1)
    sin_h = jnp.concatenate([-jnp.sin(ang), jnp.sin(ang)], axis=1)
    cos_t = jnp.tile(cos_h, (1, RET_H))
    sin_t = jnp.tile(sin_h, (1, RET_H))
    if not seq:
        cos_t = jnp.tile(cos_t, (n_seg, 1))
        sin_t = jnp.tile(sin_t, (n_seg, 1))
    log_gamma = jnp.log(1.0 - jnp.exp2(-5.0 - jnp.arange(RET_H, dtype=F32)))
    tpos = jnp.arange(L, dtype=F32)
    bret = (tpos[:, None] + 1.0) * log_gamma[None, :]
    dmat = tpos[:, None] - tpos[None, :]
    retd = jnp.where(dmat[None] >= 0, jnp.exp(dmat[None] * log_gamma[:, None, None]), 0.0)
    retd = jnp.transpose(retd, (1, 0, 2)).reshape(L, RET_H * L)
    reteb = jnp.repeat(jnp.exp(bret), RET_DK, axis=1)
    retebl = jnp.repeat(jnp.exp(bret[L - 1:L] - bret), RET_DK, axis=1)
    retsdec = jnp.broadcast_to(jnp.repeat(jnp.exp(bret[L - 1]), RET_DK)[:, None], (RET_KW, RET_VW))

    tri, de, lmask = _pack_consts(L, n_seg)
    gn64 = jnp.asarray(_head_block_mask(GLA_DV, GLA_DV, GLA_H) / GLA_DV, BF16)
    blkmask = jnp.asarray(_head_block_mask(GLA_DK, GLA_DV, GLA_H), F32)
    hmk = jnp.asarray(_head_block_mask(L, RET_DK, RET_H), F32)
    hmv = jnp.asarray(_head_block_mask(L, RET_DV, RET_H), F32)

    def full(a):
        nd = a.ndim
        return pl.BlockSpec(a.shape, lambda b, c: (0,) * nd)

    def per_b(a):
        nd = a.ndim
        return pl.BlockSpec((n_state,) + a.shape[1:], lambda b, c: (b,) + (0,) * (nd - 1))

    if seq:
        tab_spec = pl.BlockSpec((R, LANES), lambda b, c: (c, 0))
    else:
        tab_spec = pl.BlockSpec((R, LANES), lambda b, c: (0, 0))
    consts = [retd, reteb, retebl, retsdec]
    states = [sgla0, sret0, sgdn0, conv0]
    params = [prm["wgu"], prm["gbias"], prm["alog"], prm["dtb"], prm["glan"], prm["retn"], prm["gdnn"],
              prm["convw"], tri, de, lmask, gn64, blkmask, hmk, hmv]
    in_specs = ([pl.BlockSpec((R, Z_WIDTH), lambda b, c: (blk0 + b * steps_per_b + c, 0)), tab_spec, tab_spec]
                + [full(a) for a in consts] + [per_b(a) for a in states] + [full(a) for a in params])
    out_shape = [jax.ShapeDtypeStruct((B * T, D_MODEL), BF16)] + [
        jax.ShapeDtypeStruct(a.shape, F32) for a in states]
    out_specs = [pl.BlockSpec((R, D_MODEL), lambda b, c: (b * steps_per_b + c, 0))] + [per_b(a) for a in states]
    n_cs = 1 if seq else n_seg
    return pl.pallas_call(
        functools.partial(_mixer_kernel, L, n_seg, seq),
        out_shape=out_shape,
        grid=grid,
        in_specs=in_specs,
        out_specs=out_specs,
        scratch_shapes=[pltpu.VMEM((n_cs, R // n_cs + 8, GDN_CONV_CH), F32),
                        pltpu.VMEM((n_state, GLA_KW, GLA_VW), F32), pltpu.VMEM((n_state, RET_KW, RET_VW), F32)],
        compiler_params=_cparams(("parallel", "arbitrary")),
        name="mixer",
    )(zf, cos_t, sin_t, *consts, *states, *params)


def _out_proj_kernel(split, h_ref, oa_ref, ob_ref, w_ref, out_ref):
    o = _two_src_load(oa_ref, ob_ref, split)
    out_ref[...] = h_ref[...] + jnp.dot(o, w_ref[...], preferred_element_type=F32)


def _out_proj(h, oa, ob, w, tm):
    n = h.shape[0]
    split = oa.shape[0] // tm
    return pl.pallas_call(
        functools.partial(_out_proj_kernel, split),
        out_shape=jax.ShapeDtypeStruct((n, D_MODEL), F32),
        grid=(n // tm,),
        in_specs=[pl.BlockSpec((tm, D_MODEL), lambda i: (i, 0))] + _two_src_specs(tm, D_MODEL, split)
        + [pl.BlockSpec((D_MODEL, D_MODEL), lambda i: (0, 0))],
        out_specs=pl.BlockSpec((tm, D_MODEL), lambda i: (i, 0)),
        compiler_params=_cparams(("parallel",)),
        name="out_proj",
    )(h, oa, ob, w)


FF_TILE = 1792
FF_SUB = 896


def _swiglu_partial(v, wg_ref, wu_ref, wd_ref):
    parts = []
    for c0 in range(0, wg_ref.shape[1], FF_SUB):
        a = jnp.dot(v, wg_ref[:, c0:c0 + FF_SUB], preferred_element_type=F32)
        u = jnp.dot(v, wu_ref[:, c0:c0 + FF_SUB], preferred_element_type=F32)
        parts.append((_silu(a) * u).astype(BF16))
    return jnp.dot(jnp.concatenate(parts, axis=1), wd_ref[...], preferred_element_type=F32)


def _ffn_kernel(split, h_ref, oa_ref, ob_ref, wo_ref, g_ref, wg_ref, wu_ref, wd_ref, out_ref, v_ref, acc_ref):
    f = pl.program_id(1)

    @pl.when(f == 0)
    def _():
        h2 = h_ref[...] + jnp.dot(_two_src_load(oa_ref, ob_ref, split), wo_ref[...], preferred_element_type=F32)
        v_ref[...] = _rms(h2, g_ref[...]).astype(BF16)
        acc_ref[...] = h2

    acc_ref[...] += _swiglu_partial(v_ref[...], wg_ref, wu_ref, wd_ref)

    @pl.when(f == pl.num_programs(1) - 1)
    def _():
        out_ref[...] = acc_ref[...]


def _ffn(h, oa, ob, wo, g, wg, wu, wd, tm, tf):
    n = h.shape[0]
    split = oa.shape[0] // tm
    return pl.pallas_call(
        functools.partial(_ffn_kernel, split),
        out_shape=jax.ShapeDtypeStruct((n, D_MODEL), F32),
        grid=(n // tm, D_FF // tf),
        in_specs=[pl.BlockSpec((tm, D_MODEL), lambda i, f: (i, 0))] + _two_src_specs(tm, D_MODEL, split)
        + [pl.BlockSpec((D_MODEL, D_MODEL), lambda i, f: (0, 0)),
           pl.BlockSpec((1, D_MODEL), lambda i, f: (0, 0)),
           pl.BlockSpec((D_MODEL, tf), lambda i, f: (0, f)),
           pl.BlockSpec((D_MODEL, tf), lambda i, f: (0, f)),
           pl.BlockSpec((tf, D_MODEL), lambda i, f: (f, 0))],
        out_specs=pl.BlockSpec((tm, D_MODEL), lambda i, f: (i, 0)),
        scratch_shapes=[pltpu.VMEM((tm, D_MODEL), BF16), pltpu.VMEM((tm, D_MODEL), F32)],
        compiler_params=_cparams(("parallel", "arbitrary")),
        name="ffn",
    )(h, oa, ob, wo, g, wg, wu, wd)


MOE_TM = 512
MOE_BLK = 512
MOE_TF = FF_TILE
SEG_ALIGN = 16
SEL_ROWS = 128


def _moe_rows(n):
    n_tiles = n // MOE_TM
    bound = 2 * n + n_tiles * N_EXPERTS * (SEG_ALIGN - 1) + N_EXPERTS * (MOE_BLK - 1)
    return -(-bound // MOE_BLK) * MOE_BLK


def _route_kernel(h_ref, g_ref, r_ref, v_ref, gate_ref, memb_ref, cnt_ref):
    vf = _rms(h_ref[...], g_ref[...])
    v_hi = vf.astype(BF16)
    v_ref[...] = v_hi
    v_mid = (vf - v_hi.astype(F32)).astype(BF16)
    logits = jnp.dot(jnp.concatenate([v_hi, v_hi, v_mid], axis=1), r_ref[...],
                     preferred_element_type=F32)
    lane = lax.broadcasted_iota(jnp.int32, logits.shape, 1)
    neg = jnp.float32(-jnp.inf)
    lg = jnp.where(lane < N_EXPERTS, logits, neg)
    m1 = jnp.max(lg, axis=1, keepdims=True)
    i1 = jnp.min(jnp.where(lg == m1, lane, LANES), axis=1, keepdims=True)
    lg2 = jnp.where(lane == i1, neg, lg)
    m2 = jnp.max(lg2, axis=1, keepdims=True)
    i2 = jnp.min(jnp.where(lg2 == m2, lane, LANES), axis=1, keepdims=True)
    e2 = jnp.exp(m2 - m1)
    den = 1.0 + e2
    gate_ref[...] = jnp.where(lane == i1, 1.0 / den, 0.0) + jnp.where(lane == i2, e2 / den, 0.0)
    memb = jnp.where((lane == i1) | (lane == i2), 1.0, 0.0)
    memb_ref[...] = memb
    cnt_ref[...] = jnp.broadcast_to(jnp.sum(memb, axis=0, keepdims=True), cnt_ref.shape)


def _moe_route(h, g, router):
    n = h.shape[0]
    tm = MOE_TM
    r_hi = router.astype(BF16)
    r_mid = (router - r_hi.astype(F32)).astype(BF16)
    r3 = jnp.concatenate([r_hi, r_mid, r_hi], axis=0)
    return pl.pallas_call(
        _route_kernel,
        out_shape=[jax.ShapeDtypeStruct((n, D_MODEL), BF16), jax.ShapeDtypeStruct((n, LANES), F32),
                   jax.ShapeDtypeStruct((n, LANES), F32), jax.ShapeDtypeStruct((n // tm, 8, LANES), F32)],
        grid=(n // tm,),
        in_specs=[pl.BlockSpec((tm, D_MODEL), lambda i: (i, 0)),
                  pl.BlockSpec((1, D_MODEL), lambda i: (0, 0)),
                  pl.BlockSpec((3 * D_MODEL, LANES), lambda i: (0, 0))],
        out_specs=[pl.BlockSpec((tm, D_MODEL), lambda i: (i, 0)),
                   pl.BlockSpec((tm, LANES), lambda i: (i, 0)),
                   pl.BlockSpec((tm, LANES), lambda i: (i, 0)),
                   pl.BlockSpec((None, 8, LANES), lambda i: (i, 0, 0))],
        compiler_params=_cparams(("parallel",)),
        name="moe_route",
    )(h, g, r3)


def _seg_pad(c):
    return (c + (SEG_ALIGN - 1)) // SEG_ALIGN * SEG_ALIGN


def _gather_kernel(base_ref, cnt_ref, v_ref, memb_ref, tri_ref, xs_in_ref, xs_ref, stg_ref, sem, nout_ref):
    del xs_in_ref
    i = pl.program_id(0)
    n_steps = pl.num_programs(0)
    par = i % 2
    unit = SEG_ALIGN

    def unit_copy(p, src_row, dst_row):
        return pltpu.make_async_copy(stg_ref.at[p, pl.ds(src_row, unit)], xs_ref.at[pl.ds(dst_row, unit)], sem.at[p])

    def wait_units(p, count):
        def body(u, carry):
            unit_copy(p, 0, 0).wait()
            return carry
        lax.fori_loop(0, count, body, 0)

    @pl.when(i >= 2)
    def _():
        wait_units(par, nout_ref[par])

    tm = v_ref.shape[0]
    memb = memb_ref[...]
    rank = jnp.dot(tri_ref[...], memb.astype(BF16), preferred_element_type=F32)
    rank_t = rank.T[0:N_EXPERTS, :]
    is_m = memb.T[0:N_EXPERTS, :] > 0.0
    sub = lax.broadcasted_iota(jnp.int32, (N_EXPERTS, tm), 0)
    pos = rank_t
    so = jnp.int32(0)
    for e in range(N_EXPERTS):
        pos = pos + jnp.where(sub == e, so.astype(F32), 0.0)
        so = so + _seg_pad(cnt_ref[i, e])
    stg_rows = stg_ref.shape[1]
    pos_a = jnp.min(jnp.where(is_m, pos, float(stg_rows)), axis=0, keepdims=True)
    pos_b = jnp.max(jnp.where(is_m, pos, -1.0), axis=0, keepdims=True)
    ridx = lax.broadcasted_iota(jnp.int32, (stg_rows, tm), 0).astype(F32)
    sel = jnp.where((ridx == pos_a) | (ridx == pos_b), 1.0, 0.0).astype(BF16)
    stg_ref[par] = jnp.dot(sel, v_ref[...], preferred_element_type=F32).astype(BF16)

    so = jnp.int32(0)
    for e in range(N_EXPERTS):
        cp = _seg_pad(cnt_ref[i, e])
        dst = base_ref[i, e]

        def send(u, carry, so=so, dst=dst):
            unit_copy(par, pl.multiple_of(so + u * unit, unit), pl.multiple_of(dst + u * unit, unit)).start()
            return carry

        lax.fori_loop(0, cp // unit, send, 0)
        so = so + cp
    nout_ref[par] = so // unit

    @pl.when(i == n_steps - 1)
    def _():
        wait_units(par, nout_ref[par])

        @pl.when(i >= 1)
        def _():
            wait_units(1 - par, nout_ref[1 - par])


def _moe_gather(base, cnt, v, memb, tri, xs_rows):
    n = v.shape[0]
    tm = MOE_TM
    stg_rows = 2 * tm + N_EXPERTS * SEG_ALIGN
    xs0 = jnp.zeros((xs_rows, D_MODEL), BF16)
    grid_spec = pltpu.PrefetchScalarGridSpec(
        num_scalar_prefetch=2,
        grid=(n // tm,),
        in_specs=[pl.BlockSpec((tm, D_MODEL), lambda i, b, c: (i, 0)),
                  pl.BlockSpec((tm, LANES), lambda i, b, c: (i, 0)),
                  pl.BlockSpec((tm, tm), lambda i, b, c: (0, 0)),
                  pl.BlockSpec(memory_space=pl.ANY)],
        out_specs=pl.BlockSpec(memory_space=pl.ANY),
        scratch_shapes=[pltpu.VMEM((2, stg_rows, D_MODEL), BF16), pltpu.SemaphoreType.DMA((2,)),
                        pltpu.SMEM((2,), jnp.int32)],
    )
    return pl.pallas_call(
        _gather_kernel,
        out_shape=jax.ShapeDtypeStruct((xs_rows, D_MODEL), BF16),
        grid_spec=grid_spec,
        input_output_aliases={5: 0},
        compiler_params=_cparams(("arbitrary",)),
        name="moe_gather",
    )(base, cnt, v, memb, tri, xs0)


def _expert_kernel(bexp_ref, bval_ref, xs_ref, wg_ref, wu_ref, wd_ref, ys_ref, acc_ref):
    del bexp_ref
    k = pl.program_id(0)
    f = pl.program_id(1)

    @pl.when(f == 0)
    def _():
        acc_ref[...] = jnp.zeros_like(acc_ref)

    @pl.when(bval_ref[k] > 0)
    def _():
        acc_ref[...] += _swiglu_partial(xs_ref[...], wg_ref, wu_ref, wd_ref)

    @pl.when(f == pl.num_programs(1) - 1)
    def _():
        ys_ref[...] = acc_ref[...]


def _moe_experts(bexp, bval, xs, wg, wu, wd):
    n_blk = xs.shape[0] // MOE_BLK
    n_f = D_FF // MOE_TF

    def f_idx(k, f, bval):
        return jnp.where(bval[k] > 0, f, n_f - 1)

    grid_spec = pltpu.PrefetchScalarGridSpec(
        num_scalar_prefetch=2,
        grid=(n_blk + 1, n_f),
        in_specs=[pl.BlockSpec((MOE_BLK, D_MODEL), lambda k, f, be, bv: (jnp.minimum(k, n_blk - 1), 0)),
                  pl.BlockSpec((None, D_MODEL, MOE_TF), lambda k, f, be, bv: (be[k], 0, f_idx(k, f, bv))),
                  pl.BlockSpec((None, D_MODEL, MOE_TF), lambda k, f, be, bv: (be[k], 0, f_idx(k, f, bv))),
                  pl.BlockSpec((None, MOE_TF, D_MODEL), lambda k, f, be, bv: (be[k], f_idx(k, f, bv), 0))],
        out_specs=pl.BlockSpec((MOE_BLK, D_MODEL), lambda k, f, be, bv: (k, 0)),
        scratch_shapes=[pltpu.VMEM((MOE_BLK, D_MODEL), F32)],
    )
    return pl.pallas_call(
        _expert_kernel,
        out_shape=jax.ShapeDtypeStruct(((n_blk + 1) * MOE_BLK, D_MODEL), F32),
        grid_spec=grid_spec,
        compiler_params=_cparams(("arbitrary", "arbitrary")),
        name="moe_experts",
    )(bexp, bval, xs, wg, wu, wd)


def _combine_kernel(base_ref, cnt_ref, h_ref, gate_ref, memb_ref, tri_ref, ys_ref, out_ref, ybuf_ref, acc_ref, sem):
    i = pl.program_id(0)
    n_steps = pl.num_programs(0)
    par = i % 2
    tm = h_ref.shape[0]

    def chunk_copy(p, src_row, slot_row):
        return pltpu.make_async_copy(ys_ref.at[pl.ds(src_row, SEL_ROWS)], ybuf_ref.at[p, pl.ds(slot_row, SEL_ROWS)],
                                     sem.at[p])

    def n_chunks(c):
        return (c + (SEL_ROWS - 1)) // SEL_ROWS

    def fetch(step, p):
        slot = jnp.int32(0)
        for e in range(N_EXPERTS):
            nq = n_chunks(cnt_ref[step, e])
            src = base_ref[step, e]

            def body(qq, carry, slot=slot, src=src):
                chunk_copy(p, pl.multiple_of(src + qq * SEL_ROWS, SEG_ALIGN),
                           pl.multiple_of((slot + qq) * SEL_ROWS, SEL_ROWS)).start()
                return carry

            lax.fori_loop(0, nq, body, 0)
            slot = slot + nq
        return slot

    @pl.when(i == 0)
    def _():
        fetch(0, 0)

    @pl.when(i + 1 < n_steps)
    def _():
        fetch(i + 1, 1 - par)

    total = jnp.int32(0)
    for e in range(N_EXPERTS):
        total = total + n_chunks(cnt_ref[i, e])

    def wait_body(u, carry):
        chunk_copy(par, 0, 0).wait()
        return carry

    lax.fori_loop(0, total, wait_body, 0)

    memb = memb_ref[...]
    gates = gate_ref[...]
    rank = jnp.dot(tri_ref[...], memb.astype(BF16), preferred_element_type=F32)
    cidx = lax.broadcasted_iota(jnp.int32, (tm, SEL_ROWS), 1).astype(F32)
    acc_ref[...] = h_ref[...]

    slot = jnp.int32(0)
    for e in range(N_EXPERTS):
        nq = n_chunks(cnt_ref[i, e])
        rcol = rank[:, e:e + 1]
        mcol = memb[:, e:e + 1]
        gcol = gates[:, e:e + 1]

        def chunk(qq, carry, slot=slot, rcol=rcol, mcol=mcol, gcol=gcol):
            y = ybuf_ref[par, pl.ds(pl.multiple_of((slot + qq) * SEL_ROWS, SEL_ROWS), SEL_ROWS), :]
            sel = jnp.where((cidx + (qq * SEL_ROWS).astype(F32) == rcol) & (mcol > 0.0), 1.0, 0.0).astype(BF16)
            yh = y.astype(BF16)
            yl = (y - yh.astype(F32)).astype(BF16)
            got = jnp.dot(jnp.concatenate([sel, sel], axis=1), jnp.concatenate([yh, yl], axis=0),
                          preferred_element_type=F32)
            acc_ref[...] += gcol * got
            return carry

        lax.fori_loop(0, nq, chunk, 0)
        slot = slot + nq
    out_ref[...] = acc_ref[...]


def _moe_combine(base, cnt, h, gates, memb, tri, ys):
    n = h.shape[0]
    tm = MOE_TM
    max_chunks = 2 * tm // SEL_ROWS + N_EXPERTS
    grid_spec = pltpu.PrefetchScalarGridSpec(
        num_scalar_prefetch=2,
        grid=(n // tm,),
        in_specs=[pl.BlockSpec((tm, D_MODEL), lambda i, b, c: (i, 0)),
                  pl.BlockSpec((tm, LANES), lambda i, b, c: (i, 0)),
                  pl.BlockSpec((tm, LANES), lambda i, b, c: (i, 0)),
                  pl.BlockSpec((tm, tm), lambda i, b, c: (0, 0)),
                  pl.BlockSpec(memory_space=pl.ANY)],
        out_specs=pl.BlockSpec((tm, D_MODEL), lambda i, b, c: (i, 0)),
        scratch_shapes=[pltpu.VMEM((2, max_chunks * SEL_ROWS, D_MODEL), F32), pltpu.VMEM((tm, D_MODEL), F32),
                        pltpu.SemaphoreType.DMA((2,))],
    )
    return pl.pallas_call(
        _combine_kernel,
        out_shape=jax.ShapeDtypeStruct((n, D_MODEL), F32),
        grid_spec=grid_spec,
        compiler_params=_cparams(("arbitrary",)),
        name="moe_combine",
    )(base, cnt, h, gates, memb, tri, ys)


def _moe(h, g, router, wg, wu, wd):
    n = h.shape[0]
    xs_rows = _moe_rows(n)
    n_blk = xs_rows // MOE_BLK
    v, gates, memb, cnt_f = _moe_route(h, g, router)
    cnt = cnt_f[:, 0, :N_EXPERTS].astype(jnp.int32)
    cp = _seg_pad(cnt)
    exp_rows = -(-jnp.sum(cp, axis=0) // MOE_BLK) * MOE_BLK
    exp_end = jnp.cumsum(exp_rows)
    exp_start = exp_end - exp_rows
    base = (exp_start[None, :] + jnp.cumsum(cp, axis=0) - cp).astype(jnp.int32)
    blk_row = jnp.arange(n_blk + 1, dtype=jnp.int32) * MOE_BLK
    bval = (blk_row < exp_end[-1]).astype(jnp.int32)
    bexp = jnp.minimum(jnp.sum((blk_row[:, None] >= exp_end[None, :]).astype(jnp.int32), axis=1), N_EXPERTS - 1)
    last_valid = jnp.maximum(jnp.sum(bval) - 1, 0)
    bexp = jnp.where(bval > 0, bexp, bexp[last_valid]).astype(jnp.int32)
    tri = jnp.asarray(np.tril(np.ones((MOE_TM, MOE_TM), np.float32), -1), BF16)
    xs = _moe_gather(base, cnt, v, memb, tri, xs_rows)
    ys = _moe_experts(bexp, bval, xs, wg, wu, wd)
    return _moe_combine(base, cnt, h, gates, memb, tri, ys)


def _ple_final_kernel(split, h_ref, pa_ref, pb_ref, g_ref, wup_ref, wgate_ref, gf_ref, ya_ref, yb_ref):
    hn = _ple_update(h_ref[...], _two_src_load(pa_ref, pb_ref, split), g_ref, wup_ref, wgate_ref)
    y = _rms(hn, gf_ref[...])
    i = pl.program_id(0)

    @pl.when(i < split)
    def _():
        ya_ref[...] = y

    @pl.when(i >= split)
    def _():
        yb_ref[...] = y


def _ple_final(h, pa, pb, layer, g, wup, wgate, gf, tm):
    n = h.shape[0]
    split = pa.shape[1] // tm
    return pl.pallas_call(
        functools.partial(_ple_final_kernel, split),
        out_shape=[jax.ShapeDtypeStruct((pa.shape[1], D_MODEL), F32), jax.ShapeDtypeStruct((pb.shape[1], D_MODEL), F32)],
        grid=(n // tm,),
        in_specs=[pl.BlockSpec((tm, D_MODEL), lambda i: (i, 0))] + _two_src_specs(tm, PLE_DIM, split, layer)
        + [pl.BlockSpec((1, D_MODEL), lambda i: (0, 0)),
           pl.BlockSpec((PLE_DIM, D_MODEL), lambda i: (0, 0)),
           pl.BlockSpec((D_MODEL, D_MODEL), lambda i: (0, 0)),
           pl.BlockSpec((1, D_MODEL), lambda i: (0, 0))],
        out_specs=_two_src_specs(tm, D_MODEL, split),
        compiler_params=_cparams(("arbitrary",)),
        name="ple_final",
    )(h, pa, pb, g, wup, wgate, gf)


def _reorder_w_in(w):
    sizes = (GLA_KW, GLA_KW, GLA_VW, GLA_RANK, GLA_VW, RET_KW, RET_KW, RET_VW, RET_VW,
             GDN_CONV_CH, GDN_H, GDN_H, GDN_VW)
    pts = np.cumsum(np.array(sizes))[:-1].tolist()
    gq, gk, gv, glr, gr, rq, rk, rv, rr, dqkv, da, db, dz = jnp.split(w, pts, axis=1)
    pad = jnp.zeros((w.shape[0], LANES - GLA_RANK - 2 * GDN_H), w.dtype)
    return jnp.concatenate([gq, gk, gv, gr, rq, rk, rv, rr, dqkv, dz, glr, da, db, pad], axis=1)


def _lane_row(vals, offset):
    return jnp.zeros((1, LANES), F32).at[0, offset:offset + vals.shape[0]].set(vals.astype(F32))


def kernel(x_prompt, x_sample, state_gla, state_ret, state_gdn, state_gdn_conv, p_prompt, p_sample, norm_mix, w_in, gla_w_gate_up, gla_b_gate, gla_norm, ret_norm, gdn_conv, gdn_a_log, gdn_dt_bias, gdn_norm, w_out, norm_ffn, ffn_w_gate, ffn_w_up, ffn_w_down, moe_router, moe_w_gate, moe_w_up, moe_w_down, ple_w_up, ple_norm, ple_w_gate, norm_final):
    depth = w_in.shape[0]
    Bp, Tp, _ = x_prompt.shape
    Bs, Ts, _ = x_sample.shape
    n_p, n_s = Bp * Tp, Bs * Ts
    tm = TOKEN_TILE
    xp = x_prompt.reshape(n_p, D_MODEL)
    xs = x_sample.reshape(n_s, D_MODEL)
    pp = p_prompt.reshape(depth, n_p, PLE_DIM)
    ps = p_sample.reshape(depth, n_s, PLE_DIM)

    outs_p = [[], [], [], []]
    outs_s = [[], [], [], []]
    h = None
    for i in range(depth):
        w_in_k = _reorder_w_in(w_in[i].astype(BF16))
        prm = dict(
            wgu=jnp.zeros((LANES, GLA_KW), F32).at[:GLA_RANK].set(gla_w_gate_up[i]).astype(BF16),
            gbias=gla_b_gate[i].reshape(1, GLA_KW).astype(F32),
            alog=_lane_row(gdn_a_log[i], SM_DA),
            dtb=_lane_row(gdn_dt_bias[i], SM_DA),
            glan=jnp.tile(gla_norm[i].astype(F32), GLA_H).reshape(1, GLA_VW),
            retn=jnp.tile(ret_norm[i].astype(F32), RET_H).reshape(1, RET_VW),
            gdnn=gdn_norm[i].astype(F32).reshape(1, GDN_DV),
            convw=gdn_conv[i].astype(F32),
        )
        if i == 0:
            z, h = _norm_proj2(xp, xs, norm_mix[i].reshape(1, D_MODEL), w_in_k, tm)
        else:
            z, h = _ple_norm_proj(h, pp, ps, i - 1, ple_norm[i - 1].reshape(1, D_MODEL),
                                  ple_w_up[i - 1].astype(BF16), ple_w_gate[i - 1].astype(BF16),
                                  norm_mix[i].reshape(1, D_MODEL), w_in_k, tm)
        o_p, gla_p, ret_p, gdn_p, conv_p = _mixer(
            z, 0, Bp, Tp, 0, True,
            jnp.zeros((Bp, GLA_H, GLA_DK, GLA_DV), F32), jnp.zeros((Bp, RET_H, RET_DK, RET_DV), F32),
            jnp.zeros((Bp, GDN_H, GDN_DK, GDN_DV), F32), jnp.zeros((Bp, GDN_CONV_W - 1, GDN_CONV_CH), F32), prm)
        o_s, gla_s, ret_s, gdn_s, conv_s = _mixer(
            z, n_p, Bs, Ts, PAST_LEN, False,
            state_gla[i].astype(F32), state_ret[i].astype(F32),
            state_gdn[i].astype(F32), state_gdn_conv[i].astype(F32), prm)
        for lst, val in zip(outs_p, (gla_p, ret_p, gdn_p, conv_p)):
            lst.append(val)
        for lst, val in zip(outs_s, (gla_s, ret_s, gdn_s, conv_s)):
            lst.append(val)
        wo = w_out[i].astype(BF16)
        j = i // 2
        if i % 2 == 0:
            h = _ffn(h, o_p, o_s, wo, norm_ffn[i].reshape(1, D_MODEL), ffn_w_gate[j].astype(BF16),
                     ffn_w_up[j].astype(BF16), ffn_w_down[j].astype(BF16), tm, FF_TILE)
        else:
            router = jnp.zeros((D_MODEL, LANES), F32).at[:, :N_EXPERTS].set(moe_router[j].astype(F32))
            h = _out_proj(h, o_p, o_s, wo, tm)
            h = _moe(h, norm_ffn[i].reshape(1, D_MODEL), router, moe_w_gate[j].astype(BF16),
                     moe_w_up[j].astype(BF16), moe_w_down[j].astype(BF16))

    last = depth - 1
    y_p, y_s = _ple_final(h, pp, ps, last, ple_norm[last].reshape(1, D_MODEL), ple_w_up[last].astype(BF16),
                          ple_w_gate[last].astype(BF16), norm_final.reshape(1, D_MODEL), tm)
    return (y_p.reshape(Bp, Tp, D_MODEL), y_s.reshape(Bs, Ts, D_MODEL),
            jnp.stack(outs_p[0]), jnp.stack(outs_p[1]), jnp.stack(outs_p[2]), jnp.stack(outs_p[3]),
            jnp.stack(outs_s[0]), jnp.stack(outs_s[1]), jnp.stack(outs_s[2]), jnp.stack(outs_s[3]))
```

```python
import functools
import math

import jax
import jax.numpy as jnp
import numpy as np
from jax import lax
from jax.experimental import pallas as pl
from jax.experimental.pallas import tpu as pltpu

F32 = jnp.float32
BF16 = jnp.bfloat16

D_MODEL = 1024
CHUNK = 64
PLE_DIM = 256
EPS = 1e-6
PAST_LEN = 4096

GLA_H, GLA_DK, GLA_DV, GLA_RANK, GLA_TAU = 4, 32, 64, 16, 16.0
RET_H, RET_DK, RET_DV = 4, 32, 64
ROPE_BASE = 10000.0
GDN_H, GDN_DK, GDN_DV, GDN_CONV_W = 4, 128, 128, 4
GLA_KW, GLA_VW = GLA_H * GLA_DK, GLA_H * GLA_DV
RET_KW, RET_VW = RET_H * RET_DK, RET_H * RET_DV
GDN_KW, GDN_VW = GDN_H * GDN_DK, GDN_H * GDN_DV
GDN_CONV_CH = 2 * GDN_KW + GDN_VW
D_FF = 3584
N_EXPERTS = 8

LANES = 128

Z_GQ, Z_GK, Z_GV, Z_GR = 0, 128, 256, 512
Z_RQ, Z_RK, Z_RV, Z_RR = 768, 896, 1024, 1280
Z_CONV = 1536
Z_DZ = Z_CONV + GDN_CONV_CH
Z_SMALL = Z_DZ + GDN_VW
Z_WIDTH = Z_SMALL + LANES
SM_DA, SM_DB = GLA_RANK, GLA_RANK + GDN_H

MIX_ROWS = 256
GDN_SUB = 128
TOKEN_TILE = 512
VMEM_LIMIT = 56 * 1024 * 1024


def _cparams(sem):
    return pltpu.CompilerParams(dimension_semantics=sem, vmem_limit_bytes=VMEM_LIMIT)


def _rms(x, g):
    return x * lax.rsqrt(jnp.mean(x * x, axis=-1, keepdims=True) + EPS) * g


def _bdot(a, b):
    return jnp.dot(a.astype(BF16), b.astype(BF16), preferred_element_type=F32)


def _bdot_t(a, b):
    return lax.dot_general(a.astype(BF16), b.astype(BF16), (((1,), (1,)), ((), ())),
                           preferred_element_type=F32)


def _bdot_tl(a, b):
    return lax.dot_general(a.astype(BF16), b.astype(BF16), (((0,), (0,)), ((), ())),
                           preferred_element_type=F32)


def _silu(x):
    return x * jax.nn.sigmoid(x)


def _split3(a):
    hi = a.astype(BF16)
    r1 = a - hi.astype(F32)
    mid = r1.astype(BF16)
    lo = (r1 - mid.astype(F32)).astype(BF16)
    return [hi, mid, lo]


def _sum3(x):
    return x[:, 0:LANES] + x[:, LANES:2 * LANES] + x[:, 2 * LANES:3 * LANES]


def _two_src_specs(tm, width, split, layer=None):
    if layer is None:
        return [pl.BlockSpec((tm, width), lambda i, *_: (jnp.minimum(i, split - 1), 0)),
                pl.BlockSpec((tm, width), lambda i, *_: (jnp.maximum(i - split, 0), 0))]
    return [pl.BlockSpec((None, tm, width), lambda i, *_: (layer, jnp.minimum(i, split - 1), 0)),
            pl.BlockSpec((None, tm, width), lambda i, *_: (layer, jnp.maximum(i - split, 0), 0))]


def _two_src_load(a_ref, b_ref, split):
    return jnp.where(pl.program_id(0) < split, a_ref[...], b_ref[...])


def _norm_proj2_kernel(split, xa_ref, xb_ref, g_ref, w_ref, z_ref, h_ref):
    x = _two_src_load(xa_ref, xb_ref, split)
    h_ref[...] = x
    z_ref[...] = jnp.dot(_rms(x, g_ref[...]).astype(BF16), w_ref[...], preferred_element_type=F32)


def _norm_proj2(xa, xb, g, w, tm):
    n = xa.shape[0] + xb.shape[0]
    split = xa.shape[0] // tm
    return pl.pallas_call(
        functools.partial(_norm_proj2_kernel, split),
        out_shape=[jax.ShapeDtypeStruct((n, w.shape[1]), F32), jax.ShapeDtypeStruct((n, D_MODEL), F32)],
        grid=(n // tm,),
        in_specs=_two_src_specs(tm, D_MODEL, split) + [pl.BlockSpec((1, D_MODEL), lambda i: (0, 0)),
                                                      pl.BlockSpec(w.shape, lambda i: (0, 0))],
        out_specs=[pl.BlockSpec((tm, w.shape[1]), lambda i: (i, 0)),
                   pl.BlockSpec((tm, D_MODEL), lambda i: (i, 0))],
        compiler_params=_cparams(("parallel",)),
        name="norm_proj_first",
    )(xa, xb, g, w)


def _ple_update(h, p, g_ref, wup_ref, wgate_ref):
    up = jnp.dot(p.astype(BF16), wup_ref[...], preferred_element_type=F32)
    gt = jnp.dot(_rms(h, g_ref[...]).astype(BF16), wgate_ref[...], preferred_element_type=F32)
    return h + up * jax.nn.sigmoid(gt)


def _ple_norm_proj_kernel(split, h_ref, pa_ref, pb_ref, gp_ref, wup_ref, wgate_ref, g_ref, w_ref, z_ref, hn_ref):
    hn = _ple_update(h_ref[...], _two_src_load(pa_ref, pb_ref, split), gp_ref, wup_ref, wgate_ref)
    hn_ref[...] = hn
    z_ref[...] = jnp.dot(_rms(hn, g_ref[...]).astype(BF16), w_ref[...], preferred_element_type=F32)


def _ple_norm_proj(h, pa, pb, layer, gp, wup, wgate, g, w, tm):
    n = h.shape[0]
    split = pa.shape[1] // tm
    return pl.pallas_call(
        functools.partial(_ple_norm_proj_kernel, split),
        out_shape=[jax.ShapeDtypeStruct((n, w.shape[1]), F32), jax.ShapeDtypeStruct((n, D_MODEL), F32)],
        grid=(n // tm,),
        in_specs=[pl.BlockSpec((tm, D_MODEL), lambda i: (i, 0))] + _two_src_specs(tm, PLE_DIM, split, layer)
        + [pl.BlockSpec((1, D_MODEL), lambda i: (0, 0)),
           pl.BlockSpec((PLE_DIM, D_MODEL), lambda i: (0, 0)),
           pl.BlockSpec((D_MODEL, D_MODEL), lambda i: (0, 0)),
           pl.BlockSpec((1, D_MODEL), lambda i: (0, 0)),
           pl.BlockSpec(w.shape, lambda i: (0, 0))],
        out_specs=[pl.BlockSpec((tm, w.shape[1]), lambda i: (i, 0)),
                   pl.BlockSpec((tm, D_MODEL), lambda i: (i, 0))],
        compiler_params=_cparams(("parallel",)),
        name="ple_norm_proj",
    )(h, pa, pb, gp, wup, wgate, g, w)


def _mixer_kernel(L, n_seg, seq,
                  z_ref, cos_ref, sin_ref, retd_ref, reteb_ref, retebl_ref, retsdec_ref,
                  sgla0_ref, sret0_ref, sgdn0_ref, conv0_ref,
                  wgu_ref, gbias_ref, alog_ref, dtb_ref, glan_ref, retn_ref, gdnn_ref, convw_ref,
                  tri_ref, de_ref, lmask_ref, gn64_ref, blkmask_ref, hmk_ref, hmv_ref,
                  o_ref, sgla_out_ref, sret_out_ref, sgdn_ref, convo_ref,
                  xpad_ref, sgla_ref, sret_ref):
    R = L * n_seg
    n_lev = int(math.log2(L))
    c = pl.program_id(1)
    n_state = sgla_ref.shape[0]

    def load_blockdiag(src_ref, dst_ref, dk, dv):
        for s in range(n_state):
            for h in range(GLA_H):
                pieces = []
                if h > 0:
                    pieces.append(jnp.zeros((dk, h * dv), F32))
                pieces.append(src_ref[s, h])
                if h < GLA_H - 1:
                    pieces.append(jnp.zeros((dk, (GLA_H - 1 - h) * dv), F32))
                dst_ref[s, h * dk:(h + 1) * dk, :] = jnp.concatenate(pieces, axis=1)

    def store_blockdiag(src_ref, dst_ref, dk, dv):
        for s in range(n_state):
            for h in range(GLA_H):
                dst_ref[s, h] = src_ref[s, h * dk:(h + 1) * dk, h * dv:(h + 1) * dv]

    @pl.when(c == 0)
    def _():
        load_blockdiag(sgla0_ref, sgla_ref, GLA_DK, GLA_DV)
        load_blockdiag(sret0_ref, sret_ref, RET_DK, RET_DV)
        sgdn_ref[...] = sgdn0_ref[...]
        convo_ref[...] = conv0_ref[...]

    blkmask = blkmask_ref[...]
    hmk = hmk_ref[...]
    hmv = hmv_ref[...]
    small = z_ref[:, Z_SMALL:Z_SMALL + LANES]

    def slot(j):
        return 0 if seq else j

    def rows(j):
        return slice(j * L, (j + 1) * L)

    la = jax.nn.log_sigmoid(_bdot(small, wgu_ref[...]) + gbias_ref[...]) * (1.0 / GLA_TAU)
    g_all = -jnp.exp(alog_ref[...]) * jax.nn.softplus(small + dtb_ref[...])
    beta_all = jax.nn.sigmoid(small)
    la_parts = _split3(la)
    la3 = jnp.concatenate(la_parts, axis=1)
    g3 = jnp.concatenate(_split3(g_all), axis=1)
    cs = jnp.dot(tri_ref[...], jnp.concatenate([la3, g3], axis=1), preferred_element_type=F32)
    b = _sum3(cs[:, 0:3 * LANES])
    bg = _sum3(cs[:, 3 * LANES:6 * LANES])
    de = jnp.dot(de_ref[...], jnp.concatenate(la_parts[0:2], axis=1), preferred_element_type=F32)
    ede = jnp.exp(de[:, 0:LANES] + de[:, LANES:2 * LANES])

    def group_norm_gate(o, g, gate):
        sq = o * o
        hi = sq.astype(BF16)
        lo = (sq - hi.astype(F32)).astype(BF16)
        gn = gn64_ref[...]
        ms = (jnp.dot(hi, gn, preferred_element_type=F32) + jnp.dot(lo, gn, preferred_element_type=F32))
        return o * lax.rsqrt(ms + EPS) * g * _silu(gate)

    def intra_scores(q_list, kblk_list, masks):
        att = None
        for qm, kb, mk in zip(q_list, kblk_list, masks):
            term = mk * _bdot_t(qm, kb)
            att = term if att is None else att + term
        return att

    q = z_ref[:, Z_GQ:Z_GQ + GLA_KW] * (GLA_DK ** -0.5)
    k = z_ref[:, Z_GK:Z_GK + GLA_KW]
    v = z_ref[:, Z_GV:Z_GV + GLA_VW]
    qe = q * jnp.exp(b)
    b_t = b.T
    pending = []
    state = {"gla": sgla_ref[0] if seq else None, "ret": sret_ref[0] if seq else None}
    o_gla_parts, o_ret_parts = [], []

    def gla_chunk(j, q=q, k=k, v=v):
        r = rows(j)
        qj, kj, vj = q[r], k[r], v[r]
        kblk0 = jnp.concatenate([kj] * GLA_H, axis=0) * hmk
        q_list, kb_list, masks = [], [], []
        for lv in range(n_lev):
            elv = ede[lv * R + j * L:lv * R + (j + 1) * L]
            q_list.append(qj * elv)
            kb_list.append(kblk0 * jnp.concatenate([elv] * GLA_H, axis=0))
            masks.append(lmask_ref[lv])
        q_list.append(qj)
        kb_list.append(kblk0)
        masks.append(lmask_ref[n_lev])
        att = intra_scores(q_list, kb_list, masks)
        vblk = jnp.concatenate([vj] * GLA_H, axis=0) * hmv
        last = (j + 1) * L - 1
        bl = b[last:last + 1, :]
        kv = blkmask * _bdot_tl(kj * jnp.exp(bl - b[r]), vj)
        decc = jnp.exp(b_t[:, last:last + 1])
        s_gla = state["gla"] if seq else sgla_ref[j]
        o_gla_parts.append(_bdot(att, vblk) + _bdot(qe[r], s_gla))
        s_gla = s_gla * decc + kv
        if seq:
            state["gla"] = s_gla
        else:
            sgla_ref[j] = s_gla

    def gla_finish():
        if seq:
            sgla_ref[0] = state["gla"]
        o_gla = jnp.concatenate(o_gla_parts, axis=0)
        o_ref[:, 0:GLA_VW] = group_norm_gate(o_gla, glan_ref[...], z_ref[:, Z_GR:Z_GR + GLA_VW]).astype(o_ref.dtype)

    cos = cos_ref[...]
    sin = sin_ref[...]
    lane = lax.broadcasted_iota(jnp.int32, (R, LANES), 1)
    first_half = (lane % RET_DK) < (RET_DK // 2)

    def rot(xx):
        sw = jnp.where(first_half, pltpu.roll(xx, LANES - RET_DK // 2, 1), pltpu.roll(xx, RET_DK // 2, 1))
        return xx * cos + sw * sin

    rq = rot(z_ref[:, Z_RQ:Z_RQ + RET_KW]) * (RET_DK ** -0.5)
    rk = rot(z_ref[:, Z_RK:Z_RK + RET_KW])
    rv = z_ref[:, Z_RV:Z_RV + RET_VW]
    retd = retd_ref[...]
    reteb = reteb_ref[...]
    retebl = retebl_ref[...]
    retsdec = retsdec_ref[...]

    def ret_chunk(j):
        r = rows(j)
        qj, kj, vj = rq[r], rk[r], rv[r]
        kblk = jnp.concatenate([kj] * RET_H, axis=0) * hmk
        att = _bdot_t(qj, kblk) * retd
        vblk = jnp.concatenate([vj] * RET_H, axis=0) * hmv
        kv = blkmask * _bdot_tl(kj * retebl, vj)
        s_ret = state["ret"] if seq else sret_ref[j]
        o_ret_parts.append(_bdot(att, vblk) + _bdot(qj * reteb, s_ret))
        s_ret = s_ret * retsdec + kv
        if seq:
            state["ret"] = s_ret
        else:
            sret_ref[j] = s_ret

    def ret_finish():
        if seq:
            sret_ref[0] = state["ret"]
        o_ret = jnp.concatenate(o_ret_parts, axis=0)
        o_ref[:, GLA_VW:GLA_VW + RET_VW] = group_norm_gate(
            o_ret, retn_ref[...], z_ref[:, Z_RR:Z_RR + RET_VW]).astype(o_ref.dtype)

    for j in range(n_seg):
        pending.append(functools.partial(gla_chunk, j))
        pending.append(functools.partial(ret_chunk, j))

    n_cs = 1 if seq else n_seg
    Lc = R // n_cs
    cw = convw_ref[...]
    xin = z_ref[:, Z_CONV:Z_CONV + GDN_CONV_CH]
    c_parts = []
    for s in range(n_cs):
        xs = xin[s * Lc:(s + 1) * Lc]
        xpad_ref[s, 8 - (GDN_CONV_W - 1):8, :] = convo_ref[s]
        xpad_ref[s, 8:8 + Lc, :] = xs
        convo_ref[s] = xs[Lc - (GDN_CONV_W - 1):Lc, :]
        cacc = xs * cw[GDN_CONV_W - 1:GDN_CONV_W, :]
        for jj in range(GDN_CONV_W - 1):
            sh = GDN_CONV_W - 1 - jj
            cacc = cacc + xpad_ref[s, 8 - sh:8 - sh + Lc, :] * cw[jj:jj + 1, :]
        c_parts.append(cacc)
    cact = _silu(c_parts[0] if n_cs == 1 else jnp.concatenate(c_parts, axis=0))

    n_sub = R // GDN_SUB
    row = lax.broadcasted_iota(jnp.int32, (GDN_SUB, GDN_SUB), 0)
    col = lax.broadcasted_iota(jnp.int32, (GDN_SUB, GDN_SUB), 1)
    same = lax.shift_right_logical(row, n_lev) == lax.shift_right_logical(col, n_lev)
    tril = same & (row >= col)
    strict = same & (row > col)
    eye_f = (row == col).astype(F32)
    bg_t = bg.T
    beta_t = beta_all.T
    gdnn = gdnn_ref[...]

    def l2n(xx):
        return xx * lax.rsqrt(jnp.sum(xx * xx, axis=-1, keepdims=True) + EPS)

    HS = range(GDN_H)
    PS = [(h, sp) for h in HS for sp in range(n_sub)]

    def sub(a, sp):
        return a[sp * GDN_SUB:(sp + 1) * GDN_SUB]

    qh = [l2n(cact[:, h * GDN_DK:(h + 1) * GDN_DK]) * (GDN_DK ** -0.5) for h in HS]
    kh = [l2n(cact[:, GDN_KW + h * GDN_DK:GDN_KW + (h + 1) * GDN_DK]) for h in HS]
    vh = [cact[:, 2 * GDN_KW + h * GDN_DV:2 * GDN_KW + (h + 1) * GDN_DV] for h in HS]
    bcol = [bg[:, SM_DA + h:SM_DA + h + 1] for h in HS]
    beta_col = [beta_all[:, SM_DB + h:SM_DB + h + 1] for h in HS]
    ebc = [jnp.exp(bcol[h]) for h in HS]
    khb = [kh[h].astype(BF16) for h in HS]
    beta_row = {(h, sp): beta_t[SM_DB + h:SM_DB + h + 1, sp * GDN_SUB:(sp + 1) * GDN_SUB] for h, sp in PS}
    dec = {(h, sp): jnp.exp(jnp.where(
        tril, sub(bcol[h], sp) - bg_t[SM_DA + h:SM_DA + h + 1, sp * GDN_SUB:(sp + 1) * GDN_SUB], -jnp.inf))
        for h, sp in PS}
    m = {(h, sp): _bdot_t(sub(khb[h], sp), sub(khb[h], sp)) * jnp.where(strict, dec[h, sp], 0.0) * beta_row[h, sp]
         for h, sp in PS}
    tinv = {p: eye_f - m[p] for p in PS}
    pw = {p: m[p].astype(BF16) for p in PS}
    per_stage = -(-len(pending) // (n_lev - 1))
    span = 2
    while span < L:
        pw = {p: jnp.dot(pw[p], pw[p], preferred_element_type=F32).astype(BF16) for p in PS}
        tinv = {p: tinv[p] + jnp.dot(tinv[p].astype(BF16), pw[p], preferred_element_type=F32) for p in PS}
        for thunk in pending[:per_stage]:
            thunk()
        del pending[:per_stage]
        span *= 2
    assert not pending
    gla_finish()
    ret_finish()
    store_blockdiag(sgla_ref, sgla_out_ref, GLA_DK, GLA_DV)
    store_blockdiag(sret_ref, sret_out_ref, RET_DK, RET_DV)
    x1 = {(h, sp): _bdot(tinv[h, sp], jnp.concatenate([sub(vh[h], sp), sub(ebc[h] * kh[h], sp)], axis=1)).astype(BF16)
          for h, sp in PS}
    qk = {(h, sp): _bdot_t(sub(qh[h], sp), sub(khb[h], sp)) * dec[h, sp] * beta_row[h, sp] for h, sp in PS}
    x2 = {p: _bdot(qk[p], x1[p]) for p in PS}
    qeff = {(h, sp): sub(ebc[h] * qh[h], sp) - x2[h, sp][:, GDN_DV:2 * GDN_DV] for h, sp in PS}
    x3 = [[None] * n_seg for h in HS]
    blg = [[None] * n_seg for h in HS]
    for j in range(n_seg):
        r = rows(j)
        sp, lo_r = divmod(j * L, GDN_SUB)
        last = (j + 1) * L - 1
        for h in HS:
            blg[h][j] = bcol[h][last:last + 1, :]
            kt = kh[h][r] * (jnp.exp(blg[h][j] - bcol[h][r]) * beta_col[h][r])
            x3[h][j] = _bdot_tl(kt, x1[h, sp][lo_r:lo_r + L])
    s_h = [sgdn_ref[0, h] if seq else None for h in HS]
    o_parts = [[] for h in HS]
    for j in range(n_seg):
        sp, lo_r = divmod(j * L, GDN_SUB)
        for h in HS:
            if not seq:
                s_h[h] = sgdn_ref[j, h]
            y = _bdot(jnp.concatenate([qeff[h, sp][lo_r:lo_r + L], x3[h][j][:, GDN_DV:2 * GDN_DV]], axis=0),
                      s_h[h])
            o_parts[h].append(x2[h, sp][lo_r:lo_r + L, 0:GDN_DV] + y[0:L])
            s_h[h] = jnp.exp(blg[h][j]) * s_h[h] - y[L:L + GDN_DK] + x3[h][j][:, 0:GDN_DV]
            if not seq:
                sgdn_ref[j, h] = s_h[h]
    for h in HS:
        if seq:
            sgdn_ref[0, h] = s_h[h]
        oh = jnp.concatenate(o_parts[h], axis=0)
        oh = _rms(oh, gdnn) * _silu(z_ref[:, Z_DZ + h * GDN_DV:Z_DZ + (h + 1) * GDN_DV])
        lo = GLA_VW + RET_VW + h * GDN_DV
        o_ref[:, lo:lo + GDN_DV] = oh.astype(o_ref.dtype)


def _head_block_mask(rows_per_head, cols_per_head, heads):
    r = np.arange(rows_per_head * heads)[:, None] // rows_per_head
    c = np.arange(cols_per_head * heads)[None, :] // cols_per_head
    return (r == c).astype(np.float32)


def _pack_consts(L, n_seg):
    R = L * n_seg
    t = np.arange(R)
    tl = t % L
    base = t - tl
    jj = np.arange(R)[None, :]
    tri = ((jj // L) == (t[:, None] // L)) & (jj <= t[:, None])
    tt = np.arange(L)[:, None]
    ss = np.arange(L)[None, :]
    de_list, masks = [], []
    m = L // 2
    while m >= 1:
        second = (tl // m) % 2 == 1
        ref = base + (tl // (2 * m)) * 2 * m + m - 1
        d_m = second[:, None] & (jj > ref[:, None]) & (jj <= t[:, None])
        e_m = (~second)[:, None] & (jj > t[:, None]) & (jj <= ref[:, None])
        de_list.append(d_m | e_m)
        mk = (tt // (2 * m) == ss // (2 * m)) & ((tt // m) % 2 == 1) & ((ss // m) % 2 == 0)
        masks.append(np.tile(mk, (1, GLA_H)))
        m //= 2
    masks.append(np.tile(np.eye(L, dtype=bool), (1, GLA_H)))
    de = np.concatenate(de_list, axis=0)
    return (jnp.asarray(tri, BF16), jnp.asarray(de, BF16), jnp.asarray(np.stack(masks), F32))


def _mixer(zf, row0, B, T, pos0, seq, sgla0, sret0, sgdn0, conv0, prm):
    blk0 = row0 // MIX_ROWS
    if seq:
        L = CHUNK
        n_seg = MIX_ROWS // L
        grid = (B, T // MIX_ROWS)
        n_state = 1
    else:
        L = T
        n_seg = MIX_ROWS // L
        grid = (B // n_seg, 1)
        n_state = n_seg
    R = MIX_ROWS
    steps_per_b = grid[1]
    half = RET_DK // 2
    inv = ROPE_BASE ** (-jnp.arange(half, dtype=F32) / half)
    ang = (pos0 + jnp.arange(T, dtype=jnp.int32)).astype(F32)[:, None] * inv[None, :]
    cos_h = jnp.concatenate([jnp.cos(ang), jnp.cos(ang)], axis=1)
    sin_h = jnp.concatenate([-jnp.sin(ang), jnp.sin(ang)], axis=1)
    cos_t = jnp.tile(cos_h, (1, RET_H))
    sin_t = jnp.tile(sin_h, (1, RET_H))
    if not seq:
        cos_t = jnp.tile(cos_t, (n_seg, 1))
        sin_t = jnp.tile(sin_t, (n_seg, 1))
    log_gamma = jnp.log(1.0 - jnp.exp2(-5.0 - jnp.arange(RET_H, dtype=F32)))
    tpos = jnp.arange(L, dtype=F32)
    bret = (tpos[:, None] + 1.0) * log_gamma[None, :]
    dmat = tpos[:, None] - tpos[None, :]
    retd = jnp.where(dmat[None] >= 0, jnp.exp(dmat[None] * log_gamma[:, None, None]), 0.0)
    retd = jnp.transpose(retd, (1, 0, 2)).reshape(L, RET_H * L)
    reteb = jnp.repeat(jnp.exp(bret), RET_DK, axis=1)
    retebl = jnp.repeat(jnp.exp(bret[L - 1:L] - bret), RET_DK, axis=1)
    retsdec = jnp.broadcast_to(jnp.repeat(jnp.exp(bret[L - 1]), RET_DK)[:, None], (RET_KW, RET_VW))

    tri, de, lmask = _pack_consts(L, n_seg)
    gn64 = jnp.asarray(_head_block_mask(GLA_DV, GLA_DV, GLA_H) / GLA_DV, BF16)
    blkmask = jnp.asarray(_head_block_mask(GLA_DK, GLA_DV, GLA_H), F32)
    hmk = jnp.asarray(_head_block_mask(L, RET_DK, RET_H), F32)
    hmv = jnp.asarray(_head_block_mask(L, RET_DV, RET_H), F32)

    def full(a):
        nd = a.ndim
        return pl.BlockSpec(a.shape, lambda b, c: (0,) * nd)

    def per_b(a):
        nd = a.ndim
        return pl.BlockSpec((n_state,) + a.shape[1:], lambda b, c: (b,) + (0,) * (nd - 1))

    if seq:
        tab_spec = pl.BlockSpec((R, LANES), lambda b, c: (c, 0))
    else:
        tab_spec = pl.BlockSpec((R, LANES), lambda b, c: (0, 0))
    consts = [retd, reteb, retebl, retsdec]
    states = [sgla0, sret0, sgdn0, conv0]
    params = [prm["wgu"], prm["gbias"], prm["alog"], prm["dtb"], prm["glan"], prm["retn"], prm["gdnn"],
              prm["convw"], tri, de, lmask, gn64, blkmask, hmk, hmv]
    in_specs = ([pl.BlockSpec((R, Z_WIDTH), lambda b, c: (blk0 + b * steps_per_b + c, 0)), tab_spec, tab_spec]
                + [full(a) for a in consts] + [per_b(a) for a in states] + [full(a) for a in params])
    out_shape = [jax.ShapeDtypeStruct((B * T, D_MODEL), BF16)] + [
        jax.ShapeDtypeStruct(a.shape, F32) for a in states]
    out_specs = [pl.BlockSpec((R, D_MODEL), lambda b, c: (b * steps_per_b + c, 0))] + [per_b(a) for a in states]
    n_cs = 1 if seq else n_seg
    return pl.pallas_call(
        functools.partial(_mixer_kernel, L, n_seg, seq),
        out_shape=out_shape,
        grid=grid,
        in_specs=in_specs,
        out_specs=out_specs,
        scratch_shapes=[pltpu.VMEM((n_cs, R // n_cs + 8, GDN_CONV_CH), F32),
                        pltpu.VMEM((n_state, GLA_KW, GLA_VW), F32), pltpu.VMEM((n_state, RET_KW, RET_VW), F32)],
        compiler_params=_cparams(("parallel", "arbitrary")),
        name="mixer",
    )(zf, cos_t, sin_t, *consts, *states, *params)


def _out_proj_kernel(split, h_ref, oa_ref, ob_ref, w_ref, out_ref):
    o = _two_src_load(oa_ref, ob_ref, split)
    out_ref[...] = h_ref[...] + jnp.dot(o, w_ref[...], preferred_element_type=F32)


def _out_proj(h, oa, ob, w, tm):
    n = h.shape[0]
    split = oa.shape[0] // tm
    return pl.pallas_call(
        functools.partial(_out_proj_kernel, split),
        out_shape=jax.ShapeDtypeStruct((n, D_MODEL), F32),
        grid=(n // tm,),
        in_specs=[pl.BlockSpec((tm, D_MODEL), lambda i: (i, 0))] + _two_src_specs(tm, D_MODEL, split)
        + [pl.BlockSpec((D_MODEL, D_MODEL), lambda i: (0, 0))],
        out_specs=pl.BlockSpec((tm, D_MODEL), lambda i: (i, 0)),
        compiler_params=_cparams(("parallel",)),
        name="out_proj",
    )(h, oa, ob, w)


FF_TILE = 1792
FF_SUB = 256


def _swiglu_partial(v, wg_ref, wu_ref, wd_ref):
    parts = []
    tf = wg_ref.shape[1]
    for c0 in range(0, tf, FF_SUB):
        c1 = min(c0 + FF_SUB, tf)
        a = jnp.dot(v, wg_ref[:, c0:c1], preferred_element_type=F32)
        u = jnp.dot(v, wu_ref[:, c0:c1], preferred_element_type=F32)
        parts.append((_silu(a) * u).astype(BF16))
    return jnp.dot(jnp.concatenate(parts, axis=1), wd_ref[...], preferred_element_type=F32)


def _ffn_kernel(split, h_ref, oa_ref, ob_ref, wo_ref, g_ref, wg_ref, wu_ref, wd_ref, out_ref, v_ref, acc_ref):
    f = pl.program_id(1)

    @pl.when(f == 0)
    def _():
        h2 = h_ref[...] + jnp.dot(_two_src_load(oa_ref, ob_ref, split), wo_ref[...], preferred_element_type=F32)
        v_ref[...] = _rms(h2, g_ref[...]).astype(BF16)
        acc_ref[...] = h2

    acc_ref[...] += _swiglu_partial(v_ref[...], wg_ref, wu_ref, wd_ref)

    @pl.when(f == pl.num_programs(1) - 1)
    def _():
        out_ref[...] = acc_ref[...]


def _ffn(h, oa, ob, wo, g, wg, wu, wd, tm, tf):
    n = h.shape[0]
    split = oa.shape[0] // tm
    return pl.pallas_call(
        functools.partial(_ffn_kernel, split),
        out_shape=jax.ShapeDtypeStruct((n, D_MODEL), F32),
        grid=(n // tm, D_FF // tf),
        in_specs=[pl.BlockSpec((tm, D_MODEL), lambda i, f: (i, 0))] + _two_src_specs(tm, D_MODEL, split)
        + [pl.BlockSpec((D_MODEL, D_MODEL), lambda i, f: (0, 0)),
           pl.BlockSpec((1, D_MODEL), lambda i, f: (0, 0)),
           pl.BlockSpec((D_MODEL, tf), lambda i, f: (0, f)),
           pl.BlockSpec((D_MODEL, tf), lambda i, f: (0, f)),
           pl.BlockSpec((tf, D_MODEL), lambda i, f: (f, 0))],
        out_specs=pl.BlockSpec((tm, D_MODEL), lambda i, f: (i, 0)),
        scratch_shapes=[pltpu.VMEM((tm, D_MODEL), BF16), pltpu.VMEM((tm, D_MODEL), F32)],
        compiler_params=_cparams(("parallel", "arbitrary")),
        name="ffn",
    )(h, oa, ob, wo, g, wg, wu, wd)


MOE_TM = 512
MOE_BLK = 512
MOE_TF = FF_TILE
SEG_ALIGN = 16
SEL_ROWS = 128


def _moe_rows(n):
    n_tiles = n // MOE_TM
    bound = 2 * n + n_tiles * N_EXPERTS * (SEG_ALIGN - 1) + N_EXPERTS * (MOE_BLK - 1)
    return -(-bound // MOE_BLK) * MOE_BLK


def _route_kernel(h_ref, g_ref, r_ref, v_ref, gate_ref, memb_ref, cnt_ref):
    vf = _rms(h_ref[...], g_ref[...])
    v_hi = vf.astype(BF16)
    v_ref[...] = v_hi
    v_mid = (vf - v_hi.astype(F32)).astype(BF16)
    logits = jnp.dot(jnp.concatenate([v_hi, v_hi, v_mid], axis=1), r_ref[...],
                     preferred_element_type=F32)
    lane = lax.broadcasted_iota(jnp.int32, logits.shape, 1)
    neg = jnp.float32(-jnp.inf)
    lg = jnp.where(lane < N_EXPERTS, logits, neg)
    m1 = jnp.max(lg, axis=1, keepdims=True)
    i1 = jnp.min(jnp.where(lg == m1, lane, LANES), axis=1, keepdims=True)
    lg2 = jnp.where(lane == i1, neg, lg)
    m2 = jnp.max(lg2, axis=1, keepdims=True)
    i2 = jnp.min(jnp.where(lg2 == m2, lane, LANES), axis=1, keepdims=True)
    e2 = jnp.exp(m2 - m1)
    den = 1.0 + e2
    gate_ref[...] = jnp.where(lane == i1, 1.0 / den, 0.0) + jnp.where(lane == i2, e2 / den, 0.0)
    memb = jnp.where((lane == i1) | (lane == i2), 1.0, 0.0)
    memb_ref[...] = memb
    cnt_ref[...] = jnp.broadcast_to(jnp.sum(memb, axis=0, keepdims=True), cnt_ref.shape)


def _moe_route(h, g, router):
    n = h.shape[0]
    tm = MOE_TM
    r_hi = router.astype(BF16)
    r_mid = (router - r_hi.astype(F32)).astype(BF16)
    r3 = jnp.concatenate([r_hi, r_mid, r_hi], axis=0)
    return pl.pallas_call(
        _route_kernel,
        out_shape=[jax.ShapeDtypeStruct((n, D_MODEL), BF16), jax.ShapeDtypeStruct((n, LANES), F32),
                   jax.ShapeDtypeStruct((n, LANES), F32), jax.ShapeDtypeStruct((n // tm, 8, LANES), F32)],
        grid=(n // tm,),
        in_specs=[pl.BlockSpec((tm, D_MODEL), lambda i: (i, 0)),
                  pl.BlockSpec((1, D_MODEL), lambda i: (0, 0)),
                  pl.BlockSpec((3 * D_MODEL, LANES), lambda i: (0, 0))],
        out_specs=[pl.BlockSpec((tm, D_MODEL), lambda i: (i, 0)),
                   pl.BlockSpec((tm, LANES), lambda i: (i, 0)),
                   pl.BlockSpec((tm, LANES), lambda i: (i, 0)),
                   pl.BlockSpec((None, 8, LANES), lambda i: (i, 0, 0))],
        compiler_params=_cparams(("parallel",)),
        name="moe_route",
    )(h, g, r3)


def _seg_pad(c):
    return (c + (SEG_ALIGN - 1)) // SEG_ALIGN * SEG_ALIGN


def _gather_kernel(base_ref, cnt_ref, v_ref, memb_ref, tri_ref, xs_in_ref, xs_ref, stg_ref, sem, nout_ref):
    del xs_in_ref
    i = pl.program_id(0)
    n_steps = pl.num_programs(0)
    par = i % 2
    unit = SEG_ALIGN

    def unit_copy(p, src_row, dst_row):
        return pltpu.make_async_copy(stg_ref.at[p, pl.ds(src_row, unit)], xs_ref.at[pl.ds(dst_row, unit)], sem.at[p])

    def wait_units(p, count):
        def body(u, carry):
            unit_copy(p, 0, 0).wait()
            return carry
        lax.fori_loop(0, count, body, 0)

    @pl.when(i >= 2)
    def _():
        wait_units(par, nout_ref[par])

    tm = v_ref.shape[0]
    memb = memb_ref[...]
    rank = jnp.dot(tri_ref[...], memb.astype(BF16), preferred_element_type=F32)
    rank_t = rank.T[0:N_EXPERTS, :]
    is_m = memb.T[0:N_EXPERTS, :] > 0.0
    sub = lax.broadcasted_iota(jnp.int32, (N_EXPERTS, tm), 0)
    pos = rank_t
    so = jnp.int32(0)
    for e in range(N_EXPERTS):
        pos = pos + jnp.where(sub == e, so.astype(F32), 0.0)
        so = so + _seg_pad(cnt_ref[i, e])
    stg_rows = stg_ref.shape[1]
    pos_a = jnp.min(jnp.where(is_m, pos, float(stg_rows)), axis=0, keepdims=True)
    pos_b = jnp.max(jnp.where(is_m, pos, -1.0), axis=0, keepdims=True)
    ridx = lax.broadcasted_iota(jnp.int32, (stg_rows, tm), 0).astype(F32)
    sel = jnp.where((ridx == pos_a) | (ridx == pos_b), 1.0, 0.0).astype(BF16)
    stg_ref[par] = jnp.dot(sel, v_ref[...], preferred_element_type=F32).astype(BF16)

    so = jnp.int32(0)
    for e in range(N_EXPERTS):
        cp = _seg_pad(cnt_ref[i, e])
        dst = base_ref[i, e]

        def send(u, carry, so=so, dst=dst):
            unit_copy(par, pl.multiple_of(so + u * unit, unit), pl.multiple_of(dst + u * unit, unit)).start()
            return carry

        lax.fori_loop(0, cp // unit, send, 0)
        so = so + cp
    nout_ref[par] = so // unit

    @pl.when(i == n_steps - 1)
    def _():
        wait_units(par, nout_ref[par])

        @pl.when(i >= 1)
        def _():
            wait_units(1 - par, nout_ref[1 - par])


def _moe_gather(base, cnt, v, memb, tri, xs_rows):
    n = v.shape[0]
    tm = MOE_TM
    stg_rows = 2 * tm + N_EXPERTS * SEG_ALIGN
    xs0 = jnp.zeros((xs_rows, D_MODEL), BF16)
    grid_spec = pltpu.PrefetchScalarGridSpec(
        num_scalar_prefetch=2,
        grid=(n // tm,),
        in_specs=[pl.BlockSpec((tm, D_MODEL), lambda i, b, c: (i, 0)),
                  pl.BlockSpec((tm, LANES), lambda i, b, c: (i, 0)),
                  pl.BlockSpec((tm, tm), lambda i, b, c: (0, 0)),
                  pl.BlockSpec(memory_space=pl.ANY)],
        out_specs=pl.BlockSpec(memory_space=pl.ANY),
        scratch_shapes=[pltpu.VMEM((2, stg_rows, D_MODEL), BF16), pltpu.SemaphoreType.DMA((2,)),
                        pltpu.SMEM((2,), jnp.int32)],
    )
    return pl.pallas_call(
        _gather_kernel,
        out_shape=jax.ShapeDtypeStruct((xs_rows, D_MODEL), BF16),
        grid_spec=grid_spec,
        input_output_aliases={5: 0},
        compiler_params=_cparams(("arbitrary",)),
        name="moe_gather",
    )(base, cnt, v, memb, tri, xs0)


def _expert_kernel(bexp_ref, bval_ref, xs_ref, wg_ref, wu_ref, wd_ref, ys_ref, acc_ref):
    del bexp_ref
    k = pl.program_id(0)
    f = pl.program_id(1)

    @pl.when(f == 0)
    def _():
        acc_ref[...] = jnp.zeros_like(acc_ref)

    @pl.when(bval_ref[k] > 0)
    def _():
        acc_ref[...] += _swiglu_partial(xs_ref[...], wg_ref, wu_ref, wd_ref)

    @pl.when(f == pl.num_programs(1) - 1)
    def _():
        ys_ref[...] = acc_ref[...]


def _moe_experts(bexp, bval, xs, wg, wu, wd):
    n_blk = xs.shape[0] // MOE_BLK
    n_f = D_FF // MOE_TF

    def f_idx(k, f, bval):
        return jnp.where(bval[k] > 0, f, n_f - 1)

    grid_spec = pltpu.PrefetchScalarGridSpec(
        num_scalar_prefetch=2,
        grid=(n_blk + 1, n_f),
        in_specs=[pl.BlockSpec((MOE_BLK, D_MODEL), lambda k, f, be, bv: (jnp.minimum(k, n_blk - 1), 0)),
                  pl.BlockSpec((None, D_MODEL, MOE_TF), lambda k, f, be, bv: (be[k], 0, f_idx(k, f, bv))),
                  pl.BlockSpec((None, D_MODEL, MOE_TF), lambda k, f, be, bv: (be[k], 0, f_idx(k, f, bv))),
                  pl.BlockSpec((None, MOE_TF, D_MODEL), lambda k, f, be, bv: (be[k], f_idx(k, f, bv), 0))],
        out_specs=pl.BlockSpec((MOE_BLK, D_MODEL), lambda k, f, be, bv: (k, 0)),
        scratch_shapes=[pltpu.VMEM((MOE_BLK, D_MODEL), F32)],
    )
    return pl.pallas_call(
        _expert_kernel,
        out_shape=jax.ShapeDtypeStruct(((n_blk + 1) * MOE_BLK, D_MODEL), F32),
        grid_spec=grid_spec,
        compiler_params=_cparams(("arbitrary", "arbitrary")),
        name="moe_experts",
    )(bexp, bval, xs, wg, wu, wd)


def _combine_kernel(base_ref, cnt_ref, h_ref, gate_ref, memb_ref, tri_ref, ys_ref, out_ref, ybuf_ref, acc_ref, sem):
    i = pl.program_id(0)
    n_steps = pl.num_programs(0)
    par = i % 2
    tm = h_ref.shape[0]

    def chunk_copy(p, src_row, slot_row):
        return pltpu.make_async_copy(ys_ref.at[pl.ds(src_row, SEL_ROWS)], ybuf_ref.at[p, pl.ds(slot_row, SEL_ROWS)],
                                     sem.at[p])

    def n_chunks(c):
        return (c + (SEL_ROWS - 1)) // SEL_ROWS

    def fetch(step, p):
        slot = jnp.int32(0)
        for e in range(N_EXPERTS):
            nq = n_chunks(cnt_ref[step, e])
            src = base_ref[step, e]

            def body(qq, carry, slot=slot, src=src):
                chunk_copy(p, pl.multiple_of(src + qq * SEL_ROWS, SEG_ALIGN),
                           pl.multiple_of((slot + qq) * SEL_ROWS, SEL_ROWS)).start()
                return carry

            lax.fori_loop(0, nq, body, 0)
            slot = slot + nq
        return slot

    @pl.when(i == 0)
    def _():
        fetch(0, 0)

    @pl.when(i + 1 < n_steps)
    def _():
        fetch(i + 1, 1 - par)

    total = jnp.int32(0)
    for e in range(N_EXPERTS):
        total = total + n_chunks(cnt_ref[i, e])

    def wait_body(u, carry):
        chunk_copy(par, 0, 0).wait()
        return carry

    lax.fori_loop(0, total, wait_body, 0)

    memb = memb_ref[...]
    gates = gate_ref[...]
    rank = jnp.dot(tri_ref[...], memb.astype(BF16), preferred_element_type=F32)
    cidx = lax.broadcasted_iota(jnp.int32, (tm, SEL_ROWS), 1).astype(F32)
    acc_ref[...] = h_ref[...]

    slot = jnp.int32(0)
    for e in range(N_EXPERTS):
        nq = n_chunks(cnt_ref[i, e])
        rcol = rank[:, e:e + 1]
        mcol = memb[:, e:e + 1]
        gcol = gates[:, e:e + 1]

        def chunk(qq, carry, slot=slot, rcol=rcol, mcol=mcol, gcol=gcol):
            y = ybuf_ref[par, pl.ds(pl.multiple_of((slot + qq) * SEL_ROWS, SEL_ROWS), SEL_ROWS), :]
            sel = jnp.where((cidx + (qq * SEL_ROWS).astype(F32) == rcol) & (mcol > 0.0), 1.0, 0.0).astype(BF16)
            yh = y.astype(BF16)
            yl = (y - yh.astype(F32)).astype(BF16)
            got = jnp.dot(jnp.concatenate([sel, sel], axis=1), jnp.concatenate([yh, yl], axis=0),
                          preferred_element_type=F32)
            acc_ref[...] += gcol * got
            return carry

        lax.fori_loop(0, nq, chunk, 0)
        slot = slot + nq
    out_ref[...] = acc_ref[...]


def _moe_combine(base, cnt, h, gates, memb, tri, ys):
    n = h.shape[0]
    tm = MOE_TM
    max_chunks = 2 * tm // SEL_ROWS + N_EXPERTS
    grid_spec = pltpu.PrefetchScalarGridSpec(
        num_scalar_prefetch=2,
        grid=(n // tm,),
        in_specs=[pl.BlockSpec((tm, D_MODEL), lambda i, b, c: (i, 0)),
                  pl.BlockSpec((tm, LANES), lambda i, b, c: (i, 0)),
                  pl.BlockSpec((tm, LANES), lambda i, b, c: (i, 0)),
                  pl.BlockSpec((tm, tm), lambda i, b, c: (0, 0)),
                  pl.BlockSpec(memory_space=pl.ANY)],
        out_specs=pl.BlockSpec((tm, D_MODEL), lambda i, b, c: (i, 0)),
        scratch_shapes=[pltpu.VMEM((2, max_chunks * SEL_ROWS, D_MODEL), F32), pltpu.VMEM((tm, D_MODEL), F32),
                        pltpu.SemaphoreType.DMA((2,))],
    )
    return pl.pallas_call(
        _combine_kernel,
        out_shape=jax.ShapeDtypeStruct((n, D_MODEL), F32),
        grid_spec=grid_spec,
        compiler_params=_cparams(("arbitrary",)),
        name="moe_combine",
    )(base, cnt, h, gates, memb, tri, ys)


def _moe(h, g, router, wg, wu, wd):
    n = h.shape[0]
    xs_rows = _moe_rows(n)
    n_blk = xs_rows // MOE_BLK
    v, gates, memb, cnt_f = _moe_route(h, g, router)
    cnt = cnt_f[:, 0, :N_EXPERTS].astype(jnp.int32)
    cp = _seg_pad(cnt)
    exp_rows = -(-jnp.sum(cp, axis=0) // MOE_BLK) * MOE_BLK
    exp_end = jnp.cumsum(exp_rows)
    exp_start = exp_end - exp_rows
    base = (exp_start[None, :] + jnp.cumsum(cp, axis=0) - cp).astype(jnp.int32)
    blk_row = jnp.arange(n_blk + 1, dtype=jnp.int32) * MOE_BLK
    bval = (blk_row < exp_end[-1]).astype(jnp.int32)
    bexp = jnp.minimum(jnp.sum((blk_row[:, None] >= exp_end[None, :]).astype(jnp.int32), axis=1), N_EXPERTS - 1)
    last_valid = jnp.maximum(jnp.sum(bval) - 1, 0)
    bexp = jnp.where(bval > 0, bexp, bexp[last_valid]).astype(jnp.int32)
    tri = jnp.asarray(np.tril(np.ones((MOE_TM, MOE_TM), np.float32), -1), BF16)
    xs = _moe_gather(base, cnt, v, memb, tri, xs_rows)
    ys = _moe_experts(bexp, bval, xs, wg, wu, wd)
    return _moe_combine(base, cnt, h, gates, memb, tri, ys)


def _ple_final_kernel(split, h_ref, pa_ref, pb_ref, g_ref, wup_ref, wgate_ref, gf_ref, ya_ref, yb_ref):
    hn = _ple_update(h_ref[...], _two_src_load(pa_ref, pb_ref, split), g_ref, wup_ref, wgate_ref)
    y = _rms(hn, gf_ref[...])
    i = pl.program_id(0)

    @pl.when(i < split)
    def _():
        ya_ref[...] = y

    @pl.when(i >= split)
    def _():
        yb_ref[...] = y


def _ple_final(h, pa, pb, layer, g, wup, wgate, gf, tm):
    n = h.shape[0]
    split = pa.shape[1] // tm
    return pl.pallas_call(
        functools.partial(_ple_final_kernel, split),
        out_shape=[jax.ShapeDtypeStruct((pa.shape[1], D_MODEL), F32), jax.ShapeDtypeStruct((pb.shape[1], D_MODEL), F32)],
        grid=(n // tm,),
        in_specs=[pl.BlockSpec((tm, D_MODEL), lambda i: (i, 0))] + _two_src_specs(tm, PLE_DIM, split, layer)
        + [pl.BlockSpec((1, D_MODEL), lambda i: (0, 0)),
           pl.BlockSpec((PLE_DIM, D_MODEL), lambda i: (0, 0)),
           pl.BlockSpec((D_MODEL, D_MODEL), lambda i: (0, 0)),
           pl.BlockSpec((1, D_MODEL), lambda i: (0, 0))],
        out_specs=_two_src_specs(tm, D_MODEL, split),
        compiler_params=_cparams(("arbitrary",)),
        name="ple_final",
    )(h, pa, pb, g, wup, wgate, gf)


def _reorder_w_in(w):
    sizes = (GLA_KW, GLA_KW, GLA_VW, GLA_RANK, GLA_VW, RET_KW, RET_KW, RET_VW, RET_VW,
             GDN_CONV_CH, GDN_H, GDN_H, GDN_VW)
    pts = np.cumsum(np.array(sizes))[:-1].tolist()
    gq, gk, gv, glr, gr, rq, rk, rv, rr, dqkv, da, db, dz = jnp.split(w, pts, axis=1)
    pad = jnp.zeros((w.shape[0], LANES - GLA_RANK - 2 * GDN_H), w.dtype)
    return jnp.concatenate([gq, gk, gv, gr, rq, rk, rv, rr, dqkv, dz, glr, da, db, pad], axis=1)


def _lane_row(vals, offset):
    return jnp.zeros((1, LANES), F32).at[0, offset:offset + vals.shape[0]].set(vals.astype(F32))


def kernel(x_prompt, x_sample, state_gla, state_ret, state_gdn, state_gdn_conv, p_prompt, p_sample, norm_mix, w_in, gla_w_gate_up, gla_b_gate, gla_norm, ret_norm, gdn_conv, gdn_a_log, gdn_dt_bias, gdn_norm, w_out, norm_ffn, ffn_w_gate, ffn_w_up, ffn_w_down, moe_router, moe_w_gate, moe_w_up, moe_w_down, ple_w_up, ple_norm, ple_w_gate, norm_final):
    depth = w_in.shape[0]
    Bp, Tp, _ = x_prompt.shape
    Bs, Ts, _ = x_sample.shape
    n_p, n_s = Bp * Tp, Bs * Ts
    tm = TOKEN_TILE
    xp = x_prompt.reshape(n_p, D_MODEL)
    xs = x_sample.reshape(n_s, D_MODEL)
    pp = p_prompt.reshape(depth, n_p, PLE_DIM)
    ps = p_sample.reshape(depth, n_s, PLE_DIM)

    outs_p = [[], [], [], []]
    outs_s = [[], [], [], []]
    h = None
    for i in range(depth):
        w_in_k = _reorder_w_in(w_in[i].astype(BF16))
        prm = dict(
            wgu=jnp.zeros((LANES, GLA_KW), F32).at[:GLA_RANK].set(gla_w_gate_up[i]).astype(BF16),
            gbias=gla_b_gate[i].reshape(1, GLA_KW).astype(F32),
            alog=_lane_row(gdn_a_log[i], SM_DA),
            dtb=_lane_row(gdn_dt_bias[i], SM_DA),
            glan=jnp.tile(gla_norm[i].astype(F32), GLA_H).reshape(1, GLA_VW),
            retn=jnp.tile(ret_norm[i].astype(F32), RET_H).reshape(1, RET_VW),
            gdnn=gdn_norm[i].astype(F32).reshape(1, GDN_DV),
            convw=gdn_conv[i].astype(F32),
        )
        if i == 0:
            z, h = _norm_proj2(xp, xs, norm_mix[i].reshape(1, D_MODEL), w_in_k, tm)
        else:
            z, h = _ple_norm_proj(h, pp, ps, i - 1, ple_norm[i - 1].reshape(1, D_MODEL),
                                  ple_w_up[i - 1].astype(BF16), ple_w_gate[i - 1].astype(BF16),
                                  norm_mix[i].reshape(1, D_MODEL), w_in_k, tm)
        o_p, gla_p, ret_p, gdn_p, conv_p = _mixer(
            z, 0, Bp, Tp, 0, True,
            jnp.zeros((Bp, GLA_H, GLA_DK, GLA_DV), F32), jnp.zeros((Bp, RET_H, RET_DK, RET_DV), F32),
            jnp.zeros((Bp, GDN_H, GDN_DK, GDN_DV), F32), jnp.zeros((Bp, GDN_CONV_W - 1, GDN_CONV_CH), F32), prm)
        o_s, gla_s, ret_s, gdn_s, conv_s = _mixer(
            z, n_p, Bs, Ts, PAST_LEN, False,
            state_gla[i].astype(F32), state_ret[i].astype(F32),
            state_gdn[i].astype(F32), state_gdn_conv[i].astype(F32), prm)
        for lst, val in zip(outs_p, (gla_p, ret_p, gdn_p, conv_p)):
            lst.append(val)
        for lst, val in zip(outs_s, (gla_s, ret_s, gdn_s, conv_s)):
            lst.append(val)
        wo = w_out[i].astype(BF16)
        j = i // 2
        if i % 2 == 0:
            h = _ffn(h, o_p, o_s, wo, norm_ffn[i].reshape(1, D_MODEL), ffn_w_gate[j].astype(BF16),
                     ffn_w_up[j].astype(BF16), ffn_w_down[j].astype(BF16), tm, FF_TILE)
        else:
            router = jnp.zeros((D_MODEL, LANES), F32).at[:, :N_EXPERTS].set(moe_router[j].astype(F32))
            h = _out_proj(h, o_p, o_s, wo, tm)
            h = _moe(h, norm_ffn[i].reshape(1, D_MODEL), router, moe_w_gate[j].astype(BF16),
                     moe_w_up[j].astype(BF16), moe_w_down[j].astype(BF16))

    last = depth - 1
    y_p, y_s = _ple_final(h, pp, ps, last, ple_norm[last].reshape(1, D_MODEL), ple_w_up[last].astype(BF16),
                          ple_w_gate[last].astype(BF16), norm_final.reshape(1, D_MODEL), tm)
    return (y_p.reshape(Bp, Tp, D_MODEL), y_s.reshape(Bs, Ts, D_MODEL),
            jnp.stack(outs_p[0]), jnp.stack(outs_p[1]), jnp.stack(outs_p[2]), jnp.stack(outs_p[3]),
            jnp.stack(outs_s[0]), jnp.stack(outs_s[1]), jnp.stack(outs_s[2]), jnp.stack(outs_s[3]))
```

```python
import functools
import math

import jax
import jax.numpy as jnp
import numpy as np
from jax import lax
from jax.experimental import pallas as pl
from jax.experimental.pallas import tpu as pltpu

F32 = jnp.float32
BF16 = jnp.bfloat16

D_MODEL = 1024
CHUNK = 64
PLE_DIM = 256
EPS = 1e-6
PAST_LEN = 4096

GLA_H, GLA_DK, GLA_DV, GLA_RANK, GLA_TAU = 4, 32, 64, 16, 16.0
RET_H, RET_DK, RET_DV = 4, 32, 64
ROPE_BASE = 10000.0
GDN_H, GDN_DK, GDN_DV, GDN_CONV_W = 4, 128, 128, 4
GLA_KW, GLA_VW = GLA_H * GLA_DK, GLA_H * GLA_DV
RET_KW, RET_VW = RET_H * RET_DK, RET_H * RET_DV
GDN_KW, GDN_VW = GDN_H * GDN_DK, GDN_H * GDN_DV
GDN_CONV_CH = 2 * GDN_KW + GDN_VW
D_FF = 3584
N_EXPERTS = 8

LANES = 128

Z_GQ, Z_GK, Z_GV, Z_GR = 0, 128, 256, 512
Z_RQ, Z_RK, Z_RV, Z_RR = 768, 896, 1024, 1280
Z_CONV = 1536
Z_DZ = Z_CONV + GDN_CONV_CH
Z_SMALL = Z_DZ + GDN_VW
Z_WIDTH = Z_SMALL + LANES
SM_DA, SM_DB = GLA_RANK, GLA_RANK + GDN_H

MIX_ROWS = 256
GDN_SUB = 128
TOKEN_TILE = 512
VMEM_LIMIT = 56 * 1024 * 1024


def _cparams(sem):
    return pltpu.CompilerParams(dimension_semantics=sem, vmem_limit_bytes=VMEM_LIMIT)


def _rms(x, g):
    return x * lax.rsqrt(jnp.mean(x * x, axis=-1, keepdims=True) + EPS) * g


def _bdot(a, b):
    return jnp.dot(a.astype(BF16), b.astype(BF16), preferred_element_type=F32)


def _bdot_t(a, b):
    return lax.dot_general(a.astype(BF16), b.astype(BF16), (((1,), (1,)), ((), ())),
                           preferred_element_type=F32)


def _bdot_tl(a, b):
    return lax.dot_general(a.astype(BF16), b.astype(BF16), (((0,), (0,)), ((), ())),
                           preferred_element_type=F32)


def _silu(x):
    return x * jax.nn.sigmoid(x)


def _split3(a):
    hi = a.astype(BF16)
    r1 = a - hi.astype(F32)
    mid = r1.astype(BF16)
    lo = (r1 - mid.astype(F32)).astype(BF16)
    return [hi, mid, lo]


def _sum3(x):
    return x[:, 0:LANES] + x[:, LANES:2 * LANES] + x[:, 2 * LANES:3 * LANES]


def _two_src_specs(tm, width, split, layer=None):
    if layer is None:
        return [pl.BlockSpec((tm, width), lambda i, *_: (jnp.minimum(i, split - 1), 0)),
                pl.BlockSpec((tm, width), lambda i, *_: (jnp.maximum(i - split, 0), 0))]
    return [pl.BlockSpec((None, tm, width), lambda i, *_: (layer, jnp.minimum(i, split - 1), 0)),
            pl.BlockSpec((None, tm, width), lambda i, *_: (layer, jnp.maximum(i - split, 0), 0))]


def _two_src_load(a_ref, b_ref, split):
    return jnp.where(pl.program_id(0) < split, a_ref[...], b_ref[...])

def _norm_proj2_kernel(split, xa_ref, xb_ref, g_ref, w_ref, z_ref, h_ref):
    x = _two_src_load(xa_ref, xb_ref, split)
    h_ref[...] = x
    z_ref[...] = jnp.dot(_rms(x, g_ref[...]).astype(BF16), w_ref[...], preferred_element_type=F32)


def _norm_proj2(xa, xb, g, w, tm):
    n = xa.shape[0] + xb.shape[0]
    split = xa.shape[0] // tm
    return pl.pallas_call(
        functools.partial(_norm_proj2_kernel, split),
        out_shape=[jax.ShapeDtypeStruct((n, w.shape[1]), F32), jax.ShapeDtypeStruct((n, D_MODEL), F32)],
        grid=(n // tm,),
        in_specs=_two_src_specs(tm, D_MODEL, split) + [pl.BlockSpec((1, D_MODEL), lambda i: (0, 0)),
                                                      pl.BlockSpec(w.shape, lambda i: (0, 0))],
        out_specs=[pl.BlockSpec((tm, w.shape[1]), lambda i: (i, 0)),
                   pl.BlockSpec((tm, D_MODEL), lambda i: (i, 0))],
        compiler_params=_cparams(("parallel",)),
        name="norm_proj_first",
    )(xa, xb, g, w)


def _ple_update(h, p, g_ref, wup_ref, wgate_ref):
    up = jnp.dot(p.astype(BF16), wup_ref[...], preferred_element_type=F32)
    gt = jnp.dot(_rms(h, g_ref[...]).astype(BF16), wgate_ref[...], preferred_element_type=F32)
    return h + up * jax.nn.sigmoid(gt)


def _ple_norm_proj_kernel(split, h_ref, pa_ref, pb_ref, gp_ref, wup_ref, wgate_ref, g_ref, w_ref, z_ref, hn_ref):
    hn = _ple_update(h_ref[...], _two_src_load(pa_ref, pb_ref, split), gp_ref, wup_ref, wgate_ref)
    hn_ref[...] = hn
    z_ref[...] = jnp.dot(_rms(hn, g_ref[...]).astype(BF16), w_ref[...], preferred_element_type=F32)


def _ple_norm_proj(h, pa, pb, layer, gp, wup, wgate, g, w, tm):
    n = h.shape[0]
    split = pa.shape[1] // tm
    return pl.pallas_call(
        functools.partial(_ple_norm_proj_kernel, split),
        out_shape=[jax.ShapeDtypeStruct((n, w.shape[1]), F32), jax.ShapeDtypeStruct((n, D_MODEL), F32)],
        grid=(n // tm,),
        in_specs=[pl.BlockSpec((tm, D_MODEL), lambda i: (i, 0))] + _two_src_specs(tm, PLE_DIM, split, layer)
        + [pl.BlockSpec((1, D_MODEL), lambda i: (0, 0)),
           pl.BlockSpec((PLE_DIM, D_MODEL), lambda i: (0, 0)),
           pl.BlockSpec((D_MODEL, D_MODEL), lambda i: (0, 0)),
           pl.BlockSpec((1, D_MODEL), lambda i: (0, 0)),
           pl.BlockSpec(w.shape, lambda i: (0, 0))],
        out_specs=[pl.BlockSpec((tm, w.shape[1]), lambda i: (i, 0)),
                   pl.BlockSpec((tm, D_MODEL), lambda i: (i, 0))],
        compiler_params=_cparams(("parallel",)),
        name="ple_norm_proj",
    )(h, pa, pb, gp, wup, wgate, g, w)


def _mixer_kernel(L, n_seg, seq,
                  z_ref, cos_ref, sin_ref, retd_ref, reteb_ref, retebl_ref, retsdec_ref,
                  sgla0_ref, sret0_ref, sgdn0_ref, conv0_ref,
                  wgu_ref, gbias_ref, alog_ref, dtb_ref, glan_ref, retn_ref, gdnn_ref, convw_ref,
                  tri_ref, de_ref, lmask_ref, gn64_ref, blkmask_ref, hmk_ref, hmv_ref,
                  o_ref, sgla_out_ref, sret_out_ref, sgdn_ref, convo_ref,
                  xpad_ref, sgla_ref, sret_ref):
    R = L * n_seg
    n_lev = int(math.log2(L))
    c = pl.program_id(1)
    n_state = sgla_ref.shape[0]

    def load_blockdiag(src_ref, dst_ref, dk, dv):
        for s in range(n_state):
            for h in range(GLA_H):
                pieces = []
                if h > 0:
                    pieces.append(jnp.zeros((dk, h * dv), F32))
                pieces.append(src_ref[s, h])
                if h < GLA_H - 1:
                    pieces.append(jnp.zeros((dk, (GLA_H - 1 - h) * dv), F32))
                dst_ref[s, h * dk:(h + 1) * dk, :] = jnp.concatenate(pieces, axis=1)

    def store_blockdiag(src_ref, dst_ref, dk, dv):
        for s in range(n_state):
            for h in range(GLA_H):
                dst_ref[s, h] = src_ref[s, h * dk:(h + 1) * dk, h * dv:(h + 1) * dv]

    @pl.when(c == 0)
    def _():
        load_blockdiag(sgla0_ref, sgla_ref, GLA_DK, GLA_DV)
        load_blockdiag(sret0_ref, sret_ref, RET_DK, RET_DV)
        sgdn_ref[...] = sgdn0_ref[...]
        convo_ref[...] = conv0_ref[...]

    blkmask = blkmask_ref[...]
    hmk = hmk_ref[...]
    hmv = hmv_ref[...]
    small = z_ref[:, Z_SMALL:Z_SMALL + LANES]

    def slot(j):
        return 0 if seq else j

    def rows(j):
        return slice(j * L, (j + 1) * L)

    la = jax.nn.log_sigmoid(_bdot(small, wgu_ref[...]) + gbias_ref[...]) * (1.0 / GLA_TAU)
    g_all = -jnp.exp(alog_ref[...]) * jax.nn.softplus(small + dtb_ref[...])
    beta_all = jax.nn.sigmoid(small)
    la_parts = _split3(la)
    la3 = jnp.concatenate(la_parts, axis=1)
    g3 = jnp.concatenate(_split3(g_all), axis=1)
    cs = jnp.dot(tri_ref[...], jnp.concatenate([la3, g3], axis=1), preferred_element_type=F32)
    b = _sum3(cs[:, 0:3 * LANES])
    bg = _sum3(cs[:, 3 * LANES:6 * LANES])
    de = jnp.dot(de_ref[...], jnp.concatenate(la_parts[0:2], axis=1), preferred_element_type=F32)
    ede = jnp.exp(de[:, 0:LANES] + de[:, LANES:2 * LANES])

    def group_norm_gate(o, g, gate):
        sq = o * o
        hi = sq.astype(BF16)
        lo = (sq - hi.astype(F32)).astype(BF16)
        gn = gn64_ref[...]
        ms = (jnp.dot(hi, gn, preferred_element_type=F32) + jnp.dot(lo, gn, preferred_element_type=F32))
        return o * lax.rsqrt(ms + EPS) * g * _silu(gate)

    def intra_scores(q_list, kblk_list, masks):
        att = None
        for qm, kb, mk in zip(q_list, kblk_list, masks):
            term = mk * _bdot_t(qm, kb)
            att = term if att is None else att + term
        return att

    q = z_ref[:, Z_GQ:Z_GQ + GLA_KW] * (GLA_DK ** -0.5)
    k = z_ref[:, Z_GK:Z_GK + GLA_KW]
    v = z_ref[:, Z_GV:Z_GV + GLA_VW]
    qe = q * jnp.exp(b)
    b_t = b.T
    pending = []
    state = {"gla": sgla_ref[0] if seq else None, "ret": sret_ref[0] if seq else None}
    o_gla_parts, o_ret_parts = [], []

    def gla_chunk(j, q=q, k=k, v=v):
        r = rows(j)
        qj, kj, vj = q[r], k[r], v[r]
        kblk0 = jnp.concatenate([kj] * GLA_H, axis=0) * hmk
        q_list, kb_list, masks = [], [], []
        for lv in range(n_lev):
            elv = ede[lv * R + j * L:lv * R + (j + 1) * L]
            q_list.append(qj * elv)
            kb_list.append(kblk0 * jnp.concatenate([elv] * GLA_H, axis=0))
            masks.append(lmask_ref[lv])
        q_list.append(qj)
        kb_list.append(kblk0)
        masks.append(lmask_ref[n_lev])
        att = intra_scores(q_list, kb_list, masks)
        vblk = jnp.concatenate([vj] * GLA_H, axis=0) * hmv
        last = (j + 1) * L - 1
        bl = b[last:last + 1, :]
        kv = blkmask * _bdot_tl(kj * jnp.exp(bl - b[r]), vj)
        decc = jnp.exp(b_t[:, last:last + 1])
        s_gla = state["gla"] if seq else sgla_ref[j]
        o_gla_parts.append(_bdot(att, vblk) + _bdot(qe[r], s_gla))
        s_gla = s_gla * decc + kv
        if seq:
            state["gla"] = s_gla
        else:
            sgla_ref[j] = s_gla

    def gla_finish():
        if seq:
            sgla_ref[0] = state["gla"]
        o_gla = jnp.concatenate(o_gla_parts, axis=0)
        o_ref[:, 0:GLA_VW] = group_norm_gate(o_gla, glan_ref[...], z_ref[:, Z_GR:Z_GR + GLA_VW]).astype(o_ref.dtype)

    cos = cos_ref[...]
    sin = sin_ref[...]
    lane = lax.broadcasted_iota(jnp.int32, (R, LANES), 1)
    first_half = (lane % RET_DK) < (RET_DK // 2)

    def rot(xx):
        sw = jnp.where(first_half, pltpu.roll(xx, LANES - RET_DK // 2, 1), pltpu.roll(xx, RET_DK // 2, 1))
        return xx * cos + sw * sin

    rq = rot(z_ref[:, Z_RQ:Z_RQ + RET_KW]) * (RET_DK ** -0.5)
    rk = rot(z_ref[:, Z_RK:Z_RK + RET_KW])
    rv = z_ref[:, Z_RV:Z_RV + RET_VW]
    retd = retd_ref[...]
    reteb = reteb_ref[...]
    retebl = retebl_ref[...]
    retsdec = retsdec_ref[...]

    def ret_chunk(j):
        r = rows(j)
        qj, kj, vj = rq[r], rk[r], rv[r]
        kblk = jnp.concatenate([kj] * RET_H, axis=0) * hmk
        att = _bdot_t(qj, kblk) * retd
        vblk = jnp.concatenate([vj] * RET_H, axis=0) * hmv
        kv = blkmask * _bdot_tl(kj * retebl, vj)
        s_ret = state["ret"] if seq else sret_ref[j]
        o_ret_parts.append(_bdot(att, vblk) + _bdot(qj * reteb, s_ret))
        s_ret = s_ret * retsdec + kv
        if seq:
            state["ret"] = s_ret
        else:
            sret_ref[j] = s_ret

    def ret_finish():
        if seq:
            sret_ref[0] = state["ret"]
        o_ret = jnp.concatenate(o_ret_parts, axis=0)
        o_ref[:, GLA_VW:GLA_VW + RET_VW] = group_norm_gate(
            o_ret, retn_ref[...], z_ref[:, Z_RR:Z_RR + RET_VW]).astype(o_ref.dtype)

    for j in range(n_seg):
        pending.append(functools.partial(gla_chunk, j))
        pending.append(functools.partial(ret_chunk, j))

    n_cs = 1 if seq else n_seg
    Lc = R // n_cs
    cw = convw_ref[...]
    xin = z_ref[:, Z_CONV:Z_CONV + GDN_CONV_CH]
    c_parts = []
    for s in range(n_cs):
        xs = xin[s * Lc:(s + 1) * Lc]
        xpad_ref[s, 8 - (GDN_CONV_W - 1):8, :] = convo_ref[s]
        xpad_ref[s, 8:8 + Lc, :] = xs
        convo_ref[s] = xs[Lc - (GDN_CONV_W - 1):Lc, :]
        cacc = xs * cw[GDN_CONV_W - 1:GDN_CONV_W, :]
        for jj in range(GDN_CONV_W - 1):
            sh = GDN_CONV_W - 1 - jj
            cacc = cacc + xpad_ref[s, 8 - sh:8 - sh + Lc, :] * cw[jj:jj + 1, :]
        c_parts.append(cacc)
    cact = _silu(c_parts[0] if n_cs == 1 else jnp.concatenate(c_parts, axis=0))

    n_sub = R // GDN_SUB
    row = lax.broadcasted_iota(jnp.int32, (GDN_SUB, GDN_SUB), 0)
    col = lax.broadcasted_iota(jnp.int32, (GDN_SUB, GDN_SUB), 1)
    same = lax.shift_right_logical(row, n_lev) == lax.shift_right_logical(col, n_lev)
    tril = same & (row >= col)
    strict = same & (row > col)
    eye_f = (row == col).astype(F32)
    bg_t = bg.T
    beta_t = beta_all.T
    gdnn = gdnn_ref[...]

    def l2n(xx):
        return xx * lax.rsqrt(jnp.sum(xx * xx, axis=-1, keepdims=True) + EPS)

    HS = range(GDN_H)
    PS = [(h, sp) for h in HS for sp in range(n_sub)]

    def sub(a, sp):
        return a[sp * GDN_SUB:(sp + 1) * GDN_SUB]

    qh = [l2n(cact[:, h * GDN_DK:(h + 1) * GDN_DK]) * (GDN_DK ** -0.5) for h in HS]
    kh = [l2n(cact[:, GDN_KW + h * GDN_DK:GDN_KW + (h + 1) * GDN_DK]) for h in HS]
    vh = [cact[:, 2 * GDN_KW + h * GDN_DV:2 * GDN_KW + (h + 1) * GDN_DV] for h in HS]
    bcol = [bg[:, SM_DA + h:SM_DA + h + 1] for h in HS]
    beta_col = [beta_all[:, SM_DB + h:SM_DB + h + 1] for h in HS]
    ebc = [jnp.exp(bcol[h]) for h in HS]
    khb = [kh[h].astype(BF16) for h in HS]
    beta_row = {(h, sp): beta_t[SM_DB + h:SM_DB + h + 1, sp * GDN_SUB:(sp + 1) * GDN_SUB] for h, sp in PS}
    dec = {(h, sp): jnp.exp(jnp.where(
        tril, sub(bcol[h], sp) - bg_t[SM_DA + h:SM_DA + h + 1, sp * GDN_SUB:(sp + 1) * GDN_SUB], -jnp.inf))
        for h, sp in PS}
    m = {(h, sp): _bdot_t(sub(khb[h], sp), sub(khb[h], sp)) * jnp.where(strict, dec[h, sp], 0.0) * beta_row[h, sp]
         for h, sp in PS}
    tinv = {p: eye_f - m[p] for p in PS}
    pw = {p: m[p].astype(BF16) for p in PS}
    per_stage = -(-len(pending) // (n_lev - 1))
    span = 2
    while span < L:
        pw = {p: jnp.dot(pw[p], pw[p], preferred_element_type=F32).astype(BF16) for p in PS}
        tinv = {p: tinv[p] + jnp.dot(tinv[p].astype(BF16), pw[p], preferred_element_type=F32) for p in PS}
        for thunk in pending[:per_stage]:
            thunk()
        del pending[:per_stage]
        span *= 2
    assert not pending
    gla_finish()
    ret_finish()
    store_blockdiag(sgla_ref, sgla_out_ref, GLA_DK, GLA_DV)
    store_blockdiag(sret_ref, sret_out_ref, RET_DK, RET_DV)
    x1 = {(h, sp): _bdot(tinv[h, sp], jnp.concatenate([sub(vh[h], sp), sub(ebc[h] * kh[h], sp)], axis=1)).astype(BF16)
          for h, sp in PS}
    qk = {(h, sp): _bdot_t(sub(qh[h], sp), sub(khb[h], sp)) * dec[h, sp] * beta_row[h, sp] for h, sp in PS}
    x2 = {p: _bdot(qk[p], x1[p]) for p in PS}
    qeff = {(h, sp): sub(ebc[h] * qh[h], sp) - x2[h, sp][:, GDN_DV:2 * GDN_DV] for h, sp in PS}
    x3 = [[None] * n_seg for h in HS]
    blg = [[None] * n_seg for h in HS]
    for j in range(n_seg):
        r = rows(j)
        sp, lo_r = divmod(j * L, GDN_SUB)
        last = (j + 1) * L - 1
        for h in HS:
            blg[h][j] = bcol[h][last:last + 1, :]
            kt = kh[h][r] * (jnp.exp(blg[h][j] - bcol[h][r]) * beta_col[h][r])
            x3[h][j] = _bdot_tl(kt, x1[h, sp][lo_r:lo_r + L])
    s_h = [sgdn_ref[0, h] if seq else None for h in HS]
    o_parts = [[] for h in HS]
    for j in range(n_seg):
        sp, lo_r = divmod(j * L, GDN_SUB)
        for h in HS:
            if not seq:
                s_h[h] = sgdn_ref[j, h]
            y = _bdot(jnp.concatenate([qeff[h, sp][lo_r:lo_r + L], x3[h][j][:, GDN_DV:2 * GDN_DV]], axis=0),
                      s_h[h])
            o_parts[h].append(x2[h, sp][lo_r:lo_r + L, 0:GDN_DV] + y[0:L])
            s_h[h] = jnp.exp(blg[h][j]) * s_h[h] - y[L:L + GDN_DK] + x3[h][j][:, 0:GDN_DV]
            if not seq:
                sgdn_ref[j, h] = s_h[h]
    for h in HS:
        if seq:
            sgdn_ref[0, h] = s_h[h]
        oh = jnp.concatenate(o_parts[h], axis=0)
        oh = _rms(oh, gdnn) * _silu(z_ref[:, Z_DZ + h * GDN_DV:Z_DZ + (h + 1) * GDN_DV])
        lo = GLA_VW + RET_VW + h * GDN_DV
        o_ref[:, lo:lo + GDN_DV] = oh.astype(o_ref.dtype)


def _head_block_mask(rows_per_head, cols_per_head, heads):
    r = np.arange(rows_per_head * heads)[:, None] // rows_per_head
    c = np.arange(cols_per_head * heads)[None, :] // cols_per_head
    return (r == c).astype(np.float32)


def _pack_consts(L, n_seg):
    R = L * n_seg
    t = np.arange(R)
    tl = t % L
    base = t - tl
    jj = np.arange(R)[None, :]
    tri = ((jj // L) == (t[:, None] // L)) & (jj <= t[:, None])
    tt = np.arange(L)[:, None]
    ss = np.arange(L)[None, :]
    de_list, masks = [], []
    m = L // 2
    while m >= 1:
        second = (tl // m) % 2 == 1
        ref = base + (tl // (2 * m)) * 2 * m + m - 1
        d_m = second[:, None] & (jj > ref[:, None]) & (jj <= t[:, None])
        e_m = (~second)[:, None] & (jj > t[:, None]) & (jj <= ref[:, None])
        de_list.append(d_m | e_m)
        mk = (tt // (2 * m) == ss // (2 * m)) & ((tt // m) % 2 == 1) & ((ss // m) % 2 == 0)
        masks.append(np.tile(mk, (1, GLA_H)))
        m //= 2
    masks.append(np.tile(np.eye(L, dtype=bool), (1, GLA_H)))
    de = np.concatenate(de_list, axis=0)
    return (jnp.asarray(tri, BF16), jnp.asarray(de, BF16), jnp.asarray(np.stack(masks), F32))


def _mixer(zf, row0, B, T, pos0, seq, sgla0, sret0, sgdn0, conv0, prm):
    blk0 = row0 // MIX_ROWS
    if seq:
        L = CHUNK
        n_seg = MIX_ROWS // L
        grid = (B, T // MIX_ROWS)
        n_state = 1
    else:
        L = T
        n_seg = MIX_ROWS // L
        grid = (B // n_seg, 1)
        n_state = n_seg
    R = MIX_ROWS
    steps_per_b = grid[1]
    half = RET_DK // 2
    inv = ROPE_BASE ** (-jnp.arange(half, dtype=F32) / half)
    ang = (pos0 + jnp.arange(T, dtype=jnp.int32)).astype(F32)[:, None] * inv[None, :]
    cos_h = jnp.concatenate([jnp.cos(ang), jnp.cos(ang)], axis=1)
    sin_h = jnp.concatenate([-jnp.sin(ang), jnp.sin(ang)], axis=1)
    cos_t = jnp.tile(cos_h, (1, RET_H))
    sin_t = jnp.tile(sin_h, (1, RET_H))
    if not seq:
        cos_t = jnp.tile(cos_t, (n_seg, 1))
        sin_t = jnp.tile(sin_t, (n_seg, 1))
    log_gamma = jnp.log(1.0 - jnp.exp2(-5.0 - jnp.arange(RET_H, dtype=F32)))
    tpos = jnp.arange(L, dtype=F32)
    bret = (tpos[:, None] + 1.0) * log_gamma[None, :]
    dmat = tpos[:, None] - tpos[None, :]
    retd = jnp.where(dmat[None] >= 0, jnp.exp(dmat[None] * log_gamma[:, None, None]), 0.0)
    retd = jnp.transpose(retd, (1, 0, 2)).reshape(L, RET_H * L)
    reteb = jnp.repeat(jnp.exp(bret), RET_DK, axis=1)
    retebl = jnp.repeat(jnp.exp(bret[L - 1:L] - bret), RET_DK, axis=1)
    retsdec = jnp.broadcast_to(jnp.repeat(jnp.exp(bret[L - 1]), RET_DK)[:, None], (RET_KW, RET_VW))

    tri, de, lmask = _pack_consts(L, n_seg)
    gn64 = jnp.asarray(_head_block_mask(GLA_DV, GLA_DV, GLA_H) / GLA_DV, BF16)
    blkmask = jnp.asarray(_head_block_mask(GLA_DK, GLA_DV, GLA_H), F32)
    hmk = jnp.asarray(_head_block_mask(L, RET_DK, RET_H), F32)
    hmv = jnp.asarray(_head_block_mask(L, RET_DV, RET_H), F32)

    def full(a):
        nd = a.ndim
        return pl.BlockSpec(a.shape, lambda b, c: (0,) * nd)

    def per_b(a):
        nd = a.ndim
        return pl.BlockSpec((n_state,) + a.shape[1:], lambda b, c: (b,) + (0,) * (nd - 1))

    if seq:
        tab_spec = pl.BlockSpec((R, LANES), lambda b, c: (c, 0))
    else:
        tab_spec = pl.BlockSpec((R, LANES), lambda b, c: (0, 0))
    consts = [retd, reteb, retebl, retsdec]
    states = [sgla0, sret0, sgdn0, conv0]
    params = [prm["wgu"], prm["gbias"], prm["alog"], prm["dtb"], prm["glan"], prm["retn"], prm["gdnn"],
              prm["convw"], tri, de, lmask, gn64, blkmask, hmk, hmv]
    in_specs = ([pl.BlockSpec((R, Z_WIDTH), lambda b, c: (blk0 + b * steps_per_b + c, 0)), tab_spec, tab_spec]
                + [full(a) for a in consts] + [per_b(a) for a in states] + [full(a) for a in params])
    out_shape = [jax.ShapeDtypeStruct((B * T, D_MODEL), BF16)] + [
        jax.ShapeDtypeStruct(a.shape, F32) for a in states]
    out_specs = [pl.BlockSpec((R, D_MODEL), lambda b, c: (b * steps_per_b + c, 0))] + [per_b(a) for a in states]
    n_cs = 1 if seq else n_seg
    return pl.pallas_call(
        functools.partial(_mixer_kernel, L, n_seg, seq),
        out_shape=out_shape,
        grid=grid,
        in_specs=in_specs,
        out_specs=out_specs,
        scratch_shapes=[pltpu.VMEM((n_cs, R // n_cs + 8, GDN_CONV_CH), F32),
                        pltpu.VMEM((n_state, GLA_KW, GLA_VW), F32), pltpu.VMEM((n_state, RET_KW, RET_VW), F32)],
        compiler_params=_cparams(("parallel", "arbitrary")),
        name="mixer",
    )(zf, cos_t, sin_t, *consts, *states, *params)


def _out_proj_kernel(split, h_ref, oa_ref, ob_ref, w_ref, out_ref):
    o = _two_src_load(oa_ref, ob_ref, split)
    out_ref[...] = h_ref[...] + jnp.dot(o, w_ref[...], preferred_element_type=F32)


def _out_proj(h, oa, ob, w, tm):
    n = h.shape[0]
    split = oa.shape[0] // tm
    return pl.pallas_call(
        functools.partial(_out_proj_kernel, split),
        out_shape=jax.ShapeDtypeStruct((n, D_MODEL), F32),
        grid=(n // tm,),
        in_specs=[pl.BlockSpec((tm, D_MODEL), lambda i: (i, 0))] + _two_src_specs(tm, D_MODEL, split)
        + [pl.BlockSpec((D_MODEL, D_MODEL), lambda i: (0, 0))],
        out_specs=pl.BlockSpec((tm, D_MODEL), lambda i: (i, 0)),
        compiler_params=_cparams(("parallel",)),
        name="out_proj",
    )(h, oa, ob, w)


FF_TILE = 1792
FF_SUB = 256


def _swiglu_partial(v, wg_ref, wu_ref, wd_ref):
    parts = []
    tf = wg_ref.shape[1]
    for c0 in range(0, tf, FF_SUB):
        c1 = min(c0 + FF_SUB, tf)
        a = jnp.dot(v, wg_ref[:, c0:c1], preferred_element_type=F32)
        u = jnp.dot(v, wu_ref[:, c0:c1], preferred_element_type=F32)
        parts.append((_silu(a) * u).astype(BF16))
    return jnp.dot(jnp.concatenate(parts, axis=1), wd_ref[...], preferred_element_type=F32)


def _ffn_kernel(split, h_ref, oa_ref, ob_ref, wo_ref, g_ref, wg_ref, wu_ref, wd_ref, out_ref, v_ref, acc_ref):
    f = pl.program_id(1)

    @pl.when(f == 0)
    def _():
        h2 = h_ref[...] + jnp.dot(_two_src_load(oa_ref, ob_ref, split), wo_ref[...], preferred_element_type=F32)
        v_ref[...] = _rms(h2, g_ref[...]).astype(BF16)
        acc_ref[...] = h2

    acc_ref[...] += _swiglu_partial(v_ref[...], wg_ref, wu_ref, wd_ref)

    @pl.when(f == pl.num_programs(1) - 1)
    def _():
        out_ref[...] = acc_ref[...]


def _ffn(h, oa, ob, wo, g, wg, wu, wd, tm, tf):
    n = h.shape[0]
    split = oa.shape[0] // tm
    return pl.pallas_call(
        functools.partial(_ffn_kernel, split),
        out_shape=jax.ShapeDtypeStruct((n, D_MODEL), F32),
        grid=(n // tm, D_FF // tf),
        in_specs=[pl.BlockSpec((tm, D_MODEL), lambda i, f: (i, 0))] + _two_src_specs(tm, D_MODEL, split)
        + [pl.BlockSpec((D_MODEL, D_MODEL), lambda i, f: (0, 0)),
           pl.BlockSpec((1, D_MODEL), lambda i, f: (0, 0)),
           pl.BlockSpec((D_MODEL, tf), lambda i, f: (0, f)),
           pl.BlockSpec((D_MODEL, tf), lambda i, f: (0, f)),
           pl.BlockSpec((tf, D_MODEL), lambda i, f: (f, 0))],
        out_specs=pl.BlockSpec((tm, D_MODEL), lambda i, f: (i, 0)),
        scratch_shapes=[pltpu.VMEM((tm, D_MODEL), BF16), pltpu.VMEM((tm, D_MODEL), F32)],
        compiler_params=_cparams(("parallel", "arbitrary")),
        name="ffn",
    )(h, oa, ob, wo, g, wg, wu, wd)


MOE_TM = 512
MOE_BLK = 512
MOE_TF = FF_TILE
SEG_ALIGN = 16
SEL_ROWS = 128


def _moe_rows(n):
    n_tiles = n // MOE_TM
    bound = 2 * n + n_tiles * N_EXPERTS * (SEG_ALIGN - 1) + N_EXPERTS * (MOE_BLK - 1)
    return -(-bound // MOE_BLK) * MOE_BLK


def _route_kernel(h_ref, g_ref, r_ref, v_ref, gate_ref, memb_ref, cnt_ref):
    vf = _rms(h_ref[...], g_ref[...])
    v_hi = vf.astype(BF16)
    v_ref[...] = v_hi
    v_mid = (vf - v_hi.astype(F32)).astype(BF16)
    logits = jnp.dot(jnp.concatenate([v_hi, v_hi, v_mid], axis=1), r_ref[...],
                     preferred_element_type=F32)
    lane = lax.broadcasted_iota(jnp.int32, logits.shape, 1)
    neg = jnp.float32(-jnp.inf)
    lg = jnp.where(lane < N_EXPERTS, logits, neg)
    m1 = jnp.max(lg, axis=1, keepdims=True)
    i1 = jnp.min(jnp.where(lg == m1, lane, LANES), axis=1, keepdims=True)
    lg2 = jnp.where(lane == i1, neg, lg)
    m2 = jnp.max(lg2, axis=1, keepdims=True)
    i2 = jnp.min(jnp.where(lg2 == m2, lane, LANES), axis=1, keepdims=True)
    e2 = jnp.exp(m2 - m1)
    den = 1.0 + e2
    gate_ref[...] = jnp.where(lane == i1, 1.0 / den, 0.0) + jnp.where(lane == i2, e2 / den, 0.0)
    memb = jnp.where((lane == i1) | (lane == i2), 1.0, 0.0)
    memb_ref[...] = memb
    cnt_ref[...] = jnp.broadcast_to(jnp.sum(memb, axis=0, keepdims=True), cnt_ref.shape)


def _moe_route(h, g, router):
    n = h.shape[0]
    tm = MOE_TM
    r_hi = router.astype(BF16)
    r_mid = (router - r_hi.astype(F32)).astype(BF16)
    r3 = jnp.concatenate([r_hi, r_mid, r_hi], axis=0)
    return pl.pallas_call(
        _route_kernel,
        out_shape=[jax.ShapeDtypeStruct((n, D_MODEL), BF16), jax.ShapeDtypeStruct((n, LANES), F32),
                   jax.ShapeDtypeStruct((n, LANES), F32), jax.ShapeDtypeStruct((n // tm, 8, LANES), F32)],
        grid=(n // tm,),
        in_specs=[pl.BlockSpec((tm, D_MODEL), lambda i: (i, 0)),
                  pl.BlockSpec((1, D_MODEL), lambda i: (0, 0)),
                  pl.BlockSpec((3 * D_MODEL, LANES), lambda i: (0, 0))],
        out_specs=[pl.BlockSpec((tm, D_MODEL), lambda i: (i, 0)),
                   pl.BlockSpec((tm, LANES), lambda i: (i, 0)),
                   pl.BlockSpec((tm, LANES), lambda i: (i, 0)),
                   pl.BlockSpec((None, 8, LANES), lambda i: (i, 0, 0))],
        compiler_params=_cparams(("parallel",)),
        name="moe_route",
    )(h, g, r3)


def _seg_pad(c):
    return (c + (SEG_ALIGN - 1)) // SEG_ALIGN * SEG_ALIGN


def _gather_kernel(base_ref, cnt_ref, v_ref, memb_ref, tri_ref, xs_in_ref, xs_ref, stg_ref, sem, nout_ref):
    del xs_in_ref
    i = pl.program_id(0)
    n_steps = pl.num_programs(0)
    par = i % 2
    unit = SEG_ALIGN

    def unit_copy(p, src_row, dst_row):
        return pltpu.make_async_copy(stg_ref.at[p, pl.ds(src_row, unit)], xs_ref.at[pl.ds(dst_row, unit)], sem.at[p])

    def wait_units(p, count):
        def body(u, carry):
            unit_copy(p, 0, 0).wait()
            return carry
        lax.fori_loop(0, count, body, 0)

    @pl.when(i >= 2)
    def _():
        wait_units(par, nout_ref[par])

    tm = v_ref.shape[0]
    memb = memb_ref[...]
    rank = jnp.dot(tri_ref[...], memb.astype(BF16), preferred_element_type=F32)
    rank_t = rank.T[0:N_EXPERTS, :]
    is_m = memb.T[0:N_EXPERTS, :] > 0.0
    sub = lax.broadcasted_iota(jnp.int32, (N_EXPERTS, tm), 0)
    pos = rank_t
    so = jnp.int32(0)
    for e in range(N_EXPERTS):
        pos = pos + jnp.where(sub == e, so.astype(F32), 0.0)
        so = so + _seg_pad(cnt_ref[i, e])
    stg_rows = stg_ref.shape[1]
    pos_a = jnp.min(jnp.where(is_m, pos, float(stg_rows)), axis=0, keepdims=True)
    pos_b = jnp.max(jnp.where(is_m, pos, -1.0), axis=0, keepdims=True)
    ridx = lax.broadcasted_iota(jnp.int32, (stg_rows, tm), 0).astype(F32)
    sel = jnp.where((ridx == pos_a) | (ridx == pos_b), 1.0, 0.0).astype(BF16)
    stg_ref[par] = jnp.dot(sel, v_ref[...], preferred_element_type=F32).astype(BF16)

    so = jnp.int32(0)
    for e in range(N_EXPERTS):
        cp = _seg_pad(cnt_ref[i, e])
        dst = base_ref[i, e]

        def send(u, carry, so=so, dst=dst):
            unit_copy(par, pl.multiple_of(so + u * unit, unit), pl.multiple_of(dst + u * unit, unit)).start()
            return carry

        lax.fori_loop(0, cp // unit, send, 0)
        so = so + cp
    nout_ref[par] = so // unit

    @pl.when(i == n_steps - 1)
    def _():
        wait_units(par, nout_ref[par])

        @pl.when(i >= 1)
        def _():
            wait_units(1 - par, nout_ref[1 - par])


def _moe_gather(base, cnt, v, memb, tri, xs_rows):
    n = v.shape[0]
    tm = MOE_TM
    stg_rows = 2 * tm + N_EXPERTS * SEG_ALIGN
    xs0 = jnp.zeros((xs_rows, D_MODEL), BF16)
    grid_spec = pltpu.PrefetchScalarGridSpec(
        num_scalar_prefetch=2,
        grid=(n // tm,),
        in_specs=[pl.BlockSpec((tm, D_MODEL), lambda i, b, c: (i, 0)),
                  pl.BlockSpec((tm, LANES), lambda i, b, c: (i, 0)),
                  pl.BlockSpec((tm, tm), lambda i, b, c: (0, 0)),
                  pl.BlockSpec(memory_space=pl.ANY)],
        out_specs=pl.BlockSpec(memory_space=pl.ANY),
        scratch_shapes=[pltpu.VMEM((2, stg_rows, D_MODEL), BF16), pltpu.SemaphoreType.DMA((2,)),
                        pltpu.SMEM((2,), jnp.int32)],
    )
    return pl.pallas_call(
        _gather_kernel,
        out_shape=jax.ShapeDtypeStruct((xs_rows, D_MODEL), BF16),
        grid_spec=grid_spec,
        input_output_aliases={5: 0},
        compiler_params=_cparams(("arbitrary",)),
        name="moe_gather",
    )(base, cnt, v, memb, tri, xs0)


def _expert_kernel(bexp_ref, bval_ref, xs_ref, wg_ref, wu_ref, wd_ref, ys_ref, acc_ref):
    del bexp_ref
    k = pl.program_id(0)
    f = pl.program_id(1)

    last = pl.num_programs(1) - 1
    valid = bval_ref[k] > 0

    @pl.when(valid)
    def _():
        part = _swiglu_partial(xs_ref[...], wg_ref, wu_ref, wd_ref)

        @pl.when(f == 0)
        def _():
            acc_ref[...] = part

        @pl.when((f > 0) & (f < last))
        def _():
            acc_ref[...] += part

        @pl.when(f == last)
        def _():
            ys_ref[...] = acc_ref[...] + part

    @pl.when(jnp.logical_not(valid) & (f == last))
    def _():
        ys_ref[...] = jnp.zeros_like(ys_ref)


def _moe_experts(bexp, bval, xs, wg, wu, wd):
    n_blk = xs.shape[0] // MOE_BLK
    n_f = D_FF // MOE_TF
    assert n_f >= 2

    def f_idx(k, f, bval):
        return jnp.where(bval[k] > 0, f, n_f - 1)

    grid_spec = pltpu.PrefetchScalarGridSpec(
        num_scalar_prefetch=2,
        grid=(n_blk + 1, n_f),
        in_specs=[pl.BlockSpec((MOE_BLK, D_MODEL), lambda k, f, be, bv: (jnp.minimum(k, n_blk - 1), 0)),
                  pl.BlockSpec((None, D_MODEL, MOE_TF), lambda k, f, be, bv: (be[k], 0, f_idx(k, f, bv))),
                  pl.BlockSpec((None, D_MODEL, MOE_TF), lambda k, f, be, bv: (be[k], 0, f_idx(k, f, bv))),
                  pl.BlockSpec((None, MOE_TF, D_MODEL), lambda k, f, be, bv: (be[k], f_idx(k, f, bv), 0))],
        out_specs=pl.BlockSpec((MOE_BLK, D_MODEL), lambda k, f, be, bv: (k, 0)),
        scratch_shapes=[pltpu.VMEM((MOE_BLK, D_MODEL), F32)],
    )
    return pl.pallas_call(
        _expert_kernel,
        out_shape=jax.ShapeDtypeStruct(((n_blk + 1) * MOE_BLK, D_MODEL), F32),
        grid_spec=grid_spec,
        compiler_params=_cparams(("arbitrary", "arbitrary")),
        name="moe_experts",
    )(bexp, bval, xs, wg, wu, wd)


def _combine_kernel(base_ref, cnt_ref, h_ref, gate_ref, memb_ref, tri_ref, ys_ref, out_ref, ybuf_ref, acc_ref, sem):
    i = pl.program_id(0)
    n_steps = pl.num_programs(0)
    par = i % 2
    tm = h_ref.shape[0]

    def chunk_copy(p, src_row, slot_row):
        return pltpu.make_async_copy(ys_ref.at[pl.ds(src_row, SEL_ROWS)], ybuf_ref.at[p, pl.ds(slot_row, SEL_ROWS)],
                                     sem.at[p])

    def n_chunks(c):
        return (c + (SEL_ROWS - 1)) // SEL_ROWS

    def fetch(step, p):
        slot = jnp.int32(0)
        for e in range(N_EXPERTS):
            nq = n_chunks(cnt_ref[step, e])
            src = base_ref[step, e]

            def body(qq, carry, slot=slot, src=src):
                chunk_copy(p, pl.multiple_of(src + qq * SEL_ROWS, SEG_ALIGN),
                           pl.multiple_of((slot + qq) * SEL_ROWS, SEL_ROWS)).start()
                return carry

            lax.fori_loop(0, nq, body, 0)
            slot = slot + nq
        return slot

    @pl.when(i == 0)
    def _():
        fetch(0, 0)

    @pl.when(i + 1 < n_steps)
    def _():
        fetch(i + 1, 1 - par)

    total = jnp.int32(0)
    for e in range(N_EXPERTS):
        total = total + n_chunks(cnt_ref[i, e])

    def wait_body(u, carry):
        chunk_copy(par, 0, 0).wait()
        return carry

    lax.fori_loop(0, total, wait_body, 0)

    memb = memb_ref[...]
    gates = gate_ref[...]
    rank = jnp.dot(tri_ref[...], memb.astype(BF16), preferred_element_type=F32)
    cidx = lax.broadcasted_iota(jnp.int32, (tm, SEL_ROWS), 1).astype(F32)
    acc_ref[...] = h_ref[...]

    slot = jnp.int32(0)
    for e in range(N_EXPERTS):
        nq = n_chunks(cnt_ref[i, e])
        rcol = rank[:, e:e + 1]
        mcol = memb[:, e:e + 1]
        gcol = gates[:, e:e + 1]

        def chunk(qq, carry, slot=slot, rcol=rcol, mcol=mcol, gcol=gcol):
            y = ybuf_ref[par, pl.ds(pl.multiple_of((slot + qq) * SEL_ROWS, SEL_ROWS), SEL_ROWS), :]
            sel = jnp.where((cidx + (qq * SEL_ROWS).astype(F32) == rcol) & (mcol > 0.0), 1.0, 0.0).astype(BF16)
            yh = y.astype(BF16)
            yl = (y - yh.astype(F32)).astype(BF16)
            got = jnp.dot(jnp.concatenate([sel, sel], axis=1), jnp.concatenate([yh, yl], axis=0),
                          preferred_element_type=F32)
            acc_ref[...] += gcol * got
            return carry

        lax.fori_loop(0, nq, chunk, 0)
        slot = slot + nq
    out_ref[...] = acc_ref[...]


def _moe_combine(base, cnt, h, gates, memb, tri, ys):
    n = h.shape[0]
    tm = MOE_TM
    max_chunks = 2 * tm // SEL_ROWS + N_EXPERTS
    grid_spec = pltpu.PrefetchScalarGridSpec(
        num_scalar_prefetch=2,
        grid=(n // tm,),
        in_specs=[pl.BlockSpec((tm, D_MODEL), lambda i, b, c: (i, 0)),
                  pl.BlockSpec((tm, LANES), lambda i, b, c: (i, 0)),
                  pl.BlockSpec((tm, LANES), lambda i, b, c: (i, 0)),
                  pl.BlockSpec((tm, tm), lambda i, b, c: (0, 0)),
                  pl.BlockSpec(memory_space=pl.ANY)],
        out_specs=pl.BlockSpec((tm, D_MODEL), lambda i, b, c: (i, 0)),
        scratch_shapes=[pltpu.VMEM((2, max_chunks * SEL_ROWS, D_MODEL), F32), pltpu.VMEM((tm, D_MODEL), F32),
                        pltpu.SemaphoreType.DMA((2,))],
    )
    return pl.pallas_call(
        _combine_kernel,
        out_shape=jax.ShapeDtypeStruct((n, D_MODEL), F32),
        grid_spec=grid_spec,
        compiler_params=_cparams(("arbitrary",)),
        name="moe_combine",
    )(base, cnt, h, gates, memb, tri, ys)


def _moe(h, g, router, wg, wu, wd):
    n = h.shape[0]
    xs_rows = _moe_rows(n)
    n_blk = xs_rows // MOE_BLK
    v, gates, memb, cnt_f = _moe_route(h, g, router)
    cnt = cnt_f[:, 0, :N_EXPERTS].astype(jnp.int32)
    cp = _seg_pad(cnt)
    exp_rows = -(-jnp.sum(cp, axis=0) // MOE_BLK) * MOE_BLK
    exp_end = jnp.cumsum(exp_rows)
    exp_start = exp_end - exp_rows
    base = (exp_start[None, :] + jnp.cumsum(cp, axis=0) - cp).astype(jnp.int32)
    blk_row = jnp.arange(n_blk + 1, dtype=jnp.int32) * MOE_BLK
    bval = (blk_row < exp_end[-1]).astype(jnp.int32)
    bexp = jnp.minimum(jnp.sum((blk_row[:, None] >= exp_end[None, :]).astype(jnp.int32), axis=1), N_EXPERTS - 1)
    last_valid = jnp.maximum(jnp.sum(bval) - 1, 0)
    bexp = jnp.where(bval > 0, bexp, bexp[last_valid]).astype(jnp.int32)
    tri = jnp.asarray(np.tril(np.ones((MOE_TM, MOE_TM), np.float32), -1), BF16)
    xs = _moe_gather(base, cnt, v, memb, tri, xs_rows)
    ys = _moe_experts(bexp, bval, xs, wg, wu, wd)
    return _moe_combine(base, cnt, h, gates, memb, tri, ys)


def _ple_final_kernel(split, h_ref, pa_ref, pb_ref, g_ref, wup_ref, wgate_ref, gf_ref, ya_ref, yb_ref):
    hn = _ple_update(h_ref[...], _two_src_load(pa_ref, pb_ref, split), g_ref, wup_ref, wgate_ref)
    y = _rms(hn, gf_ref[...])
    i = pl.program_id(0)

    @pl.when(i < split)
    def _():
        ya_ref[...] = y

    @pl.when(i >= split)
    def _():
        yb_ref[...] = y


def _ple_final(h, pa, pb, layer, g, wup, wgate, gf, tm):
    n = h.shape[0]
    split = pa.shape[1] // tm
    return pl.pallas_call(
        functools.partial(_ple_final_kernel, split),
        out_shape=[jax.ShapeDtypeStruct((pa.shape[1], D_MODEL), F32), jax.ShapeDtypeStruct((pb.shape[1], D_MODEL), F32)],
        grid=(n // tm,),
        in_specs=[pl.BlockSpec((tm, D_MODEL), lambda i: (i, 0))] + _two_src_specs(tm, PLE_DIM, split, layer)
        + [pl.BlockSpec((1, D_MODEL), lambda i: (0, 0)),
           pl.BlockSpec((PLE_DIM, D_MODEL), lambda i: (0, 0)),
           pl.BlockSpec((D_MODEL, D_MODEL), lambda i: (0, 0)),
           pl.BlockSpec((1, D_MODEL), lambda i: (0, 0))],
        out_specs=_two_src_specs(tm, D_MODEL, split),
        compiler_params=_cparams(("arbitrary",)),
        name="ple_final",
    )(h, pa, pb, g, wup, wgate, gf)


def _reorder_w_in(w):
    sizes = (GLA_KW, GLA_KW, GLA_VW, GLA_RANK, GLA_VW, RET_KW, RET_KW, RET_VW, RET_VW,
             GDN_CONV_CH, GDN_H, GDN_H, GDN_VW)
    pts = np.cumsum(np.array(sizes))[:-1].tolist()
    gq, gk, gv, glr, gr, rq, rk, rv, rr, dqkv, da, db, dz = jnp.split(w, pts, axis=1)
    pad = jnp.zeros((w.shape[0], LANES - GLA_RANK - 2 * GDN_H), w.dtype)
    return jnp.concatenate([gq, gk, gv, gr, rq, rk, rv, rr, dqkv, dz, glr, da, db, pad], axis=1)


def _lane_row(vals, offset):
    return jnp.zeros((1, LANES), F32).at[0, offset:offset + vals.shape[0]].set(vals.astype(F32))


def kernel(x_prompt, x_sample, state_gla, state_ret, state_gdn, state_gdn_conv, p_prompt, p_sample, norm_mix, w_in, gla_w_gate_up, gla_b_gate, gla_norm, ret_norm, gdn_conv, gdn_a_log, gdn_dt_bias, gdn_norm, w_out, norm_ffn, ffn_w_gate, ffn_w_up, ffn_w_down, moe_router, moe_w_gate, moe_w_up, moe_w_down, ple_w_up, ple_norm, ple_w_gate, norm_final):
    depth = w_in.shape[0]
    Bp, Tp, _ = x_prompt.shape
    Bs, Ts, _ = x_sample.shape
    n_p, n_s = Bp * Tp, Bs * Ts
    tm = TOKEN_TILE
    xp = x_prompt.reshape(n_p, D_MODEL)
    xs = x_sample.reshape(n_s, D_MODEL)
    pp = p_prompt.reshape(depth, n_p, PLE_DIM)
    ps = p_sample.reshape(depth, n_s, PLE_DIM)

    outs_p = [[], [], [], []]
    outs_s = [[], [], [], []]
    h = None
    for i in range(depth):
        w_in_k = _reorder_w_in(w_in[i].astype(BF16))
        prm = dict(
            wgu=jnp.zeros((LANES, GLA_KW), F32).at[:GLA_RANK].set(gla_w_gate_up[i]).astype(BF16),
            gbias=gla_b_gate[i].reshape(1, GLA_KW).astype(F32),
            alog=_lane_row(gdn_a_log[i], SM_DA),
            dtb=_lane_row(gdn_dt_bias[i], SM_DA),
            glan=jnp.tile(gla_norm[i].astype(F32), GLA_H).reshape(1, GLA_VW),
            retn=jnp.tile(ret_norm[i].astype(F32), RET_H).reshape(1, RET_VW),
            gdnn=gdn_norm[i].astype(F32).reshape(1, GDN_DV),
            convw=gdn_conv[i].astype(F32),
        )
        if i == 0:
            z, h = _norm_proj2(xp, xs, norm_mix[i].reshape(1, D_MODEL), w_in_k, tm)
        else:
            z, h = _ple_norm_proj(h, pp, ps, i - 1, ple_norm[i - 1].reshape(1, D_MODEL),
                                  ple_w_up[i - 1].astype(BF16), ple_w_gate[i - 1].astype(BF16),
                                  norm_mix[i].reshape(1, D_MODEL), w_in_k, tm)
        o_p, gla_p, ret_p, gdn_p, conv_p = _mixer(
            z, 0, Bp, Tp, 0, True,
            jnp.zeros((Bp, GLA_H, GLA_DK, GLA_DV), F32), jnp.zeros((Bp, RET_H, RET_DK, RET_DV), F32),
            jnp.zeros((Bp, GDN_H, GDN_DK, GDN_DV), F32), jnp.zeros((Bp, GDN_CONV_W - 1, GDN_CONV_CH), F32), prm)
        o_s, gla_s, ret_s, gdn_s, conv_s = _mixer(
            z, n_p, Bs, Ts, PAST_LEN, False,
            state_gla[i].astype(F32), state_ret[i].astype(F32),
            state_gdn[i].astype(F32), state_gdn_conv[i].astype(F32), prm)
        for lst, val in zip(outs_p, (gla_p, ret_p, gdn_p, conv_p)):
            lst.append(val)
        for lst, val in zip(outs_s, (gla_s, ret_s, gdn_s, conv_s)):
            lst.append(val)
        wo = w_out[i].astype(BF16)
        j = i // 2
        if i % 2 == 0:
            h = _ffn(h, o_p, o_s, wo, norm_ffn[i].reshape(1, D_MODEL), ffn_w_gate[j].astype(BF16),
                     ffn_w_up[j].astype(BF16), ffn_w_down[j].astype(BF16), tm, FF_TILE)
        else:
            router = jnp.zeros((D_MODEL, LANES), F32).at[:, :N_EXPERTS].set(moe_router[j].astype(F32))
            h = _out_proj(h, o_p, o_s, wo, tm)
            h = _moe(h, norm_ffn[i].reshape(1, D_MODEL), router, moe_w_gate[j].astype(BF16),
                     moe_w_up[j].astype(BF16), moe_w_down[j].astype(BF16))

    last = depth - 1
    y_p, y_s = _ple_final(h, pp, ps, last, ple_norm[last].reshape(1, D_MODEL), ple_w_up[last].astype(BF16),
                          ple_w_gate[last].astype(BF16), norm_final.reshape(1, D_MODEL), tm)
    return (y_p.reshape(Bp, Tp, D_MODEL), y_s.reshape(Bs, Ts, D_MODEL),
            jnp.stack(outs_p[0]), jnp.stack(outs_p[1]), jnp.stack(outs_p[2]), jnp.stack(outs_p[3]),
            jnp.stack(outs_s[0]), jnp.stack(outs_s[1]), jnp.stack(outs_s[2]), jnp.stack(outs_s[3]))
```
